```python
import jax, jax.numpy as jnp
from jax import lax
import numpy as np

D_MODEL = 1024
BATCH = 8
SEQ = 2048
DEPTH = 2

GRID_W = 64
CTX_LEN = 256
N_MIXERS = 4
GROUP_W = D_MODEL // N_MIXERS
HEAD_DIM = 64
Q_BLOCK = 128
ROPE_THETA = 10000.0
NORM_EPS = 1e-6
A_HEADS = GROUP_W // HEAD_DIM
A_KV_HEADS = 2
LRU_W = GROUP_W
LRU_BLOCKS = 4
CONV_W = 4
CONV_LEFT = (CONV_W - 1) // 2
LRU_C = 8.0
C_HEADS = GROUP_W // HEAD_DIM
C_KV_HEADS = 2
WINDOW = 128
D_HEADS = 4
QK_NOPE = 64
QK_ROPE = 32
V_DIM = GROUP_W // D_HEADS
Q_LORA = 192
KV_LORA = 128
MLA_SCALE = (QK_NOPE + QK_ROPE) ** -0.5
N_EXPERTS = 64
TOP_K = 8
N_EXPERT_GROUPS = 8
TOPK_GROUPS = 4
D_EXPERT = 256
ROUTED_SCALE = 2.5
MOE_BLOCK = 128

IN_SPLITS = (A_HEADS * HEAD_DIM, A_KV_HEADS * HEAD_DIM, A_KV_HEADS * HEAD_DIM,
             LRU_W, LRU_W,
             C_HEADS * HEAD_DIM, C_KV_HEADS * HEAD_DIM, C_KV_HEADS * HEAD_DIM,
             Q_LORA, KV_LORA, QK_ROPE)
IN_COLS = sum(IN_SPLITS)

kernel_name = 'hybrid_parallel_heads_moe_dit'


def rms_norm(x, g):
    xf = x.astype(jnp.float32)
    y = xf * lax.rsqrt(jnp.mean(xf * xf, axis=-1, keepdims=True) + NORM_EPS)
    return (y * g.astype(jnp.float32)).astype(x.dtype)


def modulate(x, shift, scale):
    return x * (1 + scale) + shift


def heads(t, n):
    return t.reshape(t.shape[:-1] + (n, t.shape[-1] // n))


def merge_heads(t):
    return t.reshape(t.shape[:-2] + (t.shape[-2] * t.shape[-1],))


def split_columns(proj):
    return jnp.split(proj, np.cumsum(IN_SPLITS)[:-1].tolist(), axis=-1)


def rope_tables(row, col, rot_dim):
    n = rot_dim // 4
    inv = ROPE_THETA ** (-jnp.arange(n, dtype=jnp.float32) / n)
    ang = jnp.concatenate([row[:, None] * inv, col[:, None] * inv], axis=-1)
    return jnp.cos(ang), jnp.sin(ang)


def apply_rope(x, cos, sin):
    half = x.shape[-1] // 2
    xf = x.astype(jnp.float32)
    x1, x2 = xf[..., :half], xf[..., half:]
    c, s = cos[None, :, None, :], sin[None, :, None, :]
    return jnp.concatenate([x1 * c - x2 * s, x1 * s + x2 * c], axis=-1).astype(x.dtype)


def blocked_attention(q, k, v, scale, sink=None):
    B, S, Hq, d = q.shape
    Hkv = k.shape[2]
    rep = Hq // Hkv
    nb = S // Q_BLOCK
    qb = jnp.moveaxis(q.reshape(B, nb, Q_BLOCK, Hkv, rep, d), 1, 0)

    def one_block(qblk):
        s = jnp.einsum('bqgrd,bkgd->bgrqk', qblk, k).astype(jnp.float32) * scale
        if sink is None:
            p = jax.nn.softmax(s, axis=-1)
        else:
            s_sink = jnp.broadcast_to(sink.astype(jnp.float32).reshape(Hkv, rep)[None, :, :, None, None],
                                      s.shape[:-1] + (1,))
            p = jax.nn.softmax(jnp.concatenate([s_sink, s], axis=-1), axis=-1)[..., 1:]
        return jnp.einsum('bgrqk,bkgd->bqgrd', p.astype(v.dtype), v)

    o = lax.map(one_block, qb)
    return jnp.moveaxis(o, 0, 1).reshape(B, S, Hq, v.shape[-1])


def window_attention(q, k, v, kc, vc, sink, scale):
    B, S, Hq, d = q.shape
    Hkv = k.shape[2]
    rep = Hq // Hkv
    nb = S // Q_BLOCK
    C = kc.shape[1]
    qb = q.reshape(B, nb, Q_BLOCK, Hkv, rep, d)

    def band(t):
        tb = t.reshape(B, nb, Q_BLOCK, Hkv, t.shape[-1])
        tp = jnp.pad(tb, ((0, 0), (1, 1), (0, 0), (0, 0), (0, 0)))
        return jnp.concatenate([tp[:, :-2], tp[:, 1:-1], tp[:, 2:]], axis=2)

    kw, vw = band(k), band(v)
    qi = jnp.arange(Q_BLOCK)
    ki = jnp.arange(3 * Q_BLOCK) - Q_BLOCK
    rel = ki[None, :] - qi[:, None]
    kabs = jnp.arange(nb)[:, None] * Q_BLOCK + ki[None, :]
    allowed = (jnp.abs(rel) <= WINDOW)[None] & ((kabs >= 0) & (kabs < S))[:, None, :]
    s_win = jnp.einsum('bnqgrd,bnkgd->bngrqk', qb, kw).astype(jnp.float32) * scale
    s_win = jnp.where(allowed[None, :, None, None], s_win, -jnp.inf)
    s_ctx = jnp.einsum('bnqgrd,bcgd->bngrqc', qb, kc).astype(jnp.float32) * scale
    s_sink = jnp.broadcast_to(sink.astype(jnp.float32).reshape(Hkv, rep)[None, None, :, :, None, None],
                              s_ctx.shape[:-1] + (1,))
    p = jax.nn.softmax(jnp.concatenate([s_sink, s_ctx, s_win], axis=-1), axis=-1)
    p_ctx = p[..., 1:1 + C].astype(v.dtype)
    p_win = p[..., 1 + C:].astype(v.dtype)
    o = (jnp.einsum('bngrqc,bcgd->bnqgrd', p_ctx, vc)
         + jnp.einsum('bngrqk,bnkgd->bnqgrd', p_win, vw))
    return o.reshape(B, S, Hq, v.shape[-1])


def depthwise_conv(x, w, b):
    L = x.shape[1]
    xp = jnp.pad(x, ((0, 0), (CONV_LEFT, CONV_W - 1 - CONV_LEFT), (0, 0)))
    out = b
    for k in range(CONV_W):
        out = out + xp[:, k:k + L] * w[k]
    return out


def linear_scan(a, b, h0):
    b = b.at[:, 0].add(a[:, 0] * h0)

    def combine(left, right):
        return left[0] * right[0], right[0] * left[1] + right[1]

    return lax.associative_scan(combine, (a, b), axis=1)[1]


def rglru(x, h0, wa, ba, wi, bi, lam):
    B, L, W = x.shape
    xf = x.astype(jnp.float32)
    xg = xf.reshape(B, L, LRU_BLOCKS, W // LRU_BLOCKS)
    r = jax.nn.sigmoid(jnp.einsum('blhi,hij->blhj', xg, wa.astype(jnp.float32)).reshape(B, L, W)
                       + ba.astype(jnp.float32))
    ig = jax.nn.sigmoid(jnp.einsum('blhi,hij->blhj', xg, wi.astype(jnp.float32)).reshape(B, L, W)
                        + bi.astype(jnp.float32))
    log_a = -LRU_C * r * jax.nn.softplus(-lam.astype(jnp.float32))
    a = jnp.exp(log_a)
    b = jnp.sqrt(-jnp.expm1(2.0 * log_a)) * (ig * xf)
    return linear_scan(a, b, h0)


def rglru_bidirectional(x_lat, x_ctx, p):
    B, _, W = x_lat.shape
    h_lat, h_ctx = 0.0, 0.0
    for d in range(2):
        prm = (p['lru_wa'][d], p['lru_ba'][d], p['lru_wi'][d], p['lru_bi'][d], p['lru_lambda'][d])
        xc = x_ctx if d == 0 else jnp.flip(x_ctx, axis=1)
        xl = x_lat if d == 0 else jnp.flip(x_lat, axis=1)
        hc = rglru(xc, jnp.zeros((B, W), jnp.float32), *prm)
        hl = rglru(xl, hc[:, -1], *prm)
        if d == 1:
            hc, hl = jnp.flip(hc, axis=1), jnp.flip(hl, axis=1)
        h_lat = h_lat + hl
        h_ctx = h_ctx + hc
    return h_lat, h_ctx


def mla_queries(cq, p, rope):
    q = heads(rms_norm(cq, p['d_q_norm']) @ p['d_w_uq'], D_HEADS)
    if rope is None:
        return q
    return jnp.concatenate([q[..., :QK_NOPE], apply_rope(q[..., QK_NOPE:], *rope)], axis=-1)


def mla_keys_values(ckv, kr, p, rope):
    kv = heads(rms_norm(ckv, p['d_kv_norm']) @ p['d_w_ukv'], D_HEADS)
    k_nope, v = kv[..., :QK_NOPE], kv[..., QK_NOPE:]
    kr = kr[:, :, None, :]
    if rope is not None:
        kr = apply_rope(kr, *rope)
    k = jnp.concatenate([k_nope, jnp.broadcast_to(kr, k_nope.shape[:-1] + (QK_ROPE,))], axis=-1)
    return k, v


def token_mixers(u, uc, p, rope_hd, rope_r, need_ctx):
    aq, ak, av, bx, bg, cq, ck, cv, dq, dkv, dkr = split_columns(u @ p['w_in'])
    aqc, akc, avc, bxc, bgc, cqc, ckc, cvc, dqc, dkvc, dkrc = split_columns(uc @ p['w_in'])
    sc_hd = HEAD_DIM ** -0.5

    k_a_ctx = rms_norm(heads(akc, A_KV_HEADS), p['a_k_norm'])
    v_a_ctx = heads(avc, A_KV_HEADS)
    k_a = jnp.concatenate([k_a_ctx, apply_rope(rms_norm(heads(ak, A_KV_HEADS), p['a_k_norm']), *rope_hd)], axis=1)
    v_a = jnp.concatenate([v_a_ctx, heads(av, A_KV_HEADS)], axis=1)
    q_a = apply_rope(rms_norm(heads(aq, A_HEADS), p['a_q_norm']), *rope_hd)
    y_a = merge_heads(blocked_attention(q_a, k_a, v_a, sc_hd))

    x_b = depthwise_conv(bx, p['conv_w'], p['conv_b'])
    x_b_ctx = depthwise_conv(bxc, p['conv_w'], p['conv_b'])
    h_b, h_b_ctx = rglru_bidirectional(x_b, x_b_ctx, p)
    y_b = (h_b * jax.nn.gelu(bg.astype(jnp.float32))).astype(u.dtype)

    k_c_ctx, v_c_ctx = heads(ckc, C_KV_HEADS), heads(cvc, C_KV_HEADS)
    y_c = merge_heads(window_attention(apply_rope(heads(cq, C_HEADS), *rope_hd),
                                       apply_rope(heads(ck, C_KV_HEADS), *rope_hd),
                                       heads(cv, C_KV_HEADS), k_c_ctx, v_c_ctx, p['c_sink'], sc_hd))

    k_d_ctx, v_d_ctx = mla_keys_values(dkvc, dkrc, p, None)
    k_d, v_d = mla_keys_values(dkv, dkr, p, rope_r)
    y_d = merge_heads(blocked_attention(mla_queries(dq, p, rope_r),
                                        jnp.concatenate([k_d_ctx, k_d], axis=1),
                                        jnp.concatenate([v_d_ctx, v_d], axis=1), MLA_SCALE))

    y = jnp.concatenate([y_a, y_b, y_c, y_d], axis=-1) @ p['w_out']
    if not need_ctx:
        return y, None

    y_a_ctx = merge_heads(blocked_attention(rms_norm(heads(aqc, A_HEADS), p['a_q_norm']), k_a_ctx, v_a_ctx, sc_hd))
    y_b_ctx = (h_b_ctx * jax.nn.gelu(bgc.astype(jnp.float32))).astype(uc.dtype)
    y_c_ctx = merge_heads(blocked_attention(heads(cqc, C_HEADS), k_c_ctx, v_c_ctx, sc_hd, sink=p['c_sink']))
    y_d_ctx = merge_heads(blocked_attention(mla_queries(dqc, p, None), k_d_ctx, v_d_ctx, MLA_SCALE))
    y_ctx = jnp.concatenate([y_a_ctx, y_b_ctx, y_c_ctx, y_d_ctx], axis=-1) @ p['w_out']
    return y, y_ctx


def swiglu(x, wg, wu, wd):
    return (jax.nn.silu(x @ wg) * (x @ wu)) @ wd


def moe_ffn(u, p):
    T, D = u.shape
    scores = jax.nn.sigmoid((u @ p['router_w']).astype(jnp.float32))
    sel = scores + p['router_bias'].astype(jnp.float32)
    per_group = N_EXPERTS // N_EXPERT_GROUPS
    group_score = lax.top_k(sel.reshape(T, N_EXPERT_GROUPS, per_group), 2)[0].sum(-1)
    _, top_groups = lax.top_k(group_score, TOPK_GROUPS)
    group_mask = jnp.any(top_groups[:, :, None] == jnp.arange(N_EXPERT_GROUPS)[None, None, :], axis=1)
    expert_mask = jnp.repeat(group_mask, per_group, axis=1)
    _, idx = lax.top_k(jnp.where(expert_mask, sel, -jnp.inf), TOP_K)
    w = jnp.take_along_axis(scores, idx, axis=1)
    w = w / jnp.sum(w, axis=-1, keepdims=True) * ROUTED_SCALE

    A = T * TOP_K
    flat_e = idx.reshape(A)
    order = jnp.argsort(flat_e)
    e_sorted = flat_e[order]
    counts = jnp.bincount(flat_e, length=N_EXPERTS)
    padded = (counts + MOE_BLOCK - 1) // MOE_BLOCK * MOE_BLOCK
    pad_end = jnp.cumsum(padded)
    pad_start = pad_end - padded
    start = jnp.cumsum(counts) - counts
    dest = pad_start[e_sorted] + jnp.arange(A) - start[e_sorted]
    n_blocks = -(-A // MOE_BLOCK) + N_EXPERTS
    n_rows = n_blocks * MOE_BLOCK
    row_tok = jnp.zeros((n_rows,), jnp.int32).at[dest].set((order // TOP_K).astype(jnp.int32))
    row_w = jnp.zeros((n_rows,), jnp.float32).at[dest].set(w.reshape(A)[order])
    block_expert = jnp.minimum(jnp.searchsorted(pad_end, jnp.arange(n_blocks) * MOE_BLOCK, side='right'),
                               N_EXPERTS - 1)

    def expert_block(acc, blk):
        rows = blk * MOE_BLOCK
        tok = lax.dynamic_slice(row_tok, (rows,), (MOE_BLOCK,))
        rw = lax.dynamic_slice(row_w, (rows,), (MOE_BLOCK,))
        e = block_expert[blk]
        y = swiglu(u[tok], p['w_gate'][e], p['w_up'][e], p['w_down'][e])
        return acc.at[tok].add((y * rw[:, None]).astype(acc.dtype)), None

    routed, _ = lax.scan(expert_block, jnp.zeros_like(u), jnp.arange(n_blocks))
    return routed + swiglu(u, p['sh_gate'], p['sh_up'], p['sh_down'])


def layer_forward(x, cx, mod_lat, mod_ctx, p, rope_hd, rope_r, need_ctx):
    B, S, D = x.shape
    sh1, sc1, g1, sh2, sc2, g2 = jnp.split(mod_lat[:, None, :], 6, axis=-1)
    csh1, csc1, cg1, csh2, csc2, cg2 = jnp.split(mod_ctx, 6, axis=-1)
    u = modulate(rms_norm(x, p['g_pre_mix']), sh1, sc1)
    uc = modulate(rms_norm(cx, p['g_pre_mix']), csh1, csc1)
    y, y_ctx = token_mixers(u, uc, p, rope_hd, rope_r, need_ctx)
    x = x + g1 * rms_norm(y, p['g_post_mix'])
    v = modulate(rms_norm(x, p['g_pre_ffn']), sh2, sc2).reshape(B * S, D)
    if need_ctx:
        cx = cx + cg1 * rms_norm(y_ctx, p['g_post_mix'])
        vc = modulate(rms_norm(cx, p['g_pre_ffn']), csh2, csc2).reshape(-1, D)
        f = moe_ffn(jnp.concatenate([v, vc], axis=0), p)
        f_lat = f[:B * S]
        cx = cx + cg2 * rms_norm(f[B * S:].reshape(cx.shape), p['g_post_ffn'])
    else:
        f_lat = moe_ffn(v, p)
    x = x + g2 * rms_norm(f_lat.reshape(B, S, D), p['g_post_ffn'])
    return x, cx


def setup_inputs(seed: int = 0) -> dict:
    key = jax.random.key(seed)
    ks = jax.random.split(key, 40)
    f32 = jnp.float32
    L, D = DEPTH, D_MODEL
    BW = LRU_W // LRU_BLOCKS

    def nrm(i, shape, scale):
        return jax.random.normal(ks[i], shape, f32) * scale

    def gain(i, shape):
        return 1.0 + nrm(i, shape, 0.05)

    s = jax.random.uniform(ks[19], (L, 2, LRU_W), f32, 0.9, 0.999) ** (1.0 / LRU_C)
    return {
        'x': nrm(0, (BATCH, SEQ, D), 1.0),
        'c': nrm(1, (BATCH, D), 1.0),
        'ctx': nrm(2, (BATCH, CTX_LEN, D), 1.0),
        'c_ctx': nrm(3, (D,), 1.0),
        'w_mod': nrm(4, (L, D, 6 * D), 0.5 * D ** -0.5),
        'b_mod': nrm(5, (L, 6 * D), 0.02),
        'g_pre_mix': gain(6, (L, D)),
        'g_post_mix': gain(7, (L, D)),
        'g_pre_ffn': gain(8, (L, D)),
        'g_post_ffn': gain(9, (L, D)),
        'w_in': nrm(10, (L, D, IN_COLS), D ** -0.5),
        'a_q_norm': gain(11, (L, HEAD_DIM)),
        'a_k_norm': gain(12, (L, HEAD_DIM)),
        'conv_w': nrm(13, (L, CONV_W, LRU_W), CONV_W ** -0.5),
        'conv_b': nrm(14, (L, LRU_W), 0.02),
        'lru_wa': nrm(15, (L, 2, LRU_BLOCKS, BW, BW), BW ** -0.5),
        'lru_ba': nrm(16, (L, 2, LRU_W), 0.02),
        'lru_wi': nrm(17, (L, 2, LRU_BLOCKS, BW, BW), BW ** -0.5),
        'lru_bi': nrm(18, (L, 2, LRU_W), 0.02),
        'lru_lambda': jnp.log(s) - jnp.log1p(-s),
        'c_sink': nrm(20, (L, C_HEADS), 0.5),
        'd_q_norm': gain(21, (L, Q_LORA)),
        'd_w_uq': nrm(22, (L, Q_LORA, D_HEADS * (QK_NOPE + QK_ROPE)), Q_LORA ** -0.5),
        'd_kv_norm': gain(23, (L, KV_LORA)),
        'd_w_ukv': nrm(24, (L, KV_LORA, D_HEADS * (QK_NOPE + V_DIM)), KV_LORA ** -0.5),
        'w_out': nrm(25, (L, D, D), D ** -0.5),
        'router_w': nrm(26, (L, D, N_EXPERTS), D ** -0.5),
        'router_bias': nrm(27, (L, N_EXPERTS), 0.01),
        'w_gate': nrm(28, (L, N_EXPERTS, D, D_EXPERT), D ** -0.5),
        'w_up': nrm(29, (L, N_EXPERTS, D, D_EXPERT), D ** -0.5),
        'w_down': nrm(30, (L, N_EXPERTS, D_EXPERT, D), D_EXPERT ** -0.5),
        'sh_gate': nrm(31, (L, D, D_EXPERT), D ** -0.5),
        'sh_up': nrm(32, (L, D, D_EXPERT), D ** -0.5),
        'sh_down': nrm(33, (L, D_EXPERT, D), D_EXPERT ** -0.5),
    }


def reference(x, c, ctx, c_ctx, w_mod, b_mod, g_pre_mix, g_post_mix, g_pre_ffn, g_post_ffn,
              w_in, a_q_norm, a_k_norm, conv_w, conv_b, lru_wa, lru_ba, lru_wi, lru_bi, lru_lambda,
              c_sink, d_q_norm, d_w_uq, d_kv_norm, d_w_ukv, w_out,
              router_w, router_bias, w_gate, w_up, w_down, sh_gate, sh_up, sh_down):
    S = x.shape[1]
    ROWS = S // GRID_W
    row = jnp.repeat(jnp.arange(ROWS, dtype=jnp.float32), GRID_W)
    col = jnp.tile(jnp.arange(GRID_W, dtype=jnp.float32), ROWS)
    rope_hd = rope_tables(row, col, HEAD_DIM)
    rope_r = rope_tables(row, col, QK_ROPE)
    c_act = jax.nn.silu(c)
    cctx_act = jax.nn.silu(c_ctx)
    cx = ctx
    for l in range(DEPTH):
        p = {
            'g_pre_mix': g_pre_mix[l], 'g_post_mix': g_post_mix[l],
            'g_pre_ffn': g_pre_ffn[l], 'g_post_ffn': g_post_ffn[l],
            'w_in': w_in[l], 'a_q_norm': a_q_norm[l], 'a_k_norm': a_k_norm[l],
            'conv_w': conv_w[l], 'conv_b': conv_b[l],
            'lru_wa': lru_wa[l], 'lru_ba': lru_ba[l], 'lru_wi': lru_wi[l], 'lru_bi': lru_bi[l],
            'lru_lambda': lru_lambda[l], 'c_sink': c_sink[l],
            'd_q_norm': d_q_norm[l], 'd_w_uq': d_w_uq[l], 'd_kv_norm': d_kv_norm[l], 'd_w_ukv': d_w_ukv[l],
            'w_out': w_out[l], 'router_w': router_w[l], 'router_bias': router_bias[l],
            'w_gate': w_gate[l], 'w_up': w_up[l], 'w_down': w_down[l],
            'sh_gate': sh_gate[l], 'sh_up': sh_up[l], 'sh_down': sh_down[l],
        }
        mod_lat = c_act @ w_mod[l] + b_mod[l]
        mod_ctx = cctx_act @ w_mod[l] + b_mod[l]
        x, cx = layer_forward(x, cx, mod_lat, mod_ctx, p, rope_hd, rope_r, l < DEPTH - 1)
    return x
```

```python
import functools

import jax
import jax.numpy as jnp
from jax import lax
from jax.experimental import pallas as pl
from jax.experimental.pallas import tpu as pltpu

F32 = jnp.float32
BF16 = jnp.bfloat16

GRID_W = 64
HEAD_DIM = 64
ROPE_THETA = 10000.0
NORM_EPS = 1e-6
WINDOW = 128
Q_BLOCK = 128
D_HEADS = 4
QK_NOPE = 64
QK_ROPE = 32
LRU_C = 8.0
CONV_W = 4
TOP_K = 8
N_EXPERT_GROUPS = 8
TOPK_GROUPS = 4
ROUTED_SCALE = 2.5

LANES = 128
VMEM_LIMIT = 56 * 1024 * 1024
ROW_TILE = 256
MOE_ROWS = 1152

NEG_INF = float("-inf")


def _params(*sem):
    return pltpu.CompilerParams(dimension_semantics=sem, vmem_limit_bytes=VMEM_LIMIT)


def _rms(x, gain):
    return x * lax.rsqrt(jnp.mean(x * x, axis=-1, keepdims=True) + NORM_EPS) * gain


def _dot(a, b):
    return jnp.dot(a, b, preferred_element_type=F32)


def _dot_t(a, b):
    return lax.dot_general(a, b, (((1,), (1,)), ((), ())), preferred_element_type=F32)


def _full(a):
    return pl.BlockSpec(a.shape, lambda *_: (0,) * a.ndim)


def _mod_kernel(c_ref, w_ref, b_ref, o_ref):
    a = c_ref[...]
    a = a * jax.nn.sigmoid(a)
    o_ref[0] = _dot(a.astype(BF16), w_ref[0].astype(BF16)) + b_ref[0]


def _modulation(cond, w_mod, b_mod):
    depth, d, n = w_mod.shape
    tn = n // 4
    rows = cond.shape[0]
    return pl.pallas_call(
        _mod_kernel,
        grid=(depth, n // tn),
        in_specs=[pl.BlockSpec((rows, d), lambda l, j: (0, 0)),
                  pl.BlockSpec((1, d, tn), lambda l, j: (l, 0, j)),
                  pl.BlockSpec((1, 1, tn), lambda l, j: (l, 0, j))],
        out_specs=pl.BlockSpec((1, rows, tn), lambda l, j: (l, 0, j)),
        out_shape=jax.ShapeDtypeStruct((depth, rows, n), F32),
        compiler_params=_params("arbitrary", "arbitrary"),
    )(cond, w_mod, b_mod.reshape(depth, 1, n))


def _rope(t, cos, sin_lo, sin_hi, half):
    w = t.shape[-1]
    return t * cos + pltpu.roll(t, w - half, 1) * sin_lo + pltpu.roll(t, half, 1) * sin_hi


def _inproj_kernel(x_ref, mod_ref, gpre_ref, w_ref, seg_ref,
                   cos_ref, sinl_ref, sinh_ref, cosd_ref, sindl_ref, sindh_ref,
                   aqn_ref, akn_ref, dqn_ref, wuq_ref, dkvn_ref, wukv_ref,
                   qa_ref, ka_ref, va_ref, bx_ref, bg_ref, qc_ref, kc_ref, vc_ref,
                   qd_ref, kd_ref, vd_ref, *, q_lora, mla_scale):
    x = x_ref[0]
    m = mod_ref[0, 0]
    u = (_rms(x, gpre_ref[...]) * (1.0 + m[1:2]) + m[0:1]).astype(BF16)
    proj = _dot(u, w_ref[...])

    seg = seg_ref[...]
    cos = jnp.concatenate([cos_ref[...]] * 2, axis=1)
    sin_lo = jnp.concatenate([sinl_ref[...]] * 2, axis=1)
    sin_hi = jnp.concatenate([sinh_ref[...]] * 2, axis=1)
    cosd = jnp.concatenate([cosd_ref[...]] * 4, axis=1)
    sind_lo = jnp.concatenate([sindl_ref[...]] * 4, axis=1)
    sind_hi = jnp.concatenate([sindh_ref[...]] * 4, axis=1)

    def head_rms(t, gain):
        sq = t * t
        hi = sq.astype(BF16)
        lo = (sq - hi.astype(F32)).astype(BF16)
        ms = (_dot(hi, seg) + _dot(lo, seg)) * (1.0 / HEAD_DIM)
        return t * lax.rsqrt(ms + NORM_EPS) * gain

    def rope_hd(t):
        return _rope(t, cos, sin_lo, sin_hi, HEAD_DIM // 2)

    def rope_r(t):
        return _rope(t, cosd, sind_lo, sind_hi, QK_ROPE // 2)

    sc_hd = HEAD_DIM ** -0.5
    qa_ref[0] = (rope_hd(head_rms(proj[:, 0:256], aqn_ref[...])) * sc_hd).astype(BF16)
    ka_ref[0] = rope_hd(head_rms(proj[:, 256:512], akn_ref[...])).astype(BF16)
    va_ref[0] = proj[:, 512:768].astype(BF16)
    bx_ref[0] = proj[:, 768:1024]
    bg_ref[0] = proj[:, 1024:1280]
    qc_ref[0] = (rope_hd(proj[:, 1280:1536]) * sc_hd).astype(BF16)
    kc_ref[0] = rope_hd(proj[:, 1536:1792]).astype(BF16)
    vc_ref[0] = proj[:, 1792:2048].astype(BF16)
    cq = proj[:, 2048:2304]
    cq = cq * lax.rsqrt(jnp.sum(cq * cq, axis=-1, keepdims=True) * (1.0 / q_lora) + NORM_EPS) * dqn_ref[...]
    qd = _dot(cq.astype(BF16), wuq_ref[...])
    qd_ref[0] = (rope_r(qd) * mla_scale).astype(BF16)
    ckv = _rms(proj[:, 2304:2432], dkvn_ref[...])
    kv = _dot(ckv.astype(BF16), wukv_ref[...])
    kd_ref[0] = (kv[:, 0:512] + rope_r(proj[:, 2432:2944])).astype(BF16)
    vd_ref[0] = kv[:, 512:768].astype(BF16)


def _inproj(xs, modtab, gpre, w_in_p, seg, tabs, aqn, akn, dqn, wuq, dkvn, wukv, *, q_lora, mla_scale):
    bsz, L, d = xs.shape
    nt = L // ROW_TILE
    rowblk = lambda w: pl.BlockSpec((1, ROW_TILE, w), lambda b, t: (b, t, 0))
    tab = lambda a: pl.BlockSpec((ROW_TILE, a.shape[1]), lambda b, t: (t, 0))
    out_w = [(256, BF16)] * 3 + [(256, F32)] * 2 + [(256, BF16)] * 3 + [(512, BF16), (512, BF16), (256, BF16)]
    return pl.pallas_call(
        functools.partial(_inproj_kernel, q_lora=q_lora, mla_scale=mla_scale),
        grid=(bsz, nt),
        in_specs=[rowblk(d),
                  pl.BlockSpec((1, 1, 6, d), lambda b, t: (b, jnp.minimum(t, 1), 0, 0)),
                  _full(gpre), _full(w_in_p), _full(seg)] + [tab(a) for a in tabs]
                 + [_full(a) for a in (aqn, akn, dqn, wuq, dkvn, wukv)],
        out_specs=[rowblk(w) for w, _ in out_w],
        out_shape=[jax.ShapeDtypeStruct((bsz, L, w), dt) for w, dt in out_w],
        compiler_params=_params("arbitrary", "arbitrary"),
    )(xs, modtab, gpre, w_in_p, seg, *tabs, aqn, akn, dqn, wuq, dkvn, wukv)


def _attn_kernel(qa_ref, qb_ref, ka_ref, kb_ref, v_ref, o_ref, *, split_q, ctx_len, k_len):
    lane = lax.broadcasted_iota(jnp.int32, (1, LANES), 1)
    low = lane < HEAD_DIM

    def run(n_keys):
        outs = []
        for h, (q_ref, k_ref) in enumerate(((qa_ref, ka_ref), (qb_ref, kb_ref))):
            q = q_ref[0]
            if split_q:
                keep = low if h == 0 else jnp.logical_not(low)
                q = jnp.where(keep, q, jnp.zeros_like(q))
            s = _dot_t(q, k_ref[0, 0:n_keys, :])
            p = jnp.exp(s - jnp.max(s, axis=-1, keepdims=True))
            den = jnp.sum(p, axis=-1, keepdims=True)
            outs.append(_dot(p.astype(BF16), v_ref[0, 0:n_keys, :]) / den)
        o_ref[0] = jnp.where(low, outs[0], outs[1]).astype(o_ref.dtype)

    t = pl.program_id(2)

    @pl.when(t == 0)
    def _ctx():
        run(ctx_len)

    @pl.when(t > 0)
    def _lat():
        run(k_len)


def _attention(q, k, v, *, split_q, ctx_len):
    bsz, L, _ = q.shape
    tq = ROW_TILE
    assert ctx_len == tq
    if split_q:
        qmap = [lambda b, g, t: (b, t, g)] * 2
        kmap = [lambda b, g, t: (b, 0, g)] * 2
    else:
        qmap = [lambda b, g, t: (b, t, 2 * g), lambda b, g, t: (b, t, 2 * g + 1)]
        kmap = [lambda b, g, t: (b, 0, 2 * g), lambda b, g, t: (b, 0, 2 * g + 1)]
    return pl.pallas_call(
        functools.partial(_attn_kernel, split_q=split_q, ctx_len=ctx_len, k_len=L),
        grid=(bsz, 2, L // tq),
        in_specs=[pl.BlockSpec((1, tq, LANES), qmap[0]), pl.BlockSpec((1, tq, LANES), qmap[1]),
                  pl.BlockSpec((1, L, LANES), kmap[0]), pl.BlockSpec((1, L, LANES), kmap[1]),
                  pl.BlockSpec((1, L, LANES), lambda b, g, t: (b, 0, g))],
        out_specs=pl.BlockSpec((1, tq, LANES), lambda b, g, t: (b, t, g)),
        out_shape=jax.ShapeDtypeStruct((bsz, L, 2 * LANES), BF16),
        compiler_params=_params("arbitrary", "arbitrary", "arbitrary"),
    )(q, q, k, k, v)


def _winattn_kernel(sink_ref, q_ref, k_ref, v_ref, o_ref, *, ctx_len, seq):
    g = pl.program_id(1)
    t = pl.program_id(2)
    lane = lax.broadcasted_iota(jnp.int32, (1, LANES), 1)
    low = lane < HEAD_DIM
    ctx_blocks = ctx_len // Q_BLOCK
    win = 3 * Q_BLOCK

    def finish(parts, sink):
        m = sink
        for s, _ in parts:
            m = jnp.maximum(m, jnp.max(s, axis=-1, keepdims=True))
        den = jnp.exp(sink - m)
        acc = None
        for s, vv in parts:
            p = jnp.exp(s - m)
            den = den + jnp.sum(p, axis=-1, keepdims=True)
            o = _dot(p.astype(BF16), vv)
            acc = o if acc is None else acc + o
        return acc / den

    def heads(fn):
        outs = []
        for h in range(2):
            keep = low if h == 0 else jnp.logical_not(low)
            q = q_ref[0]
            q = jnp.where(keep, q, jnp.zeros_like(q))
            outs.append(fn(q, sink_ref[2 * g + h]))
        o_ref[0] = jnp.where(low, outs[0], outs[1]).astype(o_ref.dtype)

    @pl.when(t < ctx_blocks)
    def _ctx():
        def one(q, sink):
            return finish([(_dot_t(q, k_ref[0, 0:ctx_len, :]), v_ref[0, 0:ctx_len, :])], sink)
        heads(one)

    @pl.when(t >= ctx_blocks)
    def _lat():
        n = t - ctx_blocks
        start = jnp.clip((n - 1) * Q_BLOCK, 0, seq - win)
        off = pl.multiple_of(ctx_len + start, Q_BLOCK)
        qpos = n * Q_BLOCK + lax.broadcasted_iota(jnp.int32, (Q_BLOCK, win), 0)
        kpos = start + lax.broadcasted_iota(jnp.int32, (Q_BLOCK, win), 1)
        allowed = jnp.abs(kpos - qpos) <= WINDOW

        def one(q, sink):
            s_ctx = _dot_t(q, k_ref[0, 0:ctx_len, :])
            s_win = jnp.where(allowed, _dot_t(q, k_ref[0, pl.ds(off, win), :]), NEG_INF)
            return finish([(s_ctx, v_ref[0, 0:ctx_len, :]), (s_win, v_ref[0, pl.ds(off, win), :])], sink)
        heads(one)


def _window_attention(q, k, v, sink, *, ctx_len):
    bsz, L, _ = q.shape
    return pl.pallas_call(
        functools.partial(_winattn_kernel, ctx_len=ctx_len, seq=L - ctx_len),
        grid=(bsz, 2, L // Q_BLOCK),
        in_specs=[pl.BlockSpec(memory_space=pltpu.SMEM),
                  pl.BlockSpec((1, Q_BLOCK, LANES), lambda b, g, t: (b, t, g)),
                  pl.BlockSpec((1, L, LANES), lambda b, g, t: (b, 0, g)),
                  pl.BlockSpec((1, L, LANES), lambda b, g, t: (b, 0, g))],
        out_specs=pl.BlockSpec((1, Q_BLOCK, LANES), lambda b, g, t: (b, t, g)),
        out_shape=jax.ShapeDtypeStruct((bsz, L, 2 * LANES), BF16),
        compiler_params=_params("arbitrary", "arbitrary", "arbitrary"),
    )(sink, q, k, v)


def _lru_kernel(xf_ref, xfp_ref, xfn_ref, xb_ref, xbp_ref, xbn_ref,
                cw_ref, cb_ref, wa_ref, ba_ref, wi_ref, bi_ref, lam_ref,
                hf_ref, hb_ref,
                af_s, bf_s, ab_s, bb_s, of_s, ob_s, sf_s, sb_s, *, n_chunks):
    i = pl.program_id(0)
    bsz, tc, w = xf_ref.shape
    rows = bsz * tc
    ti = lax.broadcasted_iota(jnp.int32, (bsz, tc, w), 1)

    def coeffs(x_ref, prev_ref, next_ref, chunk, d, a_s, b_s):
        has_prev = (chunk >= 2).astype(F32)
        has_next = jnp.logical_and(chunk >= 1, chunk <= n_chunks - 2).astype(F32)
        x = x_ref[...]
        p1 = prev_ref[:, 7:8, :] * has_prev
        n0 = next_ref[:, 0:1, :] * has_next
        n1 = next_ref[:, 1:2, :] * has_next
        x2 = x.reshape(rows, w)
        xm1 = jnp.where(ti == 0, p1, pltpu.roll(x2, 1, 0).reshape(bsz, tc, w))
        xp1 = jnp.where(ti == tc - 1, n0, pltpu.roll(x2, rows - 1, 0).reshape(bsz, tc, w))
        xp2 = jnp.where(ti == tc - 1, n1,
                        jnp.where(ti == tc - 2, n0, pltpu.roll(x2, rows - 2, 0).reshape(bsz, tc, w)))
        cw = cw_ref[...]
        xc = (cb_ref[...] + xm1 * cw[0:1] + x * cw[1:2] + xp1 * cw[2:3] + xp2 * cw[3:4]).reshape(rows, w)
        xcb = xc.astype(BF16)
        r = jax.nn.sigmoid(_dot(xcb, wa_ref[d]) + ba_ref[d])
        ig = jax.nn.sigmoid(_dot(xcb, wi_ref[d]) + bi_ref[d])
        log_a = (-LRU_C) * r * jax.nn.softplus(-lam_ref[d])
        a = jnp.exp(log_a)
        b = jnp.sqrt(1.0 - a * a) * (ig * xc)
        for j in range(w // LANES):
            a_s[j] = a[:, j * LANES:(j + 1) * LANES]
            b_s[j] = b[:, j * LANES:(j + 1) * LANES]

    chunk_b = jnp.where(i == 0, 0, n_chunks - i)
    coeffs(xf_ref, xfp_ref, xfn_ref, i, 0, af_s, bf_s)
    coeffs(xb_ref, xbp_ref, xbn_ref, chunk_b, 1, ab_s, bb_s)

    @pl.when(i == 0)
    def _init():
        sf_s[...] = jnp.zeros_like(sf_s)
        sb_s[...] = jnp.zeros_like(sb_s)

    nl = w // LANES

    def step(t, carry):
        fwd = pl.ds(t, bsz, stride=tc)
        bwd = pl.ds(tc - 1 - t, bsz, stride=tc)
        out = []
        for j in range(nl):
            hf = af_s[j, fwd, :] * carry[j] + bf_s[j, fwd, :]
            hb = ab_s[j, bwd, :] * carry[nl + j] + bb_s[j, bwd, :]
            of_s[j, fwd, :] = hf
            ob_s[j, bwd, :] = hb
            out.append((hf, hb))
        return tuple(o[0] for o in out) + tuple(o[1] for o in out)

    init = tuple(sf_s[j] for j in range(nl)) + tuple(sb_s[j] for j in range(nl))
    fin = lax.fori_loop(0, tc, step, init, unroll=8)
    for j in range(nl):
        sf_s[j] = fin[j]
        sb_s[j] = fin[nl + j]
    hf_ref[...] = jnp.concatenate([of_s[j] for j in range(nl)], axis=1).reshape(bsz, tc, w)
    hb_ref[...] = jnp.concatenate([ob_s[j] for j in range(nl)], axis=1).reshape(bsz, tc, w)


def _lru(bx, conv_w, conv_b, wa, ba, wi, bi, lam, *, ctx_len):
    bsz, L, w = bx.shape
    tc = ctx_len
    nc = L // tc
    hb_blocks = tc // 8
    last8 = L // 8 - 1

    def fchunk(i):
        return i

    def bchunk(i):
        return jnp.where(i == 0, 0, nc - i)

    def cur(cf):
        return pl.BlockSpec((bsz, tc, w), lambda i: (0, cf(i), 0))

    def prev(cf):
        return pl.BlockSpec((bsz, 8, w), lambda i: (0, jnp.maximum(cf(i) * hb_blocks - 1, 0), 0))

    def nxt(cf):
        return pl.BlockSpec((bsz, 8, w), lambda i: (0, jnp.minimum((cf(i) + 1) * hb_blocks, last8), 0))

    small = [conv_w, conv_b, wa, ba, wi, bi, lam]
    nl = w // LANES
    scr = [pltpu.VMEM((nl, bsz * tc, LANES), F32)] * 6 + [pltpu.VMEM((nl, bsz, LANES), F32)] * 2
    return pl.pallas_call(
        functools.partial(_lru_kernel, n_chunks=nc),
        grid=(nc,),
        in_specs=[cur(fchunk), prev(fchunk), nxt(fchunk), cur(bchunk), prev(bchunk), nxt(bchunk)]
                 + [_full(a) for a in small],
        out_specs=[cur(fchunk), cur(bchunk)],
        out_shape=[jax.ShapeDtypeStruct((bsz, L, w), F32)] * 2,
        scratch_shapes=scr,
        compiler_params=_params("arbitrary"),
    )(bx, bx, bx, bx, bx, bx, *small)


def _route(logits_t, bias, n_experts):
    per = n_experts // N_EXPERT_GROUPS
    tm = logits_t.shape[-1]
    scores = jax.nn.sigmoid(logits_t).reshape(N_EXPERT_GROUPS, per, tm)
    sel = scores + bias.reshape(N_EXPERT_GROUPS, per, 1)
    shape = sel.shape
    gi = lax.broadcasted_iota(jnp.int32, shape, 0)
    mi = lax.broadcasted_iota(jnp.int32, shape, 1)
    ei = gi * per + mi
    m1 = jnp.max(sel, axis=1, keepdims=True)
    first = jnp.min(jnp.where(sel == m1, mi, per), axis=1, keepdims=True)
    m2 = jnp.max(jnp.where(mi == first, NEG_INF, sel), axis=1, keepdims=True)
    gscore = m1 + m2
    gidx = lax.broadcasted_iota(jnp.int32, gscore.shape, 0)
    gmask = jnp.zeros(gscore.shape, F32)
    for _ in range(TOPK_GROUPS):
        m = jnp.max(gscore, axis=0, keepdims=True)
        pick = jnp.min(jnp.where(gscore == m, gidx, N_EXPERT_GROUPS), axis=0, keepdims=True)
        hit = gidx == pick
        gmask = jnp.where(hit, 1.0, gmask)
        gscore = jnp.where(hit, NEG_INF, gscore)
    cand = jnp.where(gmask > 0.0, sel, NEG_INF)
    chosen = jnp.zeros(shape, F32)
    for _ in range(TOP_K):
        m = jnp.max(jnp.max(cand, axis=1, keepdims=True), axis=0, keepdims=True)
        pick = jnp.where(cand == m, ei, n_experts)
        pick = jnp.min(jnp.min(pick, axis=1, keepdims=True), axis=0, keepdims=True)
        hit = ei == pick
        chosen = jnp.where(hit, 1.0, chosen)
        cand = jnp.where(hit, NEG_INF, cand)
    wsel = jnp.where(chosen > 0.0, scores, 0.0)
    den = jnp.sum(jnp.sum(wsel, axis=1, keepdims=True), axis=0, keepdims=True)
    return (wsel / den * ROUTED_SCALE).reshape(n_experts, tm)


def _outproj_kernel(ya_ref, hf_ref, hb_ref, bg_ref, yc_ref, yd_ref, x_ref, mod_ref,
                    wout_ref, gpost_ref, gffn_ref, rw_ref, rb_ref,
                    xo_ref, v_ref, gate_ref, *, n_experts):
    gw = ya_ref.shape[-1]
    m = mod_ref[0, 0]
    yb = ((hf_ref[0] + hb_ref[0]) * jax.nn.gelu(bg_ref[0])).astype(BF16)
    y = (_dot(ya_ref[0], wout_ref[0:gw, :]) + _dot(yb, wout_ref[gw:2 * gw, :])
         + _dot(yc_ref[0], wout_ref[2 * gw:3 * gw, :]) + _dot(yd_ref[0], wout_ref[3 * gw:4 * gw, :]))
    x1 = x_ref[0] + m[2:3] * _rms(y, gpost_ref[...])
    xo_ref[0] = x1
    v = (_rms(x1, gffn_ref[...]) * (1.0 + m[4:5]) + m[3:4]).astype(BF16)
    v_ref[0] = v
    logits_t = _dot_t(rw_ref[...], v)
    gates_t = _route(logits_t, rb_ref[...], n_experts)
    pad = jnp.zeros((LANES - n_experts, gates_t.shape[1]), F32)
    gate_ref[0] = jnp.concatenate([gates_t, pad], axis=0).T


def _outproj(ya, hf, hb, bg, yc, yd, xs, modtab, w_out, gpost, gffn, rw_t, rbias):
    bsz, L, d = xs.shape
    n_experts = rw_t.shape[0]
    gw = ya.shape[-1]
    blk = lambda w: pl.BlockSpec((1, ROW_TILE, w), lambda b, t: (b, t, 0))
    return pl.pallas_call(
        functools.partial(_outproj_kernel, n_experts=n_experts),
        grid=(bsz, L // ROW_TILE),
        in_specs=[blk(gw)] * 6 + [blk(d),
                  pl.BlockSpec((1, 1, 6, d), lambda b, t: (b, jnp.minimum(t, 1), 0, 0)),
                  _full(w_out), _full(gpost), _full(gffn), _full(rw_t), _full(rbias)],
        out_specs=[blk(d), blk(d), blk(LANES)],
        out_shape=[jax.ShapeDtypeStruct((bsz, L, d), F32), jax.ShapeDtypeStruct((bsz, L, d), BF16),
                   jax.ShapeDtypeStruct((bsz, L, LANES), F32)],
        compiler_params=_params("arbitrary", "arbitrary"),
    )(ya, hf, hb, bg, yc, yd, xs, modtab, w_out, gpost, gffn, rw_t, rbias)


def _swiglu(v, wg, wu, wd):
    hg = _dot(v, wg.astype(BF16))
    hu = _dot(v, wu.astype(BF16))
    return hg * jax.nn.sigmoid(hg) * hu, wd.astype(BF16)


def _moe_kernel(v_ref, gate_ref, x_ref, mod_ref, wg_ref, wu_ref, wd_ref,
                sg_ref, su_ref, sd_ref, gpost_ref, o_ref, acc_s, *, ctx_len, tiles_per_batch):
    j = pl.program_id(0)
    e = pl.program_id(1)
    v = v_ref[0]

    @pl.when(e == 0)
    def _shared():
        h, wd = _swiglu(v, sg_ref[...], su_ref[...], sd_ref[...])
        acc_s[...] = _dot(h.astype(BF16), wd)

    h, wd = _swiglu(v, wg_ref[0], wu_ref[0], wd_ref[0])
    gates = gate_ref[0]
    lane = lax.broadcasted_iota(jnp.int32, gates.shape, 1)
    gcol = jnp.sum(jnp.where(lane == e, gates, 0.0), axis=1, keepdims=True)
    acc_s[...] += _dot(h.astype(BF16), wd) * gcol

    @pl.when(e == pl.num_programs(1) - 1)
    def _finish():
        mods = mod_ref[0]
        row = (j % tiles_per_batch) * v.shape[0] + lax.broadcasted_iota(jnp.int32, (v.shape[0], 1), 0)
        g2 = jnp.where(row < ctx_len, mods[0, 5:6], mods[1, 5:6])
        o_ref[0] = x_ref[0] + g2 * _rms(acc_s[...], gpost_ref[...])


def _moe(v, gates, xs, modtab, wg, wu, wd, sg, su, sd, gpost, *, ctx_len):
    bsz, L, d = xs.shape
    n_experts, _, de = wg.shape
    tpb = L // MOE_ROWS
    rows = lambda w: pl.BlockSpec((1, MOE_ROWS, w), lambda j, e: (j // tpb, j % tpb, 0))
    return pl.pallas_call(
        functools.partial(_moe_kernel, ctx_len=ctx_len, tiles_per_batch=tpb),
        grid=(bsz * tpb, n_experts),
        in_specs=[rows(d), rows(LANES), rows(d),
                  pl.BlockSpec((1, 2, 6, d), lambda j, e: (j // tpb, 0, 0, 0)),
                  pl.BlockSpec((1, d, de), lambda j, e: (e, 0, 0)),
                  pl.BlockSpec((1, d, de), lambda j, e: (e, 0, 0)),
                  pl.BlockSpec((1, de, d), lambda j, e: (e, 0, 0)),
                  _full(sg), _full(su), _full(sd), _full(gpost)],
        out_specs=rows(d),
        out_shape=jax.ShapeDtypeStruct((bsz, L, d), F32),
        scratch_shapes=[pltpu.VMEM((MOE_ROWS, d), F32)],
        compiler_params=_params("arbitrary", "arbitrary"),
    )(v, gates, xs, modtab, wg, wu, wd, sg, su, sd, gpost)


def _tables(seq, ctx_len):
    def build(rot_dim, lead, slot):
        n = rot_dim // 4
        pos = jnp.arange(seq)
        row = (pos // GRID_W).astype(F32)
        col = (pos % GRID_W).astype(F32)
        inv = ROPE_THETA ** (-jnp.arange(n, dtype=F32) / n)
        ang = jnp.concatenate([row[:, None] * inv, col[:, None] * inv], axis=-1)
        cos, sin = jnp.cos(ang), jnp.sin(ang)
        zero = jnp.zeros_like(sin)
        tail = slot - lead - rot_dim
        one_l, zero_l = jnp.ones((seq, lead), F32), jnp.zeros((seq, lead), F32)
        one_t, zero_t = jnp.ones((seq, tail), F32), jnp.zeros((seq, tail), F32)
        c = jnp.concatenate([one_l, cos, cos, one_t], axis=1)
        s_lo = jnp.concatenate([zero_l, -sin, zero, zero_t], axis=1)
        s_hi = jnp.concatenate([zero_l, zero, sin, zero_t], axis=1)
        ctx_c = jnp.ones((ctx_len, slot), F32)
        ctx_s = jnp.zeros((ctx_len, slot), F32)
        return [jnp.concatenate([ctx_c, c], axis=0), jnp.concatenate([ctx_s, s_lo], axis=0),
                jnp.concatenate([ctx_s, s_hi], axis=0)]

    hd = [jnp.concatenate([t, t], axis=1) for t in build(HEAD_DIM, 0, HEAD_DIM)]
    mla = build(QK_ROPE, QK_NOPE, LANES)
    return hd + mla


def _dup_heads(w, n_heads):
    d = w.shape[0]
    wh = w.reshape(d, n_heads, 1, HEAD_DIM)
    return jnp.broadcast_to(wh, (d, n_heads, 2, HEAD_DIM)).reshape(d, n_heads * 2 * HEAD_DIM)


def kernel(x, c, ctx, c_ctx, w_mod, b_mod, g_pre_mix, g_post_mix, g_pre_ffn, g_post_ffn, w_in, a_q_norm, a_k_norm, conv_w, conv_b, lru_wa, lru_ba, lru_wi, lru_bi, lru_lambda, c_sink, d_q_norm, d_w_uq, d_kv_norm, d_w_ukv, w_out, router_w, router_bias, w_gate, w_up, w_down, sh_gate, sh_up, sh_down):
    bsz, seq, d = x.shape
    ctx_len = ctx.shape[1]
    depth = w_mod.shape[0]
    gw = d // 4
    lru_w = conv_w.shape[-1]
    q_lora = d_q_norm.shape[-1]
    kv_lora = d_kv_norm.shape[-1]
    n_experts = router_w.shape[-1]
    a_heads = gw // HEAD_DIM
    a_kv = (w_in.shape[-1] - (2 * gw + 2 * lru_w + q_lora + kv_lora + QK_ROPE)) // (4 * HEAD_DIM)
    mla_scale = (QK_NOPE + QK_ROPE) ** -0.5
    v_dim = gw // D_HEADS
    assert ctx_len == ROW_TILE and a_kv == 2 and a_heads == 4 and n_experts <= LANES

    cond = jnp.zeros((16, d), F32).at[:bsz].set(c).at[bsz].set(c_ctx)
    mod = _modulation(cond, w_mod, b_mod).reshape(depth, 16, 6, d)
    tabs = _tables(seq, ctx_len)
    seg = jnp.kron(jnp.eye(a_heads, dtype=F32), jnp.ones((HEAD_DIM, HEAD_DIM), F32)).astype(BF16)

    xs = jnp.concatenate([ctx, x], axis=1)
    for l in range(depth):
        modtab = jnp.stack([jnp.broadcast_to(mod[l, bsz], (bsz, 6, d)), mod[l, :bsz]], axis=1)

        offs = [0]
        for wdt in (gw, a_kv * HEAD_DIM, a_kv * HEAD_DIM, lru_w, lru_w, gw, a_kv * HEAD_DIM, a_kv * HEAD_DIM,
                    q_lora, kv_lora, QK_ROPE):
            offs.append(offs[-1] + wdt)
        col = lambda i: w_in[l][:, offs[i]:offs[i + 1]]
        zeros = lambda n: jnp.zeros((d, n), F32)
        kr_slot = jnp.concatenate([zeros(QK_NOPE), col(10), zeros(LANES - QK_NOPE - QK_ROPE)], axis=1)
        w_in_p = jnp.concatenate(
            [col(0), _dup_heads(col(1), a_kv), _dup_heads(col(2), a_kv), col(3), col(4),
             col(5), _dup_heads(col(6), a_kv), _dup_heads(col(7), a_kv),
             col(8), zeros(2 * LANES - q_lora), col(9)] + [kr_slot] * D_HEADS, axis=1).astype(BF16)

        qk = QK_NOPE + QK_ROPE
        wuq = d_w_uq[l].reshape(q_lora, D_HEADS, qk)
        wuq = jnp.pad(wuq, ((0, 2 * LANES - q_lora), (0, 0), (0, LANES - qk))).reshape(2 * LANES, D_HEADS * LANES)
        wukv = d_w_ukv[l].reshape(kv_lora, D_HEADS, QK_NOPE + v_dim)
        wk = jnp.pad(wukv[:, :, :QK_NOPE], ((0, 0), (0, 0), (0, LANES - QK_NOPE))).reshape(kv_lora, D_HEADS * LANES)
        wv = wukv[:, :, QK_NOPE:].reshape(kv_lora, D_HEADS * v_dim)
        wukv_p = jnp.concatenate([wk, wv], axis=1).astype(BF16)
        dqn = jnp.pad(d_q_norm[l], (0, 2 * LANES - q_lora)).reshape(1, 2 * LANES)

        qa, ka, va, bx, bg, qc, kc, vc, qd, kd, vd = _inproj(
            xs, modtab, g_pre_mix[l].reshape(1, d), w_in_p, seg, tabs,
            jnp.tile(a_q_norm[l], a_heads).reshape(1, gw), jnp.tile(a_k_norm[l], 2 * a_kv).reshape(1, gw),
            dqn, wuq.astype(BF16), d_kv_norm[l].reshape(1, kv_lora), wukv_p,
            q_lora=q_lora, mla_scale=mla_scale)

        ya = _attention(qa, ka, va, split_q=True, ctx_len=ctx_len)
        yd = _attention(qd, kd, vd, split_q=False, ctx_len=ctx_len)
        yc = _window_attention(qc, kc, vc, c_sink[l], ctx_len=ctx_len)

        blocks = lru_wa.shape[2]
        bdiag = lambda wts: jnp.stack([jax.scipy.linalg.block_diag(*[wts[dd, h] for h in range(blocks)])
                                       for dd in range(2)]).astype(BF16)
        hf, hb = _lru(bx, conv_w[l], conv_b[l].reshape(1, lru_w), bdiag(lru_wa[l]), lru_ba[l].reshape(2, 1, lru_w),
                      bdiag(lru_wi[l]), lru_bi[l].reshape(2, 1, lru_w), lru_lambda[l].reshape(2, 1, lru_w),
                      ctx_len=ctx_len)

        rw_t = router_w[l].T.astype(BF16)
        xs_mid, v, gates = _outproj(ya, hf, hb, bg, yc, yd, xs, modtab, w_out[l].astype(BF16),
                                    g_post_mix[l].reshape(1, d), g_pre_ffn[l].reshape(1, d),
                                    rw_t, router_bias[l].reshape(n_experts, 1))
        xs = _moe(v, gates, xs_mid, modtab, w_gate[l], w_up[l], w_down[l],
                  sh_gate[l], sh_up[l], sh_down[l], g_post_ffn[l].reshape(1, d), ctx_len=ctx_len)
    return xs[:, ctx_len:, :]
```

```python
import functools

import jax
import jax.numpy as jnp
from jax import lax
from jax.experimental import pallas as pl
from jax.experimental.pallas import tpu as pltpu

F32 = jnp.float32
BF16 = jnp.bfloat16

GRID_W = 64
HEAD_DIM = 64
ROPE_THETA = 10000.0
NORM_EPS = 1e-6
WINDOW = 128
Q_BLOCK = 128
D_HEADS = 4
QK_NOPE = 64
QK_ROPE = 32
LRU_C = 8.0
CONV_W = 4
TOP_K = 8
N_EXPERT_GROUPS = 8
TOPK_GROUPS = 4
ROUTED_SCALE = 2.5

LANES = 128
VMEM_LIMIT = 56 * 1024 * 1024
ROW_TILE = 256

NEG_INF = float("-inf")


def _params(*sem):
    return pltpu.CompilerParams(dimension_semantics=sem, vmem_limit_bytes=VMEM_LIMIT)


def _rms(x, gain):
    return x * lax.rsqrt(jnp.mean(x * x, axis=-1, keepdims=True) + NORM_EPS) * gain


def _dot(a, b):
    return jnp.dot(a, b, preferred_element_type=F32)


def _dot_t(a, b):
    return lax.dot_general(a, b, (((1,), (1,)), ((), ())), preferred_element_type=F32)


def _full(a):
    return pl.BlockSpec(a.shape, lambda *_: (0,) * a.ndim)


def _mod_kernel(c_ref, w_ref, b_ref, o_ref):
    a = c_ref[...]
    a = a * jax.nn.sigmoid(a)
    o_ref[0] = _dot(a.astype(BF16), w_ref[0].astype(BF16)) + b_ref[0]


def _modulation(cond, w_mod, b_mod):
    depth, d, n = w_mod.shape
    tn = n // 4
    rows = cond.shape[0]
    return pl.pallas_call(
        _mod_kernel,
        grid=(depth, n // tn),
        in_specs=[pl.BlockSpec((rows, d), lambda l, j: (0, 0)),
                  pl.BlockSpec((1, d, tn), lambda l, j: (l, 0, j)),
                  pl.BlockSpec((1, 1, tn), lambda l, j: (l, 0, j))],
        out_specs=pl.BlockSpec((1, rows, tn), lambda l, j: (l, 0, j)),
        out_shape=jax.ShapeDtypeStruct((depth, rows, n), F32),
        compiler_params=_params("arbitrary", "arbitrary"),
    )(cond, w_mod, b_mod.reshape(depth, 1, n))


def _rope(t, cos, sin_lo, sin_hi, half):
    w = t.shape[-1]
    return t * cos + pltpu.roll(t, w - half, 1) * sin_lo + pltpu.roll(t, half, 1) * sin_hi


def _inproj_kernel(x_ref, mod_ref, gpre_ref, w_ref, seg_ref,
                   cos_ref, sinl_ref, sinh_ref, cosd_ref, sindl_ref, sindh_ref,
                   aqn_ref, akn_ref, dqn_ref, wuq_ref, dkvn_ref, wukv_ref,
                   qa_ref, ka_ref, va_ref, bx_ref, bg_ref, qc_ref, kc_ref, vc_ref,
                   qd_ref, kd_ref, vd_ref, *, q_lora, mla_scale):
    x = x_ref[0]
    m = mod_ref[0, 0]
    u = (_rms(x, gpre_ref[...]) * (1.0 + m[1:2]) + m[0:1]).astype(BF16)
    proj = _dot(u, w_ref[...])

    seg = seg_ref[...]
    cos = jnp.concatenate([cos_ref[...]] * 2, axis=1)
    sin_lo = jnp.concatenate([sinl_ref[...]] * 2, axis=1)
    sin_hi = jnp.concatenate([sinh_ref[...]] * 2, axis=1)
    cosd = jnp.concatenate([cosd_ref[...]] * 4, axis=1)
    sind_lo = jnp.concatenate([sindl_ref[...]] * 4, axis=1)
    sind_hi = jnp.concatenate([sindh_ref[...]] * 4, axis=1)

    def head_rms(t, gain):
        sq = t * t
        hi = sq.astype(BF16)
        lo = (sq - hi.astype(F32)).astype(BF16)
        ms = (_dot(hi, seg) + _dot(lo, seg)) * (1.0 / HEAD_DIM)
        return t * lax.rsqrt(ms + NORM_EPS) * gain

    def rope_hd(t):
        return _rope(t, cos, sin_lo, sin_hi, HEAD_DIM // 2)

    def rope_r(t):
        return _rope(t, cosd, sind_lo, sind_hi, QK_ROPE // 2)

    sc_hd = HEAD_DIM ** -0.5
    qa_ref[0] = (rope_hd(head_rms(proj[:, 0:256], aqn_ref[...])) * sc_hd).astype(BF16)
    ka_ref[0] = rope_hd(head_rms(proj[:, 256:512], akn_ref[...])).astype(BF16)
    va_ref[0] = proj[:, 512:768].astype(BF16)
    bx_ref[0] = proj[:, 768:1024]
    bg_ref[0] = proj[:, 1024:1280]
    qc_ref[0] = (rope_hd(proj[:, 1280:1536]) * sc_hd).astype(BF16)
    kc_ref[0] = rope_hd(proj[:, 1536:1792]).astype(BF16)
    vc_ref[0] = proj[:, 1792:2048].astype(BF16)
    cq = proj[:, 2048:2304]
    cq = cq * lax.rsqrt(jnp.sum(cq * cq, axis=-1, keepdims=True) * (1.0 / q_lora) + NORM_EPS) * dqn_ref[...]
    qd = _dot(cq.astype(BF16), wuq_ref[...])
    qd_ref[0] = (rope_r(qd) * mla_scale).astype(BF16)
    ckv = _rms(proj[:, 2304:2432], dkvn_ref[...])
    kv = _dot(ckv.astype(BF16), wukv_ref[...])
    kd_ref[0] = (kv[:, 0:512] + rope_r(proj[:, 2432:2944])).astype(BF16)
    vd_ref[0] = kv[:, 512:768].astype(BF16)


def _inproj(xs, modtab, gpre, w_in_p, seg, tabs, aqn, akn, dqn, wuq, dkvn, wukv, *, q_lora, mla_scale):
    bsz, L, d = xs.shape
    nt = L // ROW_TILE
    rowblk = lambda w: pl.BlockSpec((1, ROW_TILE, w), lambda b, t: (b, t, 0))
    tab = lambda a: pl.BlockSpec((ROW_TILE, a.shape[1]), lambda b, t: (t, 0))
    out_w = [(256, BF16)] * 3 + [(256, F32)] * 2 + [(256, BF16)] * 3 + [(512, BF16), (512, BF16), (256, BF16)]
    return pl.pallas_call(
        functools.partial(_inproj_kernel, q_lora=q_lora, mla_scale=mla_scale),
        grid=(bsz, nt),
        in_specs=[rowblk(d),
                  pl.BlockSpec((1, 1, 6, d), lambda b, t: (b, jnp.minimum(t, 1), 0, 0)),
                  _full(gpre), _full(w_in_p), _full(seg)] + [tab(a) for a in tabs]
                 + [_full(a) for a in (aqn, akn, dqn, wuq, dkvn, wukv)],
        out_specs=[rowblk(w) for w, _ in out_w],
        out_shape=[jax.ShapeDtypeStruct((bsz, L, w), dt) for w, dt in out_w],
        compiler_params=_params("arbitrary", "arbitrary"),
    )(xs, modtab, gpre, w_in_p, seg, *tabs, aqn, akn, dqn, wuq, dkvn, wukv)


def _attn_kernel(qa_ref, qb_ref, ka_ref, kb_ref, v_ref, o_ref, *, split_q, ctx_len, k_len):
    lane = lax.broadcasted_iota(jnp.int32, (1, LANES), 1)
    low = lane < HEAD_DIM

    def run(n_keys):
        outs = []
        for h, (q_ref, k_ref) in enumerate(((qa_ref, ka_ref), (qb_ref, kb_ref))):
            q = q_ref[0]
            if split_q:
                keep = low if h == 0 else jnp.logical_not(low)
                q = jnp.where(keep, q, jnp.zeros_like(q))
            s = _dot_t(q, k_ref[0, 0:n_keys, :])
            p = jnp.exp(s - jnp.max(s, axis=-1, keepdims=True))
            den = jnp.sum(p, axis=-1, keepdims=True)
            outs.append(_dot(p.astype(BF16), v_ref[0, 0:n_keys, :]) / den)
        o_ref[0] = jnp.where(low, outs[0], outs[1]).astype(o_ref.dtype)

    t = pl.program_id(2)

    @pl.when(t == 0)
    def _ctx():
        run(ctx_len)

    @pl.when(t > 0)
    def _lat():
        run(k_len)


def _attention(q, k, v, *, split_q, ctx_len):
    bsz, L, _ = q.shape
    tq = ROW_TILE
    assert ctx_len == tq
    if split_q:
        qmap = [lambda b, g, t: (b, t, g)] * 2
        kmap = [lambda b, g, t: (b, 0, g)] * 2
    else:
        qmap = [lambda b, g, t: (b, t, 2 * g), lambda b, g, t: (b, t, 2 * g + 1)]
        kmap = [lambda b, g, t: (b, 0, 2 * g), lambda b, g, t: (b, 0, 2 * g + 1)]
    return pl.pallas_call(
        functools.partial(_attn_kernel, split_q=split_q, ctx_len=ctx_len, k_len=L),
        grid=(bsz, 2, L // tq),
        in_specs=[pl.BlockSpec((1, tq, LANES), qmap[0]), pl.BlockSpec((1, tq, LANES), qmap[1]),
                  pl.BlockSpec((1, L, LANES), kmap[0]), pl.BlockSpec((1, L, LANES), kmap[1]),
                  pl.BlockSpec((1, L, LANES), lambda b, g, t: (b, 0, g))],
        out_specs=pl.BlockSpec((1, tq, LANES), lambda b, g, t: (b, t, g)),
        out_shape=jax.ShapeDtypeStruct((bsz, L, 2 * LANES), BF16),
        compiler_params=_params("arbitrary", "arbitrary", "arbitrary"),
    )(q, q, k, k, v)


def _winattn_kernel(sink_ref, q_ref, k_ref, v_ref, o_ref, *, ctx_len, seq):
    g = pl.program_id(1)
    t = pl.program_id(2)
    lane = lax.broadcasted_iota(jnp.int32, (1, LANES), 1)
    low = lane < HEAD_DIM
    ctx_blocks = ctx_len // Q_BLOCK
    win = 3 * Q_BLOCK

    def finish(parts, sink):
        m = sink
        for s, _ in parts:
            m = jnp.maximum(m, jnp.max(s, axis=-1, keepdims=True))
        den = jnp.exp(sink - m)
        acc = None
        for s, vv in parts:
            p = jnp.exp(s - m)
            den = den + jnp.sum(p, axis=-1, keepdims=True)
            o = _dot(p.astype(BF16), vv)
            acc = o if acc is None else acc + o
        return acc / den

    def heads(fn):
        outs = []
        for h in range(2):
            keep = low if h == 0 else jnp.logical_not(low)
            q = q_ref[0]
            q = jnp.where(keep, q, jnp.zeros_like(q))
            outs.append(fn(q, sink_ref[2 * g + h]))
        o_ref[0] = jnp.where(low, outs[0], outs[1]).astype(o_ref.dtype)

    @pl.when(t < ctx_blocks)
    def _ctx():
        def one(q, sink):
            return finish([(_dot_t(q, k_ref[0, 0:ctx_len, :]), v_ref[0, 0:ctx_len, :])], sink)
        heads(one)

    @pl.when(t >= ctx_blocks)
    def _lat():
        n = t - ctx_blocks
        start = jnp.clip((n - 1) * Q_BLOCK, 0, seq - win)
        off = pl.multiple_of(ctx_len + start, Q_BLOCK)
        qpos = n * Q_BLOCK + lax.broadcasted_iota(jnp.int32, (Q_BLOCK, win), 0)
        kpos = start + lax.broadcasted_iota(jnp.int32, (Q_BLOCK, win), 1)
        allowed = jnp.abs(kpos - qpos) <= WINDOW

        def one(q, sink):
            s_ctx = _dot_t(q, k_ref[0, 0:ctx_len, :])
            s_win = jnp.where(allowed, _dot_t(q, k_ref[0, pl.ds(off, win), :]), NEG_INF)
            return finish([(s_ctx, v_ref[0, 0:ctx_len, :]), (s_win, v_ref[0, pl.ds(off, win), :])], sink)
        heads(one)


def _window_attention(q, k, v, sink, *, ctx_len):
    bsz, L, _ = q.shape
    return pl.pallas_call(
        functools.partial(_winattn_kernel, ctx_len=ctx_len, seq=L - ctx_len),
        grid=(bsz, 2, L // Q_BLOCK),
        in_specs=[pl.BlockSpec(memory_space=pltpu.SMEM),
                  pl.BlockSpec((1, Q_BLOCK, LANES), lambda b, g, t: (b, t, g)),
                  pl.BlockSpec((1, L, LANES), lambda b, g, t: (b, 0, g)),
                  pl.BlockSpec((1, L, LANES), lambda b, g, t: (b, 0, g))],
        out_specs=pl.BlockSpec((1, Q_BLOCK, LANES), lambda b, g, t: (b, t, g)),
        out_shape=jax.ShapeDtypeStruct((bsz, L, 2 * LANES), BF16),
        compiler_params=_params("arbitrary", "arbitrary", "arbitrary"),
    )(sink, q, k, v)


def _lru_kernel(xf_ref, xfp_ref, xfn_ref, xb_ref, xbp_ref, xbn_ref,
                cw_ref, cb_ref, wa_ref, ba_ref, wi_ref, bi_ref, lam_ref,
                hf_ref, hb_ref,
                af_s, bf_s, ab_s, bb_s, of_s, ob_s, sf_s, sb_s, *, n_chunks):
    i = pl.program_id(0)
    bsz, tc, w = xf_ref.shape
    rows = bsz * tc
    ti = lax.broadcasted_iota(jnp.int32, (bsz, tc, w), 1)

    def coeffs(x_ref, prev_ref, next_ref, chunk, d, a_s, b_s):
        has_prev = (chunk >= 2).astype(F32)
        has_next = jnp.logical_and(chunk >= 1, chunk <= n_chunks - 2).astype(F32)
        x = x_ref[...]
        p1 = prev_ref[:, 7:8, :] * has_prev
        n0 = next_ref[:, 0:1, :] * has_next
        n1 = next_ref[:, 1:2, :] * has_next
        x2 = x.reshape(rows, w)
        xm1 = jnp.where(ti == 0, p1, pltpu.roll(x2, 1, 0).reshape(bsz, tc, w))
        xp1 = jnp.where(ti == tc - 1, n0, pltpu.roll(x2, rows - 1, 0).reshape(bsz, tc, w))
        xp2 = jnp.where(ti == tc - 1, n1,
                        jnp.where(ti == tc - 2, n0, pltpu.roll(x2, rows - 2, 0).reshape(bsz, tc, w)))
        cw = cw_ref[...]
        xc = (cb_ref[...] + xm1 * cw[0:1] + x * cw[1:2] + xp1 * cw[2:3] + xp2 * cw[3:4]).reshape(rows, w)
        xcb = xc.astype(BF16)
        r = jax.nn.sigmoid(_dot(xcb, wa_ref[d]) + ba_ref[d])
        ig = jax.nn.sigmoid(_dot(xcb, wi_ref[d]) + bi_ref[d])
        log_a = (-LRU_C) * r * jax.nn.softplus(-lam_ref[d])
        a = jnp.exp(log_a)
        b = jnp.sqrt(1.0 - a * a) * (ig * xc)
        for j in range(w // LANES):
            a_s[j] = a[:, j * LANES:(j + 1) * LANES]
            b_s[j] = b[:, j * LANES:(j + 1) * LANES]

    chunk_b = jnp.where(i == 0, 0, n_chunks - i)
    coeffs(xf_ref, xfp_ref, xfn_ref, i, 0, af_s, bf_s)
    coeffs(xb_ref, xbp_ref, xbn_ref, chunk_b, 1, ab_s, bb_s)

    @pl.when(i == 0)
    def _init():
        sf_s[...] = jnp.zeros_like(sf_s)
        sb_s[...] = jnp.zeros_like(sb_s)

    nl = w // LANES

    def step(t, carry):
        fwd = pl.ds(t, bsz, stride=tc)
        bwd = pl.ds(tc - 1 - t, bsz, stride=tc)
        out = []
        for j in range(nl):
            hf = af_s[j, fwd, :] * carry[j] + bf_s[j, fwd, :]
            hb = ab_s[j, bwd, :] * carry[nl + j] + bb_s[j, bwd, :]
            of_s[j, fwd, :] = hf
            ob_s[j, bwd, :] = hb
            out.append((hf, hb))
        return tuple(o[0] for o in out) + tuple(o[1] for o in out)

    init = tuple(sf_s[j] for j in range(nl)) + tuple(sb_s[j] for j in range(nl))
    fin = lax.fori_loop(0, tc, step, init, unroll=8)
    for j in range(nl):
        sf_s[j] = fin[j]
        sb_s[j] = fin[nl + j]
    hf_ref[...] = jnp.concatenate([of_s[j] for j in range(nl)], axis=1).reshape(bsz, tc, w)
    hb_ref[...] = jnp.concatenate([ob_s[j] for j in range(nl)], axis=1).reshape(bsz, tc, w)


def _lru(bx, conv_w, conv_b, wa, ba, wi, bi, lam, *, ctx_len):
    bsz, L, w = bx.shape
    tc = ctx_len
    nc = L // tc
    hb_blocks = tc // 8
    last8 = L // 8 - 1

    def fchunk(i):
        return i

    def bchunk(i):
        return jnp.where(i == 0, 0, nc - i)

    def cur(cf):
        return pl.BlockSpec((bsz, tc, w), lambda i: (0, cf(i), 0))

    def prev(cf):
        return pl.BlockSpec((bsz, 8, w), lambda i: (0, jnp.maximum(cf(i) * hb_blocks - 1, 0), 0))

    def nxt(cf):
        return pl.BlockSpec((bsz, 8, w), lambda i: (0, jnp.minimum((cf(i) + 1) * hb_blocks, last8), 0))

    small = [conv_w, conv_b, wa, ba, wi, bi, lam]
    nl = w // LANES
    scr = [pltpu.VMEM((nl, bsz * tc, LANES), F32)] * 6 + [pltpu.VMEM((nl, bsz, LANES), F32)] * 2
    return pl.pallas_call(
        functools.partial(_lru_kernel, n_chunks=nc),
        grid=(nc,),
        in_specs=[cur(fchunk), prev(fchunk), nxt(fchunk), cur(bchunk), prev(bchunk), nxt(bchunk)]
                 + [_full(a) for a in small],
        out_specs=[cur(fchunk), cur(bchunk)],
        out_shape=[jax.ShapeDtypeStruct((bsz, L, w), F32)] * 2,
        scratch_shapes=scr,
        compiler_params=_params("arbitrary"),
    )(bx, bx, bx, bx, bx, bx, *small)


def _route(logits_t, bias, n_experts):
    per = n_experts // N_EXPERT_GROUPS
    tm = logits_t.shape[-1]
    scores = jax.nn.sigmoid(logits_t).reshape(N_EXPERT_GROUPS, per, tm)
    sel = scores + bias.reshape(N_EXPERT_GROUPS, per, 1)
    shape = sel.shape
    gi = lax.broadcasted_iota(jnp.int32, shape, 0)
    mi = lax.broadcasted_iota(jnp.int32, shape, 1)
    ei = gi * per + mi
    m1 = jnp.max(sel, axis=1, keepdims=True)
    first = jnp.min(jnp.where(sel == m1, mi, per), axis=1, keepdims=True)
    m2 = jnp.max(jnp.where(mi == first, NEG_INF, sel), axis=1, keepdims=True)
    gscore = m1 + m2
    gidx = lax.broadcasted_iota(jnp.int32, gscore.shape, 0)
    gmask = jnp.zeros(gscore.shape, F32)
    for _ in range(TOPK_GROUPS):
        m = jnp.max(gscore, axis=0, keepdims=True)
        pick = jnp.min(jnp.where(gscore == m, gidx, N_EXPERT_GROUPS), axis=0, keepdims=True)
        hit = gidx == pick
        gmask = jnp.where(hit, 1.0, gmask)
        gscore = jnp.where(hit, NEG_INF, gscore)
    cand = jnp.where(gmask > 0.0, sel, NEG_INF)
    chosen = jnp.zeros(shape, F32)
    for _ in range(TOP_K):
        m = jnp.max(jnp.max(cand, axis=1, keepdims=True), axis=0, keepdims=True)
        pick = jnp.where(cand == m, ei, n_experts)
        pick = jnp.min(jnp.min(pick, axis=1, keepdims=True), axis=0, keepdims=True)
        hit = ei == pick
        chosen = jnp.where(hit, 1.0, chosen)
        cand = jnp.where(hit, NEG_INF, cand)
    wsel = jnp.where(chosen > 0.0, scores, 0.0)
    den = jnp.sum(jnp.sum(wsel, axis=1, keepdims=True), axis=0, keepdims=True)
    return (wsel / den * ROUTED_SCALE).reshape(n_experts, tm)


def _outproj_kernel(ya_ref, hf_ref, hb_ref, bg_ref, yc_ref, yd_ref, x_ref, mod_ref,
                    wout_ref, gpost_ref, gffn_ref, rw_ref, rb_ref,
                    xo_ref, v_ref, gate_ref, cnt_ref, *, n_experts):
    gw = ya_ref.shape[-1]
    m = mod_ref[0, 0]
    yb = ((hf_ref[0] + hb_ref[0]) * jax.nn.gelu(bg_ref[0])).astype(BF16)
    y = (_dot(ya_ref[0], wout_ref[0:gw, :]) + _dot(yb, wout_ref[gw:2 * gw, :])
         + _dot(yc_ref[0], wout_ref[2 * gw:3 * gw, :]) + _dot(yd_ref[0], wout_ref[3 * gw:4 * gw, :]))
    x1 = x_ref[0] + m[2:3] * _rms(y, gpost_ref[...])
    xo_ref[0] = x1
    v = (_rms(x1, gffn_ref[...]) * (1.0 + m[4:5]) + m[3:4]).astype(BF16)
    v_ref[0] = v
    logits_t = _dot_t(rw_ref[...], v)
    gates_t = _route(logits_t, rb_ref[...], n_experts)
    pad = jnp.zeros((LANES - n_experts, gates_t.shape[1]), F32)
    gates = jnp.concatenate([gates_t, pad], axis=0).T
    gate_ref[0] = gates
    cnt_ref[0, 0] = jnp.sum(jnp.where(gates > 0.0, 1.0, 0.0), axis=0, keepdims=True).astype(jnp.int32)


def _outproj(ya, hf, hb, bg, yc, yd, xs, modtab, w_out, gpost, gffn, rw_t, rbias):
    bsz, L, d = xs.shape
    n_experts = rw_t.shape[0]
    gw = ya.shape[-1]
    blk = lambda w: pl.BlockSpec((1, ROW_TILE, w), lambda b, t: (b, t, 0))
    return pl.pallas_call(
        functools.partial(_outproj_kernel, n_experts=n_experts),
        grid=(bsz, L // ROW_TILE),
        in_specs=[blk(gw)] * 6 + [blk(d),
                  pl.BlockSpec((1, 1, 6, d), lambda b, t: (b, jnp.minimum(t, 1), 0, 0)),
                  _full(w_out), _full(gpost), _full(gffn), _full(rw_t), _full(rbias)],
        out_specs=[blk(d), blk(d), blk(LANES),
                   pl.BlockSpec((1, 1, 1, LANES), lambda b, t: (b, t, 0, 0))],
        out_shape=[jax.ShapeDtypeStruct((bsz, L, d), F32), jax.ShapeDtypeStruct((bsz, L, d), BF16),
                   jax.ShapeDtypeStruct((bsz, L, LANES), F32),
                   jax.ShapeDtypeStruct((bsz, L // ROW_TILE, 1, LANES), jnp.int32)],
        compiler_params=_params("arbitrary", "arbitrary"),
    )(ya, hf, hb, bg, yc, yd, xs, modtab, w_out, gpost, gffn, rw_t, rbias)


SLOT_ALIGN = 16
SLOT_BITS = 5
EXPERT_BLOCK = 512
SLOT_CHUNK = 512
CODE_BASE = 64.0


def _swiglu(v, wg, wu, wd):
    hg = _dot(v, wg.astype(BF16))
    hu = _dot(v, wu.astype(BF16))
    return hg * jax.nn.sigmoid(hg) * hu, wd.astype(BF16)


def _slot_cap(n_experts):
    rows = ROW_TILE * TOP_K + n_experts * (SLOT_ALIGN - 1)
    return -(-rows // SLOT_CHUNK) * SLOT_CHUNK


def _slot_codes(gates, off_row, ltri):
    chosen = gates > 0.0
    rank = _dot(ltri, jnp.where(chosen, 1.0, 0.0).astype(BF16))
    code = jnp.where(chosen, off_row.astype(F32) + rank + 1.0, 0.0)
    hi = jnp.floor(code * (1.0 / CODE_BASE))
    return hi.astype(BF16), (code - CODE_BASE * hi).astype(BF16)


def _slot_expert_onehot(off_row, len_row, first, rows):
    r = first + lax.broadcasted_iota(jnp.int32, (rows, LANES), 0)
    inside = jnp.where(r >= off_row, jnp.where(r < off_row + len_row, 1.0, 0.0), 0.0)
    return inside.astype(BF16)


def _row_digits(first, shape, axis):
    code = (first + 1 + lax.broadcasted_iota(jnp.int32, shape, axis)).astype(F32)
    hi = jnp.floor(code * (1.0 / CODE_BASE))
    return hi, code - CODE_BASE * hi


def _slot_dma(src, dst, src_off, dst_off, n_units, sem, start):
    for bit in range(SLOT_BITS):
        size = SLOT_ALIGN << bit
        done = (n_units & ((1 << bit) - 1)) * SLOT_ALIGN

        @pl.when(((n_units >> bit) & 1) == 1)
        def _piece():
            cp = pltpu.make_async_copy(
                src.at[pl.ds(pl.multiple_of(src_off + done, SLOT_ALIGN), size)],
                dst.at[pl.ds(pl.multiple_of(dst_off + done, SLOT_ALIGN), size)], sem)
            if start:
                cp.start()
            else:
                cp.wait()


def _dispatch_kernel(pos_ref, off_ref, len_ref, tpos_ref, tlen_ref,
                     v_ref, g_ref, offv_ref, lenv_ref, ltri_ref, xs_ref,
                     buf_s, zero_s, sem, *, n_experts):
    i = pl.program_id(0)
    nt = pl.num_programs(0)
    cur = i % 2
    cap = buf_s.shape[1]

    @pl.when(i == 0)
    def _zero():
        zero_s[...] = jnp.zeros_like(zero_s)

    g = g_ref[...]
    off_row, len_row = offv_ref[0], lenv_ref[0]
    hi, lo = _slot_codes(g, off_row, ltri_ref[...])
    g_hi = g.astype(BF16)
    g_lo = (g - g_hi.astype(F32)).astype(BF16)
    src = jnp.concatenate([v_ref[...], g_hi, g_lo], axis=1)
    for c in range(cap // SLOT_CHUNK):
        first = c * SLOT_CHUNK
        oh = _slot_expert_onehot(off_row, len_row, first, SLOT_CHUNK)
        rh, rl = _row_digits(first, (SLOT_CHUNK, 1), 0)
        pick = jnp.where(_dot_t(oh, hi) == rh, jnp.where(_dot_t(oh, lo) == rl, 1.0, 0.0), 0.0)
        buf_s[cur, first:first + SLOT_CHUNK, :] = _dot(pick.astype(BF16), src).astype(BF16)

    def slots(tile, which, start):
        def body(e, carry):
            k = tile * n_experts + e
            _slot_dma(buf_s.at[which], xs_ref, off_ref[k], pos_ref[k], len_ref[k], sem.at[0], start)
            return carry
        lax.fori_loop(0, n_experts, body, 0)

    def tails(start):
        def body(e, carry):
            _slot_dma(zero_s, xs_ref, 0, tpos_ref[e], tlen_ref[e], sem.at[0], start)
            return carry
        lax.fori_loop(0, n_experts, body, 0)

    @pl.when(i > 0)
    def _drain_previous():
        slots(i - 1, 1 - cur, False)

    slots(i, cur, True)

    @pl.when(i == nt - 1)
    def _last():
        tails(True)
        slots(i, cur, False)
        tails(False)


def _dispatch(v2, gates2, plan, ltri, *, n_experts, n_rows):
    t, d = v2.shape
    nt = t // ROW_TILE
    cap = _slot_cap(n_experts)
    width = d + 2 * LANES
    grid_spec = pltpu.PrefetchScalarGridSpec(
        num_scalar_prefetch=5, grid=(nt,),
        in_specs=[pl.BlockSpec((ROW_TILE, d), lambda i, *_: (i, 0)),
                  pl.BlockSpec((ROW_TILE, LANES), lambda i, *_: (i, 0)),
                  pl.BlockSpec((1, 1, LANES), lambda i, *_: (i, 0, 0)),
                  pl.BlockSpec((1, 1, LANES), lambda i, *_: (i, 0, 0)),
                  pl.BlockSpec(ltri.shape, lambda i, *_: (0, 0))],
        out_specs=pl.BlockSpec(memory_space=pl.ANY),
        scratch_shapes=[pltpu.VMEM((2, cap, width), BF16), pltpu.VMEM((EXPERT_BLOCK, width), BF16),
                        pltpu.SemaphoreType.DMA((1,))])
    return pl.pallas_call(
        functools.partial(_dispatch_kernel, n_experts=n_experts),
        grid_spec=grid_spec,
        out_shape=jax.ShapeDtypeStruct((n_rows, width), BF16),
        compiler_params=_params("arbitrary"),
    )(plan["pos"], plan["off"], plan["len"], plan["tail_pos"], plan["tail_len"],
      v2, gates2, plan["off_v"], plan["len_v"], ltri)


def _expert_kernel(be_ref, na_ref, x_ref, wg_ref, wu_ref, wd_ref, y_ref):
    b = pl.program_id(0)

    @pl.when(b < na_ref[0])
    def _active():
        d = wg_ref.shape[1]
        x = x_ref[:, 0:d]
        gates = x_ref[:, d:d + LANES].astype(F32) + x_ref[:, d + LANES:d + 2 * LANES].astype(F32)
        lane = lax.broadcasted_iota(jnp.int32, gates.shape, 1)
        gcol = jnp.sum(jnp.where(lane == be_ref[b], gates, 0.0), axis=1, keepdims=True)
        h, wd = _swiglu(x, wg_ref[0], wu_ref[0], wd_ref[0])
        y_ref[...] = (_dot(h.astype(BF16), wd) * gcol).astype(BF16)


def _experts(xs_sorted, plan, wg, wu, wd):
    n_rows, width = xs_sorted.shape
    _, d, de = wg.shape
    nb = n_rows // EXPERT_BLOCK
    rowmap = lambda b, be, na: (jnp.minimum(b, na[0] - 1), 0)
    grid_spec = pltpu.PrefetchScalarGridSpec(
        num_scalar_prefetch=2, grid=(nb,),
        in_specs=[pl.BlockSpec((EXPERT_BLOCK, width), rowmap),
                  pl.BlockSpec((1, d, de), lambda b, be, na: (be[b], 0, 0)),
                  pl.BlockSpec((1, d, de), lambda b, be, na: (be[b], 0, 0)),
                  pl.BlockSpec((1, de, d), lambda b, be, na: (be[b], 0, 0))],
        out_specs=pl.BlockSpec((EXPERT_BLOCK, d), rowmap))
    return pl.pallas_call(
        _expert_kernel, grid_spec=grid_spec,
        out_shape=jax.ShapeDtypeStruct((n_rows, d), BF16),
        compiler_params=_params("arbitrary"),
    )(plan["block_expert"], plan["n_active"], xs_sorted, wg, wu, wd)


def _combine_kernel(pos_ref, off_ref, len_ref,
                    ys_ref, v_ref, g_ref, offv_ref, lenv_ref, ltri_ref, x_ref, mod_ref,
                    sg_ref, su_ref, sd_ref, gpost_ref, o_ref, buf_s, sem, *, n_experts):
    i = pl.program_id(0)
    nt = pl.num_programs(0)
    cur = i % 2
    cap = buf_s.shape[1]

    def slots(tile, which, start):
        def body(e, carry):
            k = tile * n_experts + e
            _slot_dma(ys_ref, buf_s.at[which], pos_ref[k], off_ref[k], len_ref[k], sem.at[which], start)
            return carry
        lax.fori_loop(0, n_experts, body, 0)

    @pl.when(i == 0)
    def _first():
        buf_s[...] = jnp.zeros_like(buf_s)
        slots(0, 0, True)

    @pl.when(i + 1 < nt)
    def _prefetch():
        slots(i + 1, 1 - cur, True)

    v = v_ref[...]
    h, wd = _swiglu(v, sg_ref[...], su_ref[...], sd_ref[...])
    acc = _dot(h.astype(BF16), wd)
    off_row, len_row = offv_ref[0], lenv_ref[0]
    hi, lo = _slot_codes(g_ref[...], off_row, ltri_ref[...])

    slots(i, cur, False)
    for c in range(cap // SLOT_CHUNK):
        first = c * SLOT_CHUNK
        oh = _slot_expert_onehot(off_row, len_row, first, SLOT_CHUNK)
        rh, rl = _row_digits(first, (1, SLOT_CHUNK), 1)
        pick = jnp.where(_dot_t(hi, oh) == rh, jnp.where(_dot_t(lo, oh) == rl, 1.0, 0.0), 0.0)
        acc = acc + _dot(pick.astype(BF16), buf_s[cur, first:first + SLOT_CHUNK, :])
    o_ref[0] = x_ref[0] + mod_ref[0, 0][5:6] * _rms(acc, gpost_ref[...])


def _combine(ys_sorted, v2, gates2, plan, ltri, xs, modtab, sg, su, sd, gpost, *, n_experts):
    bsz, L, d = xs.shape
    tpb = L // ROW_TILE
    nt = bsz * tpb
    cap = _slot_cap(n_experts)
    full = lambda a: pl.BlockSpec(a.shape, lambda i, *_: (0,) * a.ndim)
    grid_spec = pltpu.PrefetchScalarGridSpec(
        num_scalar_prefetch=3, grid=(nt,),
        in_specs=[pl.BlockSpec(memory_space=pl.ANY),
                  pl.BlockSpec((ROW_TILE, d), lambda i, *_: (i, 0)),
                  pl.BlockSpec((ROW_TILE, LANES), lambda i, *_: (i, 0)),
                  pl.BlockSpec((1, 1, LANES), lambda i, *_: (i, 0, 0)),
                  pl.BlockSpec((1, 1, LANES), lambda i, *_: (i, 0, 0)),
                  full(ltri),
                  pl.BlockSpec((1, ROW_TILE, d), lambda i, *_: (i // tpb, i % tpb, 0)),
                  pl.BlockSpec((1, 1, 6, d), lambda i, *_: (i // tpb, jnp.minimum(i % tpb, 1), 0, 0)),
                  full(sg), full(su), full(sd), full(gpost)],
        out_specs=pl.BlockSpec((1, ROW_TILE, d), lambda i, *_: (i // tpb, i % tpb, 0)),
        scratch_shapes=[pltpu.VMEM((2, cap, d), BF16), pltpu.SemaphoreType.DMA((2,))])
    return pl.pallas_call(
        functools.partial(_combine_kernel, n_experts=n_experts),
        grid_spec=grid_spec,
        out_shape=jax.ShapeDtypeStruct((bsz, L, d), F32),
        compiler_params=_params("arbitrary"),
    )(plan["pos"], plan["off"], plan["len"], ys_sorted, v2, gates2, plan["off_v"], plan["len_v"], ltri,
      xs, modtab, sg, su, sd, gpost)


def _moe_plan(counts, n_experts, n_blocks):
    a = (counts + (SLOT_ALIGN - 1)) // SLOT_ALIGN * SLOT_ALIGN
    rows = jnp.sum(a, axis=0)
    region = (rows + (EXPERT_BLOCK - 1)) // EXPERT_BLOCK * EXPERT_BLOCK
    region_end = jnp.cumsum(region)
    region_start = region_end - region
    pos = region_start[None, :] + jnp.cumsum(a, axis=0) - a
    off = jnp.cumsum(a, axis=1) - a
    first_row = jnp.arange(n_blocks, dtype=jnp.int32) * EXPERT_BLOCK
    block_expert = jnp.minimum(jnp.searchsorted(region_end, first_row, side="right"), n_experts - 1)
    flat = lambda t: t[:, :n_experts].reshape(-1).astype(jnp.int32)
    nt = counts.shape[0]
    return {
        "pos": flat(pos), "off": flat(off), "len": flat(a // SLOT_ALIGN),
        "tail_pos": (region_start + rows)[:n_experts].astype(jnp.int32),
        "tail_len": ((region - rows) // SLOT_ALIGN)[:n_experts].astype(jnp.int32),
        "off_v": off.reshape(nt, 1, LANES).astype(jnp.int32),
        "len_v": a.reshape(nt, 1, LANES).astype(jnp.int32),
        "block_expert": block_expert.astype(jnp.int32),
        "n_active": (region_end[-1:] // EXPERT_BLOCK).astype(jnp.int32),
    }


def _moe(v, gates, counts, xs, modtab, wg, wu, wd, sg, su, sd, gpost):
    bsz, L, d = xs.shape
    n_experts = wg.shape[0]
    t = bsz * L
    nt = t // ROW_TILE
    worst = t * TOP_K + nt * n_experts * (SLOT_ALIGN - 1) + n_experts * (EXPERT_BLOCK - 1)
    n_blocks = -(-worst // EXPERT_BLOCK)
    plan = _moe_plan(counts.reshape(nt, LANES), n_experts, n_blocks)
    ltri = jnp.tril(jnp.ones((ROW_TILE, ROW_TILE), F32), -1).astype(BF16)
    v2, gates2 = v.reshape(t, d), gates.reshape(t, LANES)
    xs_sorted = _dispatch(v2, gates2, plan, ltri, n_experts=n_experts, n_rows=n_blocks * EXPERT_BLOCK)
    ys_sorted = _experts(xs_sorted, plan, wg, wu, wd)
    return _combine(ys_sorted, v2, gates2, plan, ltri, xs, modtab, sg, su, sd, gpost, n_experts=n_experts)


def _tables(seq, ctx_len):
    def build(rot_dim, lead, slot):
        n = rot_dim // 4
        pos = jnp.arange(seq)
        row = (pos // GRID_W).astype(F32)
        col = (pos % GRID_W).astype(F32)
        inv = ROPE_THETA ** (-jnp.arange(n, dtype=F32) / n)
        ang = jnp.concatenate([row[:, None] * inv, col[:, None] * inv], axis=-1)
        cos, sin = jnp.cos(ang), jnp.sin(ang)
        zero = jnp.zeros_like(sin)
        tail = slot - lead - rot_dim
        one_l, zero_l = jnp.ones((seq, lead), F32), jnp.zeros((seq, lead), F32)
        one_t, zero_t = jnp.ones((seq, tail), F32), jnp.zeros((seq, tail), F32)
        c = jnp.concatenate([one_l, cos, cos, one_t], axis=1)
        s_lo = jnp.concatenate([zero_l, -sin, zero, zero_t], axis=1)
        s_hi = jnp.concatenate([zero_l, zero, sin, zero_t], axis=1)
        ctx_c = jnp.ones((ctx_len, slot), F32)
        ctx_s = jnp.zeros((ctx_len, slot), F32)
        return [jnp.concatenate([ctx_c, c], axis=0), jnp.concatenate([ctx_s, s_lo], axis=0),
                jnp.concatenate([ctx_s, s_hi], axis=0)]

    hd = [jnp.concatenate([t, t], axis=1) for t in build(HEAD_DIM, 0, HEAD_DIM)]
    mla = build(QK_ROPE, QK_NOPE, LANES)
    return hd + mla


def _dup_heads(w, n_heads):
    d = w.shape[0]
    wh = w.reshape(d, n_heads, 1, HEAD_DIM)
    return jnp.broadcast_to(wh, (d, n_heads, 2, HEAD_DIM)).reshape(d, n_heads * 2 * HEAD_DIM)


def kernel(x, c, ctx, c_ctx, w_mod, b_mod, g_pre_mix, g_post_mix, g_pre_ffn, g_post_ffn, w_in, a_q_norm, a_k_norm, conv_w, conv_b, lru_wa, lru_ba, lru_wi, lru_bi, lru_lambda, c_sink, d_q_norm, d_w_uq, d_kv_norm, d_w_ukv, w_out, router_w, router_bias, w_gate, w_up, w_down, sh_gate, sh_up, sh_down):
    bsz, seq, d = x.shape
    ctx_len = ctx.shape[1]
    depth = w_mod.shape[0]
    gw = d // 4
    lru_w = conv_w.shape[-1]
    q_lora = d_q_norm.shape[-1]
    kv_lora = d_kv_norm.shape[-1]
    n_experts = router_w.shape[-1]
    a_heads = gw // HEAD_DIM
    a_kv = (w_in.shape[-1] - (2 * gw + 2 * lru_w + q_lora + kv_lora + QK_ROPE)) // (4 * HEAD_DIM)
    mla_scale = (QK_NOPE + QK_ROPE) ** -0.5
    v_dim = gw // D_HEADS
    assert ctx_len == ROW_TILE and a_kv == 2 and a_heads == 4 and n_experts <= LANES

    cond = jnp.zeros((16, d), F32).at[:bsz].set(c).at[bsz].set(c_ctx)
    mod = _modulation(cond, w_mod, b_mod).reshape(depth, 16, 6, d)
    tabs = _tables(seq, ctx_len)
    seg = jnp.kron(jnp.eye(a_heads, dtype=F32), jnp.ones((HEAD_DIM, HEAD_DIM), F32)).astype(BF16)

    xs = jnp.concatenate([ctx, x], axis=1)
    for l in range(depth):
        modtab = jnp.stack([jnp.broadcast_to(mod[l, bsz], (bsz, 6, d)), mod[l, :bsz]], axis=1)

        offs = [0]
        for wdt in (gw, a_kv * HEAD_DIM, a_kv * HEAD_DIM, lru_w, lru_w, gw, a_kv * HEAD_DIM, a_kv * HEAD_DIM,
                    q_lora, kv_lora, QK_ROPE):
            offs.append(offs[-1] + wdt)
        col = lambda i: w_in[l][:, offs[i]:offs[i + 1]]
        zeros = lambda n: jnp.zeros((d, n), F32)
        kr_slot = jnp.concatenate([zeros(QK_NOPE), col(10), zeros(LANES - QK_NOPE - QK_ROPE)], axis=1)
        w_in_p = jnp.concatenate(
            [col(0), _dup_heads(col(1), a_kv), _dup_heads(col(2), a_kv), col(3), col(4),
             col(5), _dup_heads(col(6), a_kv), _dup_heads(col(7), a_kv),
             col(8), zeros(2 * LANES - q_lora), col(9)] + [kr_slot] * D_HEADS, axis=1).astype(BF16)

        qk = QK_NOPE + QK_ROPE
        wuq = d_w_uq[l].reshape(q_lora, D_HEADS, qk)
        wuq = jnp.pad(wuq, ((0, 2 * LANES - q_lora), (0, 0), (0, LANES - qk))).reshape(2 * LANES, D_HEADS * LANES)
        wukv = d_w_ukv[l].reshape(kv_lora, D_HEADS, QK_NOPE + v_dim)
        wk = jnp.pad(wukv[:, :, :QK_NOPE], ((0, 0), (0, 0), (0, LANES - QK_NOPE))).reshape(kv_lora, D_HEADS * LANES)
        wv = wukv[:, :, QK_NOPE:].reshape(kv_lora, D_HEADS * v_dim)
        wukv_p = jnp.concatenate([wk, wv], axis=1).astype(BF16)
        dqn = jnp.pad(d_q_norm[l], (0, 2 * LANES - q_lora)).reshape(1, 2 * LANES)

        qa, ka, va, bx, bg, qc, kc, vc, qd, kd, vd = _inproj(
            xs, modtab, g_pre_mix[l].reshape(1, d), w_in_p, seg, tabs,
            jnp.tile(a_q_norm[l], a_heads).reshape(1, gw), jnp.tile(a_k_norm[l], 2 * a_kv).reshape(1, gw),
            dqn, wuq.astype(BF16), d_kv_norm[l].reshape(1, kv_lora), wukv_p,
            q_lora=q_lora, mla_scale=mla_scale)

        ya = _attention(qa, ka, va, split_q=True, ctx_len=ctx_len)
        yd = _attention(qd, kd, vd, split_q=False, ctx_len=ctx_len)
        yc = _window_attention(qc, kc, vc, c_sink[l], ctx_len=ctx_len)

        blocks = lru_wa.shape[2]
        bdiag = lambda wts: jnp.stack([jax.scipy.linalg.block_diag(*[wts[dd, h] for h in range(blocks)])
                                       for dd in range(2)]).astype(BF16)
        hf, hb = _lru(bx, conv_w[l], conv_b[l].reshape(1, lru_w), bdiag(lru_wa[l]), lru_ba[l].reshape(2, 1, lru_w),
                      bdiag(lru_wi[l]), lru_bi[l].reshape(2, 1, lru_w), lru_lambda[l].reshape(2, 1, lru_w),
                      ctx_len=ctx_len)

        rw_t = router_w[l].T.astype(BF16)
        xs_mid, v, gates, counts = _outproj(ya, hf, hb, bg, yc, yd, xs, modtab, w_out[l].astype(BF16),
                                            g_post_mix[l].reshape(1, d), g_pre_ffn[l].reshape(1, d),
                                            rw_t, router_bias[l].reshape(n_experts, 1))
        xs = _moe(v, gates, counts, xs_mid, modtab, w_gate[l], w_up[l], w_down[l],
                  sh_gate[l], sh_up[l], sh_down[l], g_post_ffn[l].reshape(1, d))
    return xs[:, ctx_len:, :]
```

```python
import functools

import jax
import jax.numpy as jnp
from jax import lax
from jax.experimental import pallas as pl
from jax.experimental.pallas import tpu as pltpu

F32 = jnp.float32
BF16 = jnp.bfloat16

GRID_W = 64
HEAD_DIM = 64
ROPE_THETA = 10000.0
NORM_EPS = 1e-6
WINDOW = 128
Q_BLOCK = 128
D_HEADS = 4
QK_NOPE = 64
QK_ROPE = 32
LRU_C = 8.0
CONV_W = 4
TOP_K = 8
N_EXPERT_GROUPS = 8
TOPK_GROUPS = 4
ROUTED_SCALE = 2.5

LANES = 128
VMEM_LIMIT = 56 * 1024 * 1024
ROW_TILE = 256

NEG_INF = float("-inf")


def _params(*sem):
    return pltpu.CompilerParams(dimension_semantics=sem, vmem_limit_bytes=VMEM_LIMIT)


def _rms(x, gain):
    return x * lax.rsqrt(jnp.mean(x * x, axis=-1, keepdims=True) + NORM_EPS) * gain


def _dot(a, b):
    return jnp.dot(a, b, preferred_element_type=F32)


def _dot_t(a, b):
    return lax.dot_general(a, b, (((1,), (1,)), ((), ())), preferred_element_type=F32)


def _full(a):
    return pl.BlockSpec(a.shape, lambda *_: (0,) * a.ndim)


def _mod_kernel(c_ref, w_ref, b_ref, o_ref):
    a = c_ref[...]
    a = a * jax.nn.sigmoid(a)
    o_ref[0] = _dot(a.astype(BF16), w_ref[0].astype(BF16)) + b_ref[0]


def _modulation(cond, w_mod, b_mod):
    depth, d, n = w_mod.shape
    tn = n // 4
    rows = cond.shape[0]
    return pl.pallas_call(
        _mod_kernel,
        grid=(depth, n // tn),
        in_specs=[pl.BlockSpec((rows, d), lambda l, j: (0, 0)),
                  pl.BlockSpec((1, d, tn), lambda l, j: (l, 0, j)),
                  pl.BlockSpec((1, 1, tn), lambda l, j: (l, 0, j))],
        out_specs=pl.BlockSpec((1, rows, tn), lambda l, j: (l, 0, j)),
        out_shape=jax.ShapeDtypeStruct((depth, rows, n), F32),
        compiler_params=_params("arbitrary", "arbitrary"),
    )(cond, w_mod, b_mod.reshape(depth, 1, n))


def _rope(t, cos, sin_lo, sin_hi, half):
    w = t.shape[-1]
    return t * cos + pltpu.roll(t, w - half, 1) * sin_lo + pltpu.roll(t, half, 1) * sin_hi


def _inproj_kernel(x_ref, mod_ref, gpre_ref, w_ref, seg_ref,
                   cos_ref, sinl_ref, sinh_ref, cosd_ref, sindl_ref, sindh_ref,
                   aqn_ref, akn_ref, dqn_ref, wuq_ref, dkvn_ref, wukv_ref,
                   qa_ref, ka_ref, va_ref, bx_ref, bg_ref, qc_ref, kc_ref, vc_ref,
                   qd_ref, kd_ref, vd_ref, *, q_lora, mla_scale):
    x = x_ref[0]
    m = mod_ref[0, 0]
    u = (_rms(x, gpre_ref[...]) * (1.0 + m[1:2]) + m[0:1]).astype(BF16)
    proj = _dot(u, w_ref[...])

    seg = seg_ref[...]
    cos = jnp.concatenate([cos_ref[...]] * 2, axis=1)
    sin_lo = jnp.concatenate([sinl_ref[...]] * 2, axis=1)
    sin_hi = jnp.concatenate([sinh_ref[...]] * 2, axis=1)
    cosd = jnp.concatenate([cosd_ref[...]] * 4, axis=1)
    sind_lo = jnp.concatenate([sindl_ref[...]] * 4, axis=1)
    sind_hi = jnp.concatenate([sindh_ref[...]] * 4, axis=1)

    def head_rms(t, gain):
        sq = t * t
        hi = sq.astype(BF16)
        lo = (sq - hi.astype(F32)).astype(BF16)
        ms = (_dot(hi, seg) + _dot(lo, seg)) * (1.0 / HEAD_DIM)
        return t * lax.rsqrt(ms + NORM_EPS) * gain

    def rope_hd(t):
        return _rope(t, cos, sin_lo, sin_hi, HEAD_DIM // 2)

    def rope_r(t):
        return _rope(t, cosd, sind_lo, sind_hi, QK_ROPE // 2)

    sc_hd = HEAD_DIM ** -0.5
    qa_ref[0] = (rope_hd(head_rms(proj[:, 0:256], aqn_ref[...])) * sc_hd).astype(BF16)
    ka_ref[0] = rope_hd(head_rms(proj[:, 256:512], akn_ref[...])).astype(BF16)
    va_ref[0] = proj[:, 512:768].astype(BF16)
    bx_ref[0] = proj[:, 768:1024]
    bg_ref[0] = proj[:, 1024:1280]
    qc_ref[0] = (rope_hd(proj[:, 1280:1536]) * sc_hd).astype(BF16)
    kc_ref[0] = rope_hd(proj[:, 1536:1792]).astype(BF16)
    vc_ref[0] = proj[:, 1792:2048].astype(BF16)
    cq = proj[:, 2048:2304]
    cq = cq * lax.rsqrt(jnp.sum(cq * cq, axis=-1, keepdims=True) * (1.0 / q_lora) + NORM_EPS) * dqn_ref[...]
    qd = _dot(cq.astype(BF16), wuq_ref[...])
    qd_ref[0] = (rope_r(qd) * mla_scale).astype(BF16)
    ckv = _rms(proj[:, 2304:2432], dkvn_ref[...])
    kv = _dot(ckv.astype(BF16), wukv_ref[...])
    kd_ref[0] = (kv[:, 0:512] + rope_r(proj[:, 2432:2944])).astype(BF16)
    vd_ref[0] = kv[:, 512:768].astype(BF16)


def _inproj(xs, modtab, gpre, w_in_p, seg, tabs, aqn, akn, dqn, wuq, dkvn, wukv, *, q_lora, mla_scale):
    bsz, L, d = xs.shape
    nt = L // ROW_TILE
    rowblk = lambda w: pl.BlockSpec((1, ROW_TILE, w), lambda b, t: (b, t, 0))
    tab = lambda a: pl.BlockSpec((ROW_TILE, a.shape[1]), lambda b, t: (t, 0))
    out_w = [(256, BF16)] * 3 + [(256, F32)] * 2 + [(256, BF16)] * 3 + [(512, BF16), (512, BF16), (256, BF16)]
    return pl.pallas_call(
        functools.partial(_inproj_kernel, q_lora=q_lora, mla_scale=mla_scale),
        grid=(bsz, nt),
        in_specs=[rowblk(d),
                  pl.BlockSpec((1, 1, 6, d), lambda b, t: (b, jnp.minimum(t, 1), 0, 0)),
                  _full(gpre), _full(w_in_p), _full(seg)] + [tab(a) for a in tabs]
                 + [_full(a) for a in (aqn, akn, dqn, wuq, dkvn, wukv)],
        out_specs=[rowblk(w) for w, _ in out_w],
        out_shape=[jax.ShapeDtypeStruct((bsz, L, w), dt) for w, dt in out_w],
        compiler_params=_params("arbitrary", "arbitrary"),
    )(xs, modtab, gpre, w_in_p, seg, *tabs, aqn, akn, dqn, wuq, dkvn, wukv)


def _attn_kernel(qa_ref, qb_ref, ka_ref, kb_ref, v_ref, o_ref, *, split_q, ctx_len, k_len):
    lane = lax.broadcasted_iota(jnp.int32, (1, LANES), 1)
    low = lane < HEAD_DIM

    def run(n_keys):
        outs = []
        for h, (q_ref, k_ref) in enumerate(((qa_ref, ka_ref), (qb_ref, kb_ref))):
            q = q_ref[0]
            if split_q:
                keep = low if h == 0 else jnp.logical_not(low)
                q = jnp.where(keep, q, jnp.zeros_like(q))
            s = _dot_t(q, k_ref[0, 0:n_keys, :])
            p = jnp.exp(s - jnp.max(s, axis=-1, keepdims=True))
            den = jnp.sum(p, axis=-1, keepdims=True)
            outs.append(_dot(p.astype(BF16), v_ref[0, 0:n_keys, :]) / den)
        o_ref[0] = jnp.where(low, outs[0], outs[1]).astype(o_ref.dtype)

    t = pl.program_id(2)

    @pl.when(t == 0)
    def _ctx():
        run(ctx_len)

    @pl.when(t > 0)
    def _lat():
        run(k_len)


def _attention(q, k, v, *, split_q, ctx_len):
    bsz, L, _ = q.shape
    tq = ROW_TILE
    assert ctx_len == tq
    if split_q:
        qmap = [lambda b, g, t: (b, t, g)] * 2
        kmap = [lambda b, g, t: (b, 0, g)] * 2
    else:
        qmap = [lambda b, g, t: (b, t, 2 * g), lambda b, g, t: (b, t, 2 * g + 1)]
        kmap = [lambda b, g, t: (b, 0, 2 * g), lambda b, g, t: (b, 0, 2 * g + 1)]
    return pl.pallas_call(
        functools.partial(_attn_kernel, split_q=split_q, ctx_len=ctx_len, k_len=L),
        grid=(bsz, 2, L // tq),
        in_specs=[pl.BlockSpec((1, tq, LANES), qmap[0]), pl.BlockSpec((1, tq, LANES), qmap[1]),
                  pl.BlockSpec((1, L, LANES), kmap[0]), pl.BlockSpec((1, L, LANES), kmap[1]),
                  pl.BlockSpec((1, L, LANES), lambda b, g, t: (b, 0, g))],
        out_specs=pl.BlockSpec((1, tq, LANES), lambda b, g, t: (b, t, g)),
        out_shape=jax.ShapeDtypeStruct((bsz, L, 2 * LANES), BF16),
        compiler_params=_params("arbitrary", "arbitrary", "arbitrary"),
    )(q, q, k, k, v)


def _winattn_kernel(sink_ref, q_ref, k_ref, v_ref, o_ref, *, ctx_len, seq):
    g = pl.program_id(1)
    t = pl.program_id(2)
    lane = lax.broadcasted_iota(jnp.int32, (1, LANES), 1)
    low = lane < HEAD_DIM
    ctx_blocks = ctx_len // Q_BLOCK
    win = 3 * Q_BLOCK

    def finish(parts, sink):
        m = sink
        for s, _ in parts:
            m = jnp.maximum(m, jnp.max(s, axis=-1, keepdims=True))
        den = jnp.exp(sink - m)
        acc = None
        for s, vv in parts:
            p = jnp.exp(s - m)
            den = den + jnp.sum(p, axis=-1, keepdims=True)
            o = _dot(p.astype(BF16), vv)
            acc = o if acc is None else acc + o
        return acc / den

    def heads(fn):
        outs = []
        for h in range(2):
            keep = low if h == 0 else jnp.logical_not(low)
            q = q_ref[0]
            q = jnp.where(keep, q, jnp.zeros_like(q))
            outs.append(fn(q, sink_ref[2 * g + h]))
        o_ref[0] = jnp.where(low, outs[0], outs[1]).astype(o_ref.dtype)

    @pl.when(t < ctx_blocks)
    def _ctx():
        def one(q, sink):
            return finish([(_dot_t(q, k_ref[0, 0:ctx_len, :]), v_ref[0, 0:ctx_len, :])], sink)
        heads(one)

    @pl.when(t >= ctx_blocks)
    def _lat():
        n = t - ctx_blocks
        start = jnp.clip((n - 1) * Q_BLOCK, 0, seq - win)
        off = pl.multiple_of(ctx_len + start, Q_BLOCK)
        qpos = n * Q_BLOCK + lax.broadcasted_iota(jnp.int32, (Q_BLOCK, win), 0)
        kpos = start + lax.broadcasted_iota(jnp.int32, (Q_BLOCK, win), 1)
        allowed = jnp.abs(kpos - qpos) <= WINDOW

        def one(q, sink):
            s_ctx = _dot_t(q, k_ref[0, 0:ctx_len, :])
            s_win = jnp.where(allowed, _dot_t(q, k_ref[0, pl.ds(off, win), :]), NEG_INF)
            return finish([(s_ctx, v_ref[0, 0:ctx_len, :]), (s_win, v_ref[0, pl.ds(off, win), :])], sink)
        heads(one)


def _window_attention(q, k, v, sink, *, ctx_len):
    bsz, L, _ = q.shape
    return pl.pallas_call(
        functools.partial(_winattn_kernel, ctx_len=ctx_len, seq=L - ctx_len),
        grid=(bsz, 2, L // Q_BLOCK),
        in_specs=[pl.BlockSpec(memory_space=pltpu.SMEM),
                  pl.BlockSpec((1, Q_BLOCK, LANES), lambda b, g, t: (b, t, g)),
                  pl.BlockSpec((1, L, LANES), lambda b, g, t: (b, 0, g)),
                  pl.BlockSpec((1, L, LANES), lambda b, g, t: (b, 0, g))],
        out_specs=pl.BlockSpec((1, Q_BLOCK, LANES), lambda b, g, t: (b, t, g)),
        out_shape=jax.ShapeDtypeStruct((bsz, L, 2 * LANES), BF16),
        compiler_params=_params("arbitrary", "arbitrary", "arbitrary"),
    )(sink, q, k, v)


def _lru_kernel(xf_ref, xfp_ref, xfn_ref, xb_ref, xbp_ref, xbn_ref,
                cw_ref, cb_ref, wa_ref, ba_ref, wi_ref, bi_ref, lam_ref,
                hf_ref, hb_ref,
                af_s, bf_s, ab_s, bb_s, of_s, ob_s, sf_s, sb_s, *, n_chunks):
    i = pl.program_id(0)
    bsz, tc, w = xf_ref.shape
    rows = bsz * tc
    ti = lax.broadcasted_iota(jnp.int32, (bsz, tc, w), 1)

    def coeffs(x_ref, prev_ref, next_ref, chunk, d, a_s, b_s):
        has_prev = (chunk >= 2).astype(F32)
        has_next = jnp.logical_and(chunk >= 1, chunk <= n_chunks - 2).astype(F32)
        x = x_ref[...]
        p1 = prev_ref[:, 7:8, :] * has_prev
        n0 = next_ref[:, 0:1, :] * has_next
        n1 = next_ref[:, 1:2, :] * has_next
        x2 = x.reshape(rows, w)
        xm1 = jnp.where(ti == 0, p1, pltpu.roll(x2, 1, 0).reshape(bsz, tc, w))
        xp1 = jnp.where(ti == tc - 1, n0, pltpu.roll(x2, rows - 1, 0).reshape(bsz, tc, w))
        xp2 = jnp.where(ti == tc - 1, n1,
                        jnp.where(ti == tc - 2, n0, pltpu.roll(x2, rows - 2, 0).reshape(bsz, tc, w)))
        cw = cw_ref[...]
        xc = (cb_ref[...] + xm1 * cw[0:1] + x * cw[1:2] + xp1 * cw[2:3] + xp2 * cw[3:4]).reshape(rows, w)
        xcb = xc.astype(BF16)
        r = jax.nn.sigmoid(_dot(xcb, wa_ref[d]) + ba_ref[d])
        ig = jax.nn.sigmoid(_dot(xcb, wi_ref[d]) + bi_ref[d])
        log_a = (-LRU_C) * r * jax.nn.softplus(-lam_ref[d])
        a = jnp.exp(log_a)
        b = jnp.sqrt(1.0 - a * a) * (ig * xc)
        for j in range(w // LANES):
            a_s[j] = a[:, j * LANES:(j + 1) * LANES]
            b_s[j] = b[:, j * LANES:(j + 1) * LANES]

    chunk_b = jnp.where(i == 0, 0, n_chunks - i)
    coeffs(xf_ref, xfp_ref, xfn_ref, i, 0, af_s, bf_s)
    coeffs(xb_ref, xbp_ref, xbn_ref, chunk_b, 1, ab_s, bb_s)

    @pl.when(i == 0)
    def _init():
        sf_s[...] = jnp.zeros_like(sf_s)
        sb_s[...] = jnp.zeros_like(sb_s)

    nl = w // LANES

    def step(t, carry):
        fwd = pl.ds(t, bsz, stride=tc)
        bwd = pl.ds(tc - 1 - t, bsz, stride=tc)
        out = []
        for j in range(nl):
            hf = af_s[j, fwd, :] * carry[j] + bf_s[j, fwd, :]
            hb = ab_s[j, bwd, :] * carry[nl + j] + bb_s[j, bwd, :]
            of_s[j, fwd, :] = hf
            ob_s[j, bwd, :] = hb
            out.append((hf, hb))
        return tuple(o[0] for o in out) + tuple(o[1] for o in out)

    init = tuple(sf_s[j] for j in range(nl)) + tuple(sb_s[j] for j in range(nl))
    fin = lax.fori_loop(0, tc, step, init, unroll=8)
    for j in range(nl):
        sf_s[j] = fin[j]
        sb_s[j] = fin[nl + j]
    hf_ref[...] = jnp.concatenate([of_s[j] for j in range(nl)], axis=1).reshape(bsz, tc, w)
    hb_ref[...] = jnp.concatenate([ob_s[j] for j in range(nl)], axis=1).reshape(bsz, tc, w)


def _lru(bx, conv_w, conv_b, wa, ba, wi, bi, lam, *, ctx_len):
    bsz, L, w = bx.shape
    tc = ctx_len
    nc = L // tc
    hb_blocks = tc // 8
    last8 = L // 8 - 1

    def fchunk(i):
        return i

    def bchunk(i):
        return jnp.where(i == 0, 0, nc - i)

    def cur(cf):
        return pl.BlockSpec((bsz, tc, w), lambda i: (0, cf(i), 0))

    def prev(cf):
        return pl.BlockSpec((bsz, 8, w), lambda i: (0, jnp.maximum(cf(i) * hb_blocks - 1, 0), 0))

    def nxt(cf):
        return pl.BlockSpec((bsz, 8, w), lambda i: (0, jnp.minimum((cf(i) + 1) * hb_blocks, last8), 0))

    small = [conv_w, conv_b, wa, ba, wi, bi, lam]
    nl = w // LANES
    scr = [pltpu.VMEM((nl, bsz * tc, LANES), F32)] * 6 + [pltpu.VMEM((nl, bsz, LANES), F32)] * 2
    return pl.pallas_call(
        functools.partial(_lru_kernel, n_chunks=nc),
        grid=(nc,),
        in_specs=[cur(fchunk), prev(fchunk), nxt(fchunk), cur(bchunk), prev(bchunk), nxt(bchunk)]
                 + [_full(a) for a in small],
        out_specs=[cur(fchunk), cur(bchunk)],
        out_shape=[jax.ShapeDtypeStruct((bsz, L, w), F32)] * 2,
        scratch_shapes=scr,
        compiler_params=_params("arbitrary"),
    )(bx, bx, bx, bx, bx, bx, *small)


def _route(logits_t, bias, n_experts):
    per = n_experts // N_EXPERT_GROUPS
    tm = logits_t.shape[-1]
    scores = jax.nn.sigmoid(logits_t).reshape(N_EXPERT_GROUPS, per, tm)
    sel = scores + bias.reshape(N_EXPERT_GROUPS, per, 1)
    shape = sel.shape
    gi = lax.broadcasted_iota(jnp.int32, shape, 0)
    mi = lax.broadcasted_iota(jnp.int32, shape, 1)
    ei = gi * per + mi
    m1 = jnp.max(sel, axis=1, keepdims=True)
    first = jnp.min(jnp.where(sel == m1, mi, per), axis=1, keepdims=True)
    m2 = jnp.max(jnp.where(mi == first, NEG_INF, sel), axis=1, keepdims=True)
    gscore = m1 + m2
    gidx = lax.broadcasted_iota(jnp.int32, gscore.shape, 0)
    gmask = jnp.zeros(gscore.shape, F32)
    for _ in range(TOPK_GROUPS):
        m = jnp.max(gscore, axis=0, keepdims=True)
        pick = jnp.min(jnp.where(gscore == m, gidx, N_EXPERT_GROUPS), axis=0, keepdims=True)
        hit = gidx == pick
        gmask = jnp.where(hit, 1.0, gmask)
        gscore = jnp.where(hit, NEG_INF, gscore)
    cand = jnp.where(gmask > 0.0, sel, NEG_INF)
    chosen = jnp.zeros(shape, F32)
    for _ in range(TOP_K):
        m = jnp.max(jnp.max(cand, axis=1, keepdims=True), axis=0, keepdims=True)
        pick = jnp.where(cand == m, ei, n_experts)
        pick = jnp.min(jnp.min(pick, axis=1, keepdims=True), axis=0, keepdims=True)
        hit = ei == pick
        chosen = jnp.where(hit, 1.0, chosen)
        cand = jnp.where(hit, NEG_INF, cand)
    wsel = jnp.where(chosen > 0.0, scores, 0.0)
    den = jnp.sum(jnp.sum(wsel, axis=1, keepdims=True), axis=0, keepdims=True)
    return (wsel / den * ROUTED_SCALE).reshape(n_experts, tm)


def _outproj_kernel(ya_ref, hf_ref, hb_ref, bg_ref, yc_ref, yd_ref, x_ref, mod_ref,
                    wout_ref, gpost_ref, gffn_ref, rw_ref, rb_ref,
                    xo_ref, v_ref, gate_ref, cnt_ref, *, n_experts):
    gw = ya_ref.shape[-1]
    m = mod_ref[0, 0]
    yb = ((hf_ref[0] + hb_ref[0]) * jax.nn.gelu(bg_ref[0])).astype(BF16)
    y = (_dot(ya_ref[0], wout_ref[0:gw, :]) + _dot(yb, wout_ref[gw:2 * gw, :])
         + _dot(yc_ref[0], wout_ref[2 * gw:3 * gw, :]) + _dot(yd_ref[0], wout_ref[3 * gw:4 * gw, :]))
    x1 = x_ref[0] + m[2:3] * _rms(y, gpost_ref[...])
    xo_ref[0] = x1
    v = (_rms(x1, gffn_ref[...]) * (1.0 + m[4:5]) + m[3:4]).astype(BF16)
    v_ref[0] = v
    logits_t = _dot_t(rw_ref[...], v)
    gates_t = _route(logits_t, rb_ref[...], n_experts)
    pad = jnp.zeros((LANES - n_experts, gates_t.shape[1]), F32)
    gates = jnp.concatenate([gates_t, pad], axis=0).T
    gate_ref[0] = gates
    cnt_ref[0, 0] = jnp.sum(jnp.where(gates > 0.0, 1.0, 0.0), axis=0, keepdims=True).astype(jnp.int32)


def _outproj(ya, hf, hb, bg, yc, yd, xs, modtab, w_out, gpost, gffn, rw_t, rbias):
    bsz, L, d = xs.shape
    n_experts = rw_t.shape[0]
    gw = ya.shape[-1]
    blk = lambda w: pl.BlockSpec((1, ROW_TILE, w), lambda b, t: (b, t, 0))
    return pl.pallas_call(
        functools.partial(_outproj_kernel, n_experts=n_experts),
        grid=(bsz, L // ROW_TILE),
        in_specs=[blk(gw)] * 6 + [blk(d),
                  pl.BlockSpec((1, 1, 6, d), lambda b, t: (b, jnp.minimum(t, 1), 0, 0)),
                  _full(w_out), _full(gpost), _full(gffn), _full(rw_t), _full(rbias)],
        out_specs=[blk(d), blk(d), blk(LANES),
                   pl.BlockSpec((1, 1, 1, LANES), lambda b, t: (b, t, 0, 0))],
        out_shape=[jax.ShapeDtypeStruct((bsz, L, d), F32), jax.ShapeDtypeStruct((bsz, L, d), BF16),
                   jax.ShapeDtypeStruct((bsz, L, LANES), F32),
                   jax.ShapeDtypeStruct((bsz, L // ROW_TILE, 1, LANES), jnp.int32)],
        compiler_params=_params("arbitrary", "arbitrary"),
    )(ya, hf, hb, bg, yc, yd, xs, modtab, w_out, gpost, gffn, rw_t, rbias)


SLOT_ALIGN = 16
EXPERT_BLOCK = 1024
SLOT_BITS = (ROW_TILE // SLOT_ALIGN).bit_length()
TAIL_BITS = (EXPERT_BLOCK // SLOT_ALIGN - 1).bit_length()
SLOT_CHUNK = 512
CODE_BASE = 64.0


def _swiglu(v, wg, wu, wd):
    hg = _dot(v, wg.astype(BF16))
    hu = _dot(v, wu.astype(BF16))
    return hg * jax.nn.sigmoid(hg) * hu, wd.astype(BF16)


def _slot_cap(n_experts):
    rows = ROW_TILE * TOP_K + n_experts * (SLOT_ALIGN - 1)
    return -(-rows // SLOT_CHUNK) * SLOT_CHUNK


def _slot_codes(gates, off_row, ltri):
    chosen = gates > 0.0
    rank = _dot(ltri, jnp.where(chosen, 1.0, 0.0).astype(BF16))
    code = jnp.where(chosen, off_row.astype(F32) + rank + 1.0, 0.0)
    hi = jnp.floor(code * (1.0 / CODE_BASE))
    return hi.astype(BF16), (code - CODE_BASE * hi).astype(BF16)


def _slot_expert_onehot(off_row, len_row, first, rows):
    r = first + lax.broadcasted_iota(jnp.int32, (rows, LANES), 0)
    inside = jnp.where(r >= off_row, jnp.where(r < off_row + len_row, 1.0, 0.0), 0.0)
    return inside.astype(BF16)


def _row_digits(first, shape, axis):
    code = (first + 1 + lax.broadcasted_iota(jnp.int32, shape, axis)).astype(F32)
    hi = jnp.floor(code * (1.0 / CODE_BASE))
    return hi, code - CODE_BASE * hi


def _slot_dma(src, dst, src_off, dst_off, n_units, sem, start, n_bits=SLOT_BITS):
    for bit in range(n_bits):
        size = SLOT_ALIGN << bit
        done = (n_units & ((1 << bit) - 1)) * SLOT_ALIGN

        @pl.when(((n_units >> bit) & 1) == 1)
        def _piece():
            cp = pltpu.make_async_copy(
                src.at[pl.ds(pl.multiple_of(src_off + done, SLOT_ALIGN), size)],
                dst.at[pl.ds(pl.multiple_of(dst_off + done, SLOT_ALIGN), size)], sem)
            if start:
                cp.start()
            else:
                cp.wait()


def _wait_rows(src, dst, n_units, sem):
    for bit in range((min(src.shape[0], dst.shape[0]) // SLOT_ALIGN).bit_length()):
        size = SLOT_ALIGN << bit

        @pl.when(((n_units >> bit) & 1) == 1)
        def _piece():
            pltpu.make_async_copy(src.at[pl.ds(0, size)], dst.at[pl.ds(0, size)], sem).wait()


def _dispatch_kernel(pos_ref, off_ref, len_ref, tot_ref, tpos_ref, tlen_ref,
                     v_ref, g_ref, offv_ref, lenv_ref, ltri_ref, xs_ref,
                     buf_s, zero_s, sem, *, n_experts):
    i = pl.program_id(0)
    nt = pl.num_programs(0)
    cur = i % 2
    cap = buf_s.shape[1]

    @pl.when(i == 0)
    def _zero():
        zero_s[...] = jnp.zeros_like(zero_s)

    g = g_ref[...]
    off_row, len_row = offv_ref[0], lenv_ref[0]
    hi, lo = _slot_codes(g, off_row, ltri_ref[...])
    g_hi = g.astype(BF16)
    g_lo = (g - g_hi.astype(F32)).astype(BF16)
    src = jnp.concatenate([v_ref[...], g_hi, g_lo], axis=1)
    for c in range(cap // SLOT_CHUNK):
        first = c * SLOT_CHUNK
        oh = _slot_expert_onehot(off_row, len_row, first, SLOT_CHUNK)
        rh, rl = _row_digits(first, (SLOT_CHUNK, 1), 0)
        pick = jnp.where(_dot_t(oh, hi) == rh, jnp.where(_dot_t(oh, lo) == rl, 1.0, 0.0), 0.0)
        buf_s[cur, first:first + SLOT_CHUNK, :] = _dot(pick.astype(BF16), src).astype(BF16)

    def slots(tile, which, start):
        def body(e, carry):
            k = tile * n_experts + e
            _slot_dma(buf_s.at[which], xs_ref, off_ref[k], pos_ref[k], len_ref[k], sem.at[0], start)
            return carry
        lax.fori_loop(0, n_experts, body, 0)

    def tails(start):
        def body(e, carry):
            _slot_dma(zero_s, xs_ref, 0, tpos_ref[e], tlen_ref[e], sem.at[0], start, TAIL_BITS)
            return carry
        lax.fori_loop(0, n_experts, body, 0)

    @pl.when(i > 0)
    def _drain_previous():
        _wait_rows(buf_s.at[1 - cur], xs_ref, tot_ref[i - 1], sem.at[0])

    slots(i, cur, True)

    @pl.when(i == nt - 1)
    def _last():
        tails(True)
        _wait_rows(buf_s.at[cur], xs_ref, tot_ref[i], sem.at[0])
        tails(False)


def _dispatch(v2, gates2, plan, ltri, *, n_experts, n_rows):
    t, d = v2.shape
    nt = t // ROW_TILE
    cap = _slot_cap(n_experts)
    width = d + 2 * LANES
    grid_spec = pltpu.PrefetchScalarGridSpec(
        num_scalar_prefetch=6, grid=(nt,),
        in_specs=[pl.BlockSpec((ROW_TILE, d), lambda i, *_: (i, 0)),
                  pl.BlockSpec((ROW_TILE, LANES), lambda i, *_: (i, 0)),
                  pl.BlockSpec((1, 1, LANES), lambda i, *_: (i, 0, 0)),
                  pl.BlockSpec((1, 1, LANES), lambda i, *_: (i, 0, 0)),
                  pl.BlockSpec(ltri.shape, lambda i, *_: (0, 0))],
        out_specs=pl.BlockSpec(memory_space=pl.ANY),
        scratch_shapes=[pltpu.VMEM((2, cap, width), BF16),
                        pltpu.VMEM((EXPERT_BLOCK, width), BF16),
                        pltpu.SemaphoreType.DMA((1,))])
    return pl.pallas_call(
        functools.partial(_dispatch_kernel, n_experts=n_experts),
        grid_spec=grid_spec,
        out_shape=jax.ShapeDtypeStruct((n_rows, width), BF16),
        compiler_params=_params("arbitrary"),
    )(plan["pos"], plan["off"], plan["len"], plan["tile_len"], plan["tail_pos"], plan["tail_len"],
      v2, gates2, plan["off_v"], plan["len_v"], ltri)


def _expert_kernel(be_ref, na_ref, x_ref, wg_ref, wu_ref, wd_ref, y_ref, wg_s, wu_s, wd_s):
    b = pl.program_id(0)
    e = be_ref[b]

    @pl.when(jnp.logical_or(b == 0, e != be_ref[jnp.maximum(b - 1, 0)]))
    def _new_expert():
        wg_s[...] = wg_ref[0].astype(BF16)
        wu_s[...] = wu_ref[0].astype(BF16)
        wd_s[...] = wd_ref[0].astype(BF16)

    @pl.when(b < na_ref[0])
    def _active():
        d = wg_ref.shape[1]
        x = x_ref[:, 0:d]
        gates = x_ref[:, d:d + LANES].astype(F32) + x_ref[:, d + LANES:d + 2 * LANES].astype(F32)
        lane = lax.broadcasted_iota(jnp.int32, gates.shape, 1)
        gcol = jnp.sum(jnp.where(lane == e, gates, 0.0), axis=1, keepdims=True)
        hg = _dot(x, wg_s[...])
        h = hg * jax.nn.sigmoid(hg) * _dot(x, wu_s[...])
        y_ref[...] = (_dot(h.astype(BF16), wd_s[...]) * gcol).astype(BF16)


def _experts(xs_sorted, plan, wg, wu, wd):
    n_rows, width = xs_sorted.shape
    _, d, de = wg.shape
    nb = n_rows // EXPERT_BLOCK
    rowmap = lambda b, be, na: (jnp.minimum(b, na[0] - 1), 0)
    grid_spec = pltpu.PrefetchScalarGridSpec(
        num_scalar_prefetch=2, grid=(nb,),
        in_specs=[pl.BlockSpec((EXPERT_BLOCK, width), rowmap),
                  pl.BlockSpec((1, d, de), lambda b, be, na: (be[b], 0, 0)),
                  pl.BlockSpec((1, d, de), lambda b, be, na: (be[b], 0, 0)),
                  pl.BlockSpec((1, de, d), lambda b, be, na: (be[b], 0, 0))],
        out_specs=pl.BlockSpec((EXPERT_BLOCK, d), rowmap),
        scratch_shapes=[pltpu.VMEM((d, de), BF16), pltpu.VMEM((d, de), BF16), pltpu.VMEM((de, d), BF16)])
    return pl.pallas_call(
        _expert_kernel, grid_spec=grid_spec,
        out_shape=jax.ShapeDtypeStruct((n_rows, d), BF16),
        compiler_params=_params("arbitrary"),
    )(plan["block_expert"], plan["n_active"], xs_sorted, wg, wu, wd)


def _combine_kernel(pos_ref, off_ref, len_ref, tot_ref,
                    ys_ref, v_ref, g_ref, offv_ref, lenv_ref, ltri_ref, x_ref, mod_ref,
                    sg_ref, su_ref, sd_ref, gpost_ref, o_ref, buf_s, sem, *, n_experts):
    i = pl.program_id(0)
    nt = pl.num_programs(0)
    cur = i % 2
    cap = buf_s.shape[1]

    def slots(tile, which, start):
        def body(e, carry):
            k = tile * n_experts + e
            _slot_dma(ys_ref, buf_s.at[which], pos_ref[k], off_ref[k], len_ref[k], sem.at[which], start)
            return carry
        lax.fori_loop(0, n_experts, body, 0)

    @pl.when(i == 0)
    def _first():
        buf_s[...] = jnp.zeros_like(buf_s)
        slots(0, 0, True)

    @pl.when(i + 1 < nt)
    def _prefetch():
        slots(i + 1, 1 - cur, True)

    v = v_ref[...]
    h, wd = _swiglu(v, sg_ref[...], su_ref[...], sd_ref[...])
    acc = _dot(h.astype(BF16), wd)
    off_row, len_row = offv_ref[0], lenv_ref[0]
    hi, lo = _slot_codes(g_ref[...], off_row, ltri_ref[...])

    _wait_rows(ys_ref, buf_s.at[cur], tot_ref[i], sem.at[cur])
    for c in range(cap // SLOT_CHUNK):
        first = c * SLOT_CHUNK
        oh = _slot_expert_onehot(off_row, len_row, first, SLOT_CHUNK)
        rh, rl = _row_digits(first, (1, SLOT_CHUNK), 1)
        pick = jnp.where(_dot_t(hi, oh) == rh, jnp.where(_dot_t(lo, oh) == rl, 1.0, 0.0), 0.0)
        acc = acc + _dot(pick.astype(BF16), buf_s[cur, first:first + SLOT_CHUNK, :])
    o_ref[0] = x_ref[0] + mod_ref[0, 0][5:6] * _rms(acc, gpost_ref[...])


def _combine(ys_sorted, v2, gates2, plan, ltri, xs, modtab, sg, su, sd, gpost, *, n_experts):
    bsz, L, d = xs.shape
    tpb = L // ROW_TILE
    nt = bsz * tpb
    cap = _slot_cap(n_experts)
    full = lambda a: pl.BlockSpec(a.shape, lambda i, *_: (0,) * a.ndim)
    grid_spec = pltpu.PrefetchScalarGridSpec(
        num_scalar_prefetch=4, grid=(nt,),
        in_specs=[pl.BlockSpec(memory_space=pl.ANY),
                  pl.BlockSpec((ROW_TILE, d), lambda i, *_: (i, 0)),
                  pl.BlockSpec((ROW_TILE, LANES), lambda i, *_: (i, 0)),
                  pl.BlockSpec((1, 1, LANES), lambda i, *_: (i, 0, 0)),
                  pl.BlockSpec((1, 1, LANES), lambda i, *_: (i, 0, 0)),
                  full(ltri),
                  pl.BlockSpec((1, ROW_TILE, d), lambda i, *_: (i // tpb, i % tpb, 0)),
                  pl.BlockSpec((1, 1, 6, d), lambda i, *_: (i // tpb, jnp.minimum(i % tpb, 1), 0, 0)),
                  full(sg), full(su), full(sd), full(gpost)],
        out_specs=pl.BlockSpec((1, ROW_TILE, d), lambda i, *_: (i // tpb, i % tpb, 0)),
        scratch_shapes=[pltpu.VMEM((2, cap, d), BF16), pltpu.SemaphoreType.DMA((2,))])
    return pl.pallas_call(
        functools.partial(_combine_kernel, n_experts=n_experts),
        grid_spec=grid_spec,
        out_shape=jax.ShapeDtypeStruct((bsz, L, d), F32),
        compiler_params=_params("arbitrary"),
    )(plan["pos"], plan["off"], plan["len"], plan["tile_len"], ys_sorted, v2, gates2, plan["off_v"], plan["len_v"], ltri,
      xs, modtab, sg, su, sd, gpost)


def _moe_plan(counts, n_experts, n_blocks):
    a = (counts + (SLOT_ALIGN - 1)) // SLOT_ALIGN * SLOT_ALIGN
    rows = jnp.sum(a, axis=0)
    region = (rows + (EXPERT_BLOCK - 1)) // EXPERT_BLOCK * EXPERT_BLOCK
    region_end = jnp.cumsum(region)
    region_start = region_end - region
    pos = region_start[None, :] + jnp.cumsum(a, axis=0) - a
    off = jnp.cumsum(a, axis=1) - a
    first_row = jnp.arange(n_blocks, dtype=jnp.int32) * EXPERT_BLOCK
    block_expert = jnp.minimum(jnp.searchsorted(region_end, first_row, side="right"), n_experts - 1)
    flat = lambda t: t[:, :n_experts].reshape(-1).astype(jnp.int32)
    nt = counts.shape[0]
    return {
        "pos": flat(pos), "off": flat(off), "len": flat(a // SLOT_ALIGN),
        "tile_len": (jnp.sum(a, axis=1) // SLOT_ALIGN).astype(jnp.int32),
        "tail_pos": (region_start + rows)[:n_experts].astype(jnp.int32),
        "tail_len": ((region - rows) // SLOT_ALIGN)[:n_experts].astype(jnp.int32),
        "off_v": off.reshape(nt, 1, LANES).astype(jnp.int32),
        "len_v": a.reshape(nt, 1, LANES).astype(jnp.int32),
        "block_expert": block_expert.astype(jnp.int32),
        "n_active": (region_end[-1:] // EXPERT_BLOCK).astype(jnp.int32),
    }


def _moe(v, gates, counts, xs, modtab, wg, wu, wd, sg, su, sd, gpost):
    bsz, L, d = xs.shape
    n_experts = wg.shape[0]
    t = bsz * L
    nt = t // ROW_TILE
    worst = t * TOP_K + nt * n_experts * (SLOT_ALIGN - 1) + n_experts * (EXPERT_BLOCK - 1)
    n_blocks = -(-worst // EXPERT_BLOCK)
    plan = _moe_plan(counts.reshape(nt, LANES), n_experts, n_blocks)
    ltri = jnp.tril(jnp.ones((ROW_TILE, ROW_TILE), F32), -1).astype(BF16)
    v2, gates2 = v.reshape(t, d), gates.reshape(t, LANES)
    xs_sorted = _dispatch(v2, gates2, plan, ltri, n_experts=n_experts, n_rows=n_blocks * EXPERT_BLOCK)
    ys_sorted = _experts(xs_sorted, plan, wg, wu, wd)
    return _combine(ys_sorted, v2, gates2, plan, ltri, xs, modtab, sg, su, sd, gpost, n_experts=n_experts)


def _tables(seq, ctx_len):
    def build(rot_dim, lead, slot):
        n = rot_dim // 4
        pos = jnp.arange(seq)
        row = (pos // GRID_W).astype(F32)
        col = (pos % GRID_W).astype(F32)
        inv = ROPE_THETA ** (-jnp.arange(n, dtype=F32) / n)
        ang = jnp.concatenate([row[:, None] * inv, col[:, None] * inv], axis=-1)
        cos, sin = jnp.cos(ang), jnp.sin(ang)
        zero = jnp.zeros_like(sin)
        tail = slot - lead - rot_dim
        one_l, zero_l = jnp.ones((seq, lead), F32), jnp.zeros((seq, lead), F32)
        one_t, zero_t = jnp.ones((seq, tail), F32), jnp.zeros((seq, tail), F32)
        c = jnp.concatenate([one_l, cos, cos, one_t], axis=1)
        s_lo = jnp.concatenate([zero_l, -sin, zero, zero_t], axis=1)
        s_hi = jnp.concatenate([zero_l, zero, sin, zero_t], axis=1)
        ctx_c = jnp.ones((ctx_len, slot), F32)
        ctx_s = jnp.zeros((ctx_len, slot), F32)
        return [jnp.concatenate([ctx_c, c], axis=0), jnp.concatenate([ctx_s, s_lo], axis=0),
                jnp.concatenate([ctx_s, s_hi], axis=0)]

    hd = [jnp.concatenate([t, t], axis=1) for t in build(HEAD_DIM, 0, HEAD_DIM)]
    mla = build(QK_ROPE, QK_NOPE, LANES)
    return hd + mla


def _dup_heads(w, n_heads):
    d = w.shape[0]
    wh = w.reshape(d, n_heads, 1, HEAD_DIM)
    return jnp.broadcast_to(wh, (d, n_heads, 2, HEAD_DIM)).reshape(d, n_heads * 2 * HEAD_DIM)


def kernel(x, c, ctx, c_ctx, w_mod, b_mod, g_pre_mix, g_post_mix, g_pre_ffn, g_post_ffn, w_in, a_q_norm, a_k_norm, conv_w, conv_b, lru_wa, lru_ba, lru_wi, lru_bi, lru_lambda, c_sink, d_q_norm, d_w_uq, d_kv_norm, d_w_ukv, w_out, router_w, router_bias, w_gate, w_up, w_down, sh_gate, sh_up, sh_down):
    bsz, seq, d = x.shape
    ctx_len = ctx.shape[1]
    depth = w_mod.shape[0]
    gw = d // 4
    lru_w = conv_w.shape[-1]
    q_lora = d_q_norm.shape[-1]
    kv_lora = d_kv_norm.shape[-1]
    n_experts = router_w.shape[-1]
    a_heads = gw // HEAD_DIM
    a_kv = (w_in.shape[-1] - (2 * gw + 2 * lru_w + q_lora + kv_lora + QK_ROPE)) // (4 * HEAD_DIM)
    mla_scale = (QK_NOPE + QK_ROPE) ** -0.5
    v_dim = gw // D_HEADS
    assert ctx_len == ROW_TILE and a_kv == 2 and a_heads == 4 and n_experts <= LANES

    cond = jnp.zeros((16, d), F32).at[:bsz].set(c).at[bsz].set(c_ctx)
    mod = _modulation(cond, w_mod, b_mod).reshape(depth, 16, 6, d)
    tabs = _tables(seq, ctx_len)
    seg = jnp.kron(jnp.eye(a_heads, dtype=F32), jnp.ones((HEAD_DIM, HEAD_DIM), F32)).astype(BF16)

    xs = jnp.concatenate([ctx, x], axis=1)
    for l in range(depth):
        modtab = jnp.stack([jnp.broadcast_to(mod[l, bsz], (bsz, 6, d)), mod[l, :bsz]], axis=1)

        offs = [0]
        for wdt in (gw, a_kv * HEAD_DIM, a_kv * HEAD_DIM, lru_w, lru_w, gw, a_kv * HEAD_DIM, a_kv * HEAD_DIM,
                    q_lora, kv_lora, QK_ROPE):
            offs.append(offs[-1] + wdt)
        col = lambda i: w_in[l][:, offs[i]:offs[i + 1]]
        zeros = lambda n: jnp.zeros((d, n), F32)
        kr_slot = jnp.concatenate([zeros(QK_NOPE), col(10), zeros(LANES - QK_NOPE - QK_ROPE)], axis=1)
        w_in_p = jnp.concatenate(
            [col(0), _dup_heads(col(1), a_kv), _dup_heads(col(2), a_kv), col(3), col(4),
             col(5), _dup_heads(col(6), a_kv), _dup_heads(col(7), a_kv),
             col(8), zeros(2 * LANES - q_lora), col(9)] + [kr_slot] * D_HEADS, axis=1).astype(BF16)

        qk = QK_NOPE + QK_ROPE
        wuq = d_w_uq[l].reshape(q_lora, D_HEADS, qk)
        wuq = jnp.pad(wuq, ((0, 2 * LANES - q_lora), (0, 0), (0, LANES - qk))).reshape(2 * LANES, D_HEADS * LANES)
        wukv = d_w_ukv[l].reshape(kv_lora, D_HEADS, QK_NOPE + v_dim)
        wk = jnp.pad(wukv[:, :, :QK_NOPE], ((0, 0), (0, 0), (0, LANES - QK_NOPE))).reshape(kv_lora, D_HEADS * LANES)
        wv = wukv[:, :, QK_NOPE:].reshape(kv_lora, D_HEADS * v_dim)
        wukv_p = jnp.concatenate([wk, wv], axis=1).astype(BF16)
        dqn = jnp.pad(d_q_norm[l], (0, 2 * LANES - q_lora)).reshape(1, 2 * LANES)

        qa, ka, va, bx, bg, qc, kc, vc, qd, kd, vd = _inproj(
            xs, modtab, g_pre_mix[l].reshape(1, d), w_in_p, seg, tabs,
            jnp.tile(a_q_norm[l], a_heads).reshape(1, gw), jnp.tile(a_k_norm[l], 2 * a_kv).reshape(1, gw),
            dqn, wuq.astype(BF16), d_kv_norm[l].reshape(1, kv_lora), wukv_p,
            q_lora=q_lora, mla_scale=mla_scale)

        ya = _attention(qa, ka, va, split_q=True, ctx_len=ctx_len)
        yd = _attention(qd, kd, vd, split_q=False, ctx_len=ctx_len)
        yc = _window_attention(qc, kc, vc, c_sink[l], ctx_len=ctx_len)

        blocks = lru_wa.shape[2]
        bdiag = lambda wts: jnp.stack([jax.scipy.linalg.block_diag(*[wts[dd, h] for h in range(blocks)])
                                       for dd in range(2)]).astype(BF16)
        hf, hb = _lru(bx, conv_w[l], conv_b[l].reshape(1, lru_w), bdiag(lru_wa[l]), lru_ba[l].reshape(2, 1, lru_w),
                      bdiag(lru_wi[l]), lru_bi[l].reshape(2, 1, lru_w), lru_lambda[l].reshape(2, 1, lru_w),
                      ctx_len=ctx_len)

        rw_t = router_w[l].T.astype(BF16)
        xs_mid, v, gates, counts = _outproj(ya, hf, hb, bg, yc, yd, xs, modtab, w_out[l].astype(BF16),
                                            g_post_mix[l].reshape(1, d), g_pre_ffn[l].reshape(1, d),
                                            rw_t, router_bias[l].reshape(n_experts, 1))
        xs = _moe(v, gates, counts, xs_mid, modtab, w_gate[l], w_up[l], w_down[l],
                  sh_gate[l], sh_up[l], sh_down[l], g_post_ffn[l].reshape(1, d))
    return xs[:, ctx_len:, :]
```

```python
import functools

import jax
import jax.numpy as jnp
from jax import lax
from jax.experimental import pallas as pl
from jax.experimental.pallas import tpu as pltpu

F32 = jnp.float32
BF16 = jnp.bfloat16

GRID_W = 64
HEAD_DIM = 64
ROPE_THETA = 10000.0
NORM_EPS = 1e-6
WINDOW = 128
Q_BLOCK = 128
D_HEADS = 4
QK_NOPE = 64
QK_ROPE = 32
LRU_C = 8.0
CONV_W = 4
TOP_K = 8
N_EXPERT_GROUPS = 8
TOPK_GROUPS = 4
ROUTED_SCALE = 2.5

LANES = 128
VMEM_LIMIT = 56 * 1024 * 1024
ROW_TILE = 256

NEG_INF = float("-inf")


def _params(*sem):
    return pltpu.CompilerParams(dimension_semantics=sem, vmem_limit_bytes=VMEM_LIMIT)


def _rms(x, gain):
    return x * lax.rsqrt(jnp.mean(x * x, axis=-1, keepdims=True) + NORM_EPS) * gain


def _dot(a, b):
    return jnp.dot(a, b, preferred_element_type=F32)


def _dot_t(a, b):
    return lax.dot_general(a, b, (((1,), (1,)), ((), ())), preferred_element_type=F32)


def _full(a):
    return pl.BlockSpec(a.shape, lambda *_: (0,) * a.ndim)


def _mod_kernel(c_ref, w_ref, b_ref, o_ref):
    a = c_ref[...]
    a = a * jax.nn.sigmoid(a)
    o_ref[0] = _dot(a.astype(BF16), w_ref[0].astype(BF16)) + b_ref[0]


def _modulation(cond, w_mod, b_mod):
    depth, d, n = w_mod.shape
    tn = n // 4
    rows = cond.shape[0]
    return pl.pallas_call(
        _mod_kernel,
        grid=(depth, n // tn),
        in_specs=[pl.BlockSpec((rows, d), lambda l, j: (0, 0)),
                  pl.BlockSpec((1, d, tn), lambda l, j: (l, 0, j)),
                  pl.BlockSpec((1, 1, tn), lambda l, j: (l, 0, j))],
        out_specs=pl.BlockSpec((1, rows, tn), lambda l, j: (l, 0, j)),
        out_shape=jax.ShapeDtypeStruct((depth, rows, n), F32),
        compiler_params=_params("arbitrary", "arbitrary"),
    )(cond, w_mod, b_mod.reshape(depth, 1, n))


def _rope(t, cos, sin_lo, sin_hi, half):
    w = t.shape[-1]
    return t * cos + pltpu.roll(t, w - half, 1) * sin_lo + pltpu.roll(t, half, 1) * sin_hi


def _inproj_kernel(x_ref, mod_ref, gpre_ref, w_ref, seg_ref,
                   cos_ref, sinl_ref, sinh_ref, cosd_ref, sindl_ref, sindh_ref,
                   aqn_ref, akn_ref, dqn_ref, wuq_ref, dkvn_ref, wukv_ref,
                   qa_ref, ka_ref, va_ref, bx_ref, bg_ref, qc_ref, kc_ref, vc_ref,
                   qd_ref, kd_ref, vd_ref, *, q_lora, mla_scale):
    x = x_ref[0]
    m = mod_ref[0, 0]
    u = (_rms(x, gpre_ref[...]) * (1.0 + m[1:2]) + m[0:1]).astype(BF16)
    proj = _dot(u, w_ref[...])

    seg = seg_ref[...]
    cos = jnp.concatenate([cos_ref[...]] * 2, axis=1)
    sin_lo = jnp.concatenate([sinl_ref[...]] * 2, axis=1)
    sin_hi = jnp.concatenate([sinh_ref[...]] * 2, axis=1)
    cosd = jnp.concatenate([cosd_ref[...]] * 4, axis=1)
    sind_lo = jnp.concatenate([sindl_ref[...]] * 4, axis=1)
    sind_hi = jnp.concatenate([sindh_ref[...]] * 4, axis=1)

    def head_rms(t, gain):
        sq = t * t
        hi = sq.astype(BF16)
        lo = (sq - hi.astype(F32)).astype(BF16)
        ms = (_dot(hi, seg) + _dot(lo, seg)) * (1.0 / HEAD_DIM)
        return t * lax.rsqrt(ms + NORM_EPS) * gain

    def rope_hd(t):
        return _rope(t, cos, sin_lo, sin_hi, HEAD_DIM // 2)

    def rope_r(t):
        return _rope(t, cosd, sind_lo, sind_hi, QK_ROPE // 2)

    sc_hd = HEAD_DIM ** -0.5
    qa_ref[0] = (rope_hd(head_rms(proj[:, 0:256], aqn_ref[...])) * sc_hd).astype(BF16)
    ka_ref[0] = rope_hd(head_rms(proj[:, 256:512], akn_ref[...])).astype(BF16)
    va_ref[0] = proj[:, 512:768].astype(BF16)
    bx_ref[0] = proj[:, 768:1024]
    bg_ref[0] = proj[:, 1024:1280]
    qc_ref[0] = (rope_hd(proj[:, 1280:1536]) * sc_hd).astype(BF16)
    kc_ref[0] = rope_hd(proj[:, 1536:1792]).astype(BF16)
    vc_ref[0] = proj[:, 1792:2048].astype(BF16)
    cq = proj[:, 2048:2304]
    cq = cq * lax.rsqrt(jnp.sum(cq * cq, axis=-1, keepdims=True) * (1.0 / q_lora) + NORM_EPS) * dqn_ref[...]
    qd = _dot(cq.astype(BF16), wuq_ref[...])
    qd_ref[0] = (rope_r(qd) * mla_scale).astype(BF16)
    ckv = _rms(proj[:, 2304:2432], dkvn_ref[...])
    kv = _dot(ckv.astype(BF16), wukv_ref[...])
    kd_ref[0] = (kv[:, 0:512] + rope_r(proj[:, 2432:2944])).astype(BF16)
    vd_ref[0] = kv[:, 512:768].astype(BF16)


def _inproj(xs, modtab, gpre, w_in_p, seg, tabs, aqn, akn, dqn, wuq, dkvn, wukv, *, q_lora, mla_scale):
    bsz, L, d = xs.shape
    nt = L // ROW_TILE
    rowblk = lambda w: pl.BlockSpec((1, ROW_TILE, w), lambda b, t: (b, t, 0))
    tab = lambda a: pl.BlockSpec((ROW_TILE, a.shape[1]), lambda b, t: (t, 0))
    out_w = [(256, BF16)] * 3 + [(256, F32)] * 2 + [(256, BF16)] * 3 + [(512, BF16), (512, BF16), (256, BF16)]
    return pl.pallas_call(
        functools.partial(_inproj_kernel, q_lora=q_lora, mla_scale=mla_scale),
        grid=(bsz, nt),
        in_specs=[rowblk(d),
                  pl.BlockSpec((1, 1, 6, d), lambda b, t: (b, jnp.minimum(t, 1), 0, 0)),
                  _full(gpre), _full(w_in_p), _full(seg)] + [tab(a) for a in tabs]
                 + [_full(a) for a in (aqn, akn, dqn, wuq, dkvn, wukv)],
        out_specs=[rowblk(w) for w, _ in out_w],
        out_shape=[jax.ShapeDtypeStruct((bsz, L, w), dt) for w, dt in out_w],
        compiler_params=_params("arbitrary", "arbitrary"),
    )(xs, modtab, gpre, w_in_p, seg, *tabs, aqn, akn, dqn, wuq, dkvn, wukv)


def _attn_kernel(qa_ref, qb_ref, ka_ref, kb_ref, v_ref, o_ref, *, split_q, ctx_len, k_len, first_tile):
    lane = lax.broadcasted_iota(jnp.int32, (1, LANES), 1)
    low = lane < HEAD_DIM

    def run(n_keys):
        outs = []
        for h, (q_ref, k_ref) in enumerate(((qa_ref, ka_ref), (qb_ref, kb_ref))):
            q = q_ref[0]
            if split_q:
                keep = low if h == 0 else jnp.logical_not(low)
                q = jnp.where(keep, q, jnp.zeros_like(q))
            s = _dot_t(q, k_ref[0, 0:n_keys, :])
            p = jnp.exp(s - jnp.max(s, axis=-1, keepdims=True))
            den = jnp.sum(p, axis=-1, keepdims=True)
            outs.append(_dot(p.astype(BF16), v_ref[0, 0:n_keys, :]) / den)
        o_ref[0] = jnp.where(low, outs[0], outs[1]).astype(o_ref.dtype)

    if first_tile > 0:
        run(k_len)
        return
    t = pl.program_id(2)

    @pl.when(t == 0)
    def _ctx():
        run(ctx_len)

    @pl.when(t > 0)
    def _lat():
        run(k_len)


def _attention(q, k, v, *, split_q, ctx_len, need_ctx):
    bsz, L, _ = q.shape
    tq = ROW_TILE
    assert ctx_len == tq
    t0 = 0 if need_ctx else 1
    if split_q:
        qmap = [lambda b, g, t: (b, t + t0, g)] * 2
        kmap = [lambda b, g, t: (b, 0, g)] * 2
    else:
        qmap = [lambda b, g, t: (b, t + t0, 2 * g), lambda b, g, t: (b, t + t0, 2 * g + 1)]
        kmap = [lambda b, g, t: (b, 0, 2 * g), lambda b, g, t: (b, 0, 2 * g + 1)]
    return pl.pallas_call(
        functools.partial(_attn_kernel, split_q=split_q, ctx_len=ctx_len, k_len=L, first_tile=t0),
        grid=(bsz, 2, L // tq - t0),
        in_specs=[pl.BlockSpec((1, tq, LANES), qmap[0]), pl.BlockSpec((1, tq, LANES), qmap[1]),
                  pl.BlockSpec((1, L, LANES), kmap[0]), pl.BlockSpec((1, L, LANES), kmap[1]),
                  pl.BlockSpec((1, L, LANES), lambda b, g, t: (b, 0, g))],
        out_specs=pl.BlockSpec((1, tq, LANES), lambda b, g, t: (b, t + t0, g)),
        out_shape=jax.ShapeDtypeStruct((bsz, L, 2 * LANES), BF16),
        compiler_params=_params("arbitrary", "arbitrary", "arbitrary"),
    )(q, q, k, k, v)


def _winattn_kernel(sink_ref, q_ref, k_ref, v_ref, o_ref, *, ctx_len, seq, first_block):
    t = pl.program_id(1) + first_block
    lane = lax.broadcasted_iota(jnp.int32, (1, LANES), 1)
    low = lane < HEAD_DIM
    ctx_blocks = ctx_len // Q_BLOCK
    win = 3 * Q_BLOCK
    n_groups = q_ref.shape[-1] // LANES

    def finish(parts, sink):
        m = sink
        for s, _ in parts:
            m = jnp.maximum(m, jnp.max(s, axis=-1, keepdims=True))
        den = jnp.exp(sink - m)
        acc = None
        for s, vv in parts:
            p = jnp.exp(s - m)
            den = den + jnp.sum(p, axis=-1, keepdims=True)
            o = _dot(p.astype(BF16), vv)
            acc = o if acc is None else acc + o
        return acc / den

    def heads(fn):
        for g in range(n_groups):
            lanes = slice(g * LANES, (g + 1) * LANES)
            outs = []
            for h in range(2):
                keep = low if h == 0 else jnp.logical_not(low)
                q = q_ref[0, :, lanes]
                q = jnp.where(keep, q, jnp.zeros_like(q))
                outs.append(fn(q, sink_ref[2 * g + h], lanes))
            o_ref[0, :, lanes] = jnp.where(low, outs[0], outs[1]).astype(o_ref.dtype)

    def ctx_queries():
        def one(q, sink, lanes):
            return finish([(_dot_t(q, k_ref[0, 0:ctx_len, lanes]), v_ref[0, 0:ctx_len, lanes])], sink)
        heads(one)

    def latent_queries():
        n = t - ctx_blocks
        start = jnp.clip((n - 1) * Q_BLOCK, 0, seq - win)
        off = pl.multiple_of(ctx_len + start, Q_BLOCK)
        qpos = n * Q_BLOCK + lax.broadcasted_iota(jnp.int32, (Q_BLOCK, win), 0)
        kpos = start + lax.broadcasted_iota(jnp.int32, (Q_BLOCK, win), 1)
        allowed = jnp.abs(kpos - qpos) <= WINDOW

        def one(q, sink, lanes):
            s_ctx = _dot_t(q, k_ref[0, 0:ctx_len, lanes])
            s_win = jnp.where(allowed, _dot_t(q, k_ref[0, pl.ds(off, win), lanes]), NEG_INF)
            return finish([(s_ctx, v_ref[0, 0:ctx_len, lanes]), (s_win, v_ref[0, pl.ds(off, win), lanes])], sink)
        heads(one)

    if first_block >= ctx_blocks:
        latent_queries()
    else:
        pl.when(t < ctx_blocks)(ctx_queries)
        pl.when(t >= ctx_blocks)(latent_queries)


def _window_attention(q, k, v, sink, *, ctx_len, need_ctx):
    bsz, L, w = q.shape
    b0 = 0 if need_ctx else ctx_len // Q_BLOCK
    return pl.pallas_call(
        functools.partial(_winattn_kernel, ctx_len=ctx_len, seq=L - ctx_len, first_block=b0),
        grid=(bsz, L // Q_BLOCK - b0),
        in_specs=[pl.BlockSpec(memory_space=pltpu.SMEM),
                  pl.BlockSpec((1, Q_BLOCK, w), lambda b, t: (b, t + b0, 0)),
                  pl.BlockSpec((1, L, w), lambda b, t: (b, 0, 0)),
                  pl.BlockSpec((1, L, w), lambda b, t: (b, 0, 0))],
        out_specs=pl.BlockSpec((1, Q_BLOCK, w), lambda b, t: (b, t + b0, 0)),
        out_shape=jax.ShapeDtypeStruct((bsz, L, w), BF16),
        compiler_params=_params("arbitrary", "arbitrary"),
    )(sink, q, k, v)


def _lru_kernel(xf_ref, xfp_ref, xfn_ref, xb_ref, xbp_ref, xbn_ref,
                cw_ref, cb_ref, wa_ref, ba_ref, wi_ref, bi_ref, lam_ref,
                hf_ref, hb_ref,
                af_s, bf_s, ab_s, bb_s, of_s, ob_s, sf_s, sb_s, *, n_chunks):
    i = pl.program_id(0)
    bsz, tc, w = xf_ref.shape
    rows = bsz * tc
    ti = lax.broadcasted_iota(jnp.int32, (bsz, tc, w), 1)

    def coeffs(x_ref, prev_ref, next_ref, chunk, d, a_s, b_s):
        has_prev = (chunk >= 2).astype(F32)
        has_next = jnp.logical_and(chunk >= 1, chunk <= n_chunks - 2).astype(F32)
        x = x_ref[...]
        p1 = prev_ref[:, 7:8, :] * has_prev
        n0 = next_ref[:, 0:1, :] * has_next
        n1 = next_ref[:, 1:2, :] * has_next
        x2 = x.reshape(rows, w)
        xm1 = jnp.where(ti == 0, p1, pltpu.roll(x2, 1, 0).reshape(bsz, tc, w))
        xp1 = jnp.where(ti == tc - 1, n0, pltpu.roll(x2, rows - 1, 0).reshape(bsz, tc, w))
        xp2 = jnp.where(ti == tc - 1, n1,
                        jnp.where(ti == tc - 2, n0, pltpu.roll(x2, rows - 2, 0).reshape(bsz, tc, w)))
        cw = cw_ref[...]
        xc = (cb_ref[...] + xm1 * cw[0:1] + x * cw[1:2] + xp1 * cw[2:3] + xp2 * cw[3:4]).reshape(rows, w)
        xcb = xc.astype(BF16)
        r = jax.nn.sigmoid(_dot(xcb, wa_ref[d]) + ba_ref[d])
        ig = jax.nn.sigmoid(_dot(xcb, wi_ref[d]) + bi_ref[d])
        log_a = (-LRU_C) * r * jax.nn.softplus(-lam_ref[d])
        a = jnp.exp(log_a)
        b = jnp.sqrt(1.0 - a * a) * (ig * xc)
        for j in range(w // LANES):
            a_s[j] = a[:, j * LANES:(j + 1) * LANES]
            b_s[j] = b[:, j * LANES:(j + 1) * LANES]

    chunk_b = jnp.where(i == 0, 0, n_chunks - i)
    coeffs(xf_ref, xfp_ref, xfn_ref, i, 0, af_s, bf_s)
    coeffs(xb_ref, xbp_ref, xbn_ref, chunk_b, 1, ab_s, bb_s)

    @pl.when(i == 0)
    def _init():
        sf_s[...] = jnp.zeros_like(sf_s)
        sb_s[...] = jnp.zeros_like(sb_s)

    nl = w // LANES

    def step(t, carry):
        fwd = pl.ds(t, bsz, stride=tc)
        bwd = pl.ds(tc - 1 - t, bsz, stride=tc)
        out = []
        for j in range(nl):
            hf = af_s[j, fwd, :] * carry[j] + bf_s[j, fwd, :]
            hb = ab_s[j, bwd, :] * carry[nl + j] + bb_s[j, bwd, :]
            of_s[j, fwd, :] = hf
            ob_s[j, bwd, :] = hb
            out.append((hf, hb))
        return tuple(o[0] for o in out) + tuple(o[1] for o in out)

    init = tuple(sf_s[j] for j in range(nl)) + tuple(sb_s[j] for j in range(nl))
    fin = lax.fori_loop(0, tc, step, init, unroll=8)
    for j in range(nl):
        sf_s[j] = fin[j]
        sb_s[j] = fin[nl + j]
    hf_ref[...] = jnp.concatenate([of_s[j] for j in range(nl)], axis=1).reshape(bsz, tc, w)
    hb_ref[...] = jnp.concatenate([ob_s[j] for j in range(nl)], axis=1).reshape(bsz, tc, w)


def _lru(bx, conv_w, conv_b, wa, ba, wi, bi, lam, *, ctx_len):
    bsz, L, w = bx.shape
    tc = ctx_len
    nc = L // tc
    hb_blocks = tc // 8
    last8 = L // 8 - 1

    def fchunk(i):
        return i

    def bchunk(i):
        return jnp.where(i == 0, 0, nc - i)

    def cur(cf):
        return pl.BlockSpec((bsz, tc, w), lambda i: (0, cf(i), 0))

    def prev(cf):
        return pl.BlockSpec((bsz, 8, w), lambda i: (0, jnp.maximum(cf(i) * hb_blocks - 1, 0), 0))

    def nxt(cf):
        return pl.BlockSpec((bsz, 8, w), lambda i: (0, jnp.minimum((cf(i) + 1) * hb_blocks, last8), 0))

    small = [conv_w, conv_b, wa, ba, wi, bi, lam]
    nl = w // LANES
    scr = [pltpu.VMEM((nl, bsz * tc, LANES), F32)] * 6 + [pltpu.VMEM((nl, bsz, LANES), F32)] * 2
    return pl.pallas_call(
        functools.partial(_lru_kernel, n_chunks=nc),
        grid=(nc,),
        in_specs=[cur(fchunk), prev(fchunk), nxt(fchunk), cur(bchunk), prev(bchunk), nxt(bchunk)]
                 + [_full(a) for a in small],
        out_specs=[cur(fchunk), cur(bchunk)],
        out_shape=[jax.ShapeDtypeStruct((bsz, L, w), F32)] * 2,
        scratch_shapes=scr,
        compiler_params=_params("arbitrary"),
    )(bx, bx, bx, bx, bx, bx, *small)


def _route(logits_t, bias, n_experts):
    per = n_experts // N_EXPERT_GROUPS
    tm = logits_t.shape[-1]
    scores = jax.nn.sigmoid(logits_t).reshape(N_EXPERT_GROUPS, per, tm)
    sel = scores + bias.reshape(N_EXPERT_GROUPS, per, 1)
    shape = sel.shape
    gi = lax.broadcasted_iota(jnp.int32, shape, 0)
    mi = lax.broadcasted_iota(jnp.int32, shape, 1)
    ei = gi * per + mi
    m1 = jnp.max(sel, axis=1, keepdims=True)
    first = jnp.min(jnp.where(sel == m1, mi, per), axis=1, keepdims=True)
    m2 = jnp.max(jnp.where(mi == first, NEG_INF, sel), axis=1, keepdims=True)
    gscore = m1 + m2
    gidx = lax.broadcasted_iota(jnp.int32, gscore.shape, 0)
    gmask = jnp.zeros(gscore.shape, F32)
    for _ in range(TOPK_GROUPS):
        m = jnp.max(gscore, axis=0, keepdims=True)
        pick = jnp.min(jnp.where(gscore == m, gidx, N_EXPERT_GROUPS), axis=0, keepdims=True)
        hit = gidx == pick
        gmask = jnp.where(hit, 1.0, gmask)
        gscore = jnp.where(hit, NEG_INF, gscore)
    cand = jnp.where(gmask > 0.0, sel, NEG_INF)
    chosen = jnp.zeros(shape, F32)
    for _ in range(TOP_K):
        m = jnp.max(jnp.max(cand, axis=1, keepdims=True), axis=0, keepdims=True)
        pick = jnp.where(cand == m, ei, n_experts)
        pick = jnp.min(jnp.min(pick, axis=1, keepdims=True), axis=0, keepdims=True)
        hit = ei == pick
        chosen = jnp.where(hit, 1.0, chosen)
        cand = jnp.where(hit, NEG_INF, cand)
    wsel = jnp.where(chosen > 0.0, scores, 0.0)
    den = jnp.sum(jnp.sum(wsel, axis=1, keepdims=True), axis=0, keepdims=True)
    return (wsel / den * ROUTED_SCALE).reshape(n_experts, tm)


def _outproj_kernel(ya_ref, hf_ref, hb_ref, bg_ref, yc_ref, yd_ref, x_ref, mod_ref,
                    wout_ref, gpost_ref, gffn_ref, rw_ref, rb_ref,
                    xo_ref, v_ref, gate_ref, cnt_ref, *, n_experts):
    gw = ya_ref.shape[-1]
    m = mod_ref[0, 0]
    yb = ((hf_ref[0] + hb_ref[0]) * jax.nn.gelu(bg_ref[0])).astype(BF16)
    y = (_dot(ya_ref[0], wout_ref[0:gw, :]) + _dot(yb, wout_ref[gw:2 * gw, :])
         + _dot(yc_ref[0], wout_ref[2 * gw:3 * gw, :]) + _dot(yd_ref[0], wout_ref[3 * gw:4 * gw, :]))
    x1 = x_ref[0] + m[2:3] * _rms(y, gpost_ref[...])
    xo_ref[0] = x1
    v = (_rms(x1, gffn_ref[...]) * (1.0 + m[4:5]) + m[3:4]).astype(BF16)
    v_ref[0] = v
    logits_t = _dot_t(rw_ref[...], v)
    gates_t = _route(logits_t, rb_ref[...], n_experts)
    pad = jnp.zeros((LANES - n_experts, gates_t.shape[1]), F32)
    gates = jnp.concatenate([gates_t, pad], axis=0).T
    gate_ref[0] = gates
    cnt_ref[0, 0] = jnp.sum(jnp.where(gates > 0.0, 1.0, 0.0), axis=0, keepdims=True).astype(jnp.int32)


def _outproj(ya, hf, hb, bg, yc, yd, xs, modtab, w_out, gpost, gffn, rw_t, rbias, *, need_ctx):
    bsz, L, d = xs.shape
    n_experts = rw_t.shape[0]
    gw = ya.shape[-1]
    t0 = 0 if need_ctx else 1
    nt = L // ROW_TILE - t0
    rows_out = nt * ROW_TILE
    blk = lambda w: pl.BlockSpec((1, ROW_TILE, w), lambda b, t: (b, t + t0, 0))
    oblk = lambda w: pl.BlockSpec((1, ROW_TILE, w), lambda b, t: (b, t, 0))
    return pl.pallas_call(
        functools.partial(_outproj_kernel, n_experts=n_experts),
        grid=(bsz, nt),
        in_specs=[blk(gw)] * 6 + [blk(d),
                  pl.BlockSpec((1, 1, 6, d), lambda b, t: (b, jnp.minimum(t + t0, 1), 0, 0)),
                  _full(w_out), _full(gpost), _full(gffn), _full(rw_t), _full(rbias)],
        out_specs=[oblk(d), oblk(d), oblk(LANES),
                   pl.BlockSpec((1, 1, 1, LANES), lambda b, t: (b, t, 0, 0))],
        out_shape=[jax.ShapeDtypeStruct((bsz, rows_out, d), F32), jax.ShapeDtypeStruct((bsz, rows_out, d), BF16),
                   jax.ShapeDtypeStruct((bsz, rows_out, LANES), F32),
                   jax.ShapeDtypeStruct((bsz, nt, 1, LANES), jnp.int32)],
        compiler_params=_params("arbitrary", "arbitrary"),
    )(ya, hf, hb, bg, yc, yd, xs, modtab, w_out, gpost, gffn, rw_t, rbias)


SLOT_ALIGN = 16
EXPERT_BLOCK = 1024
SLOT_BITS = (ROW_TILE // SLOT_ALIGN).bit_length()
TAIL_BITS = (EXPERT_BLOCK // SLOT_ALIGN - 1).bit_length()
SLOT_CHUNK = 512
COMMON_BITS = 3
CODE_BASE = 64.0


def _swiglu(v, wg, wu, wd):
    hg = _dot(v, wg.astype(BF16))
    hu = _dot(v, wu.astype(BF16))
    return hg * jax.nn.sigmoid(hg) * hu, wd.astype(BF16)


def _slot_cap(n_experts):
    rows = ROW_TILE * TOP_K + n_experts * (SLOT_ALIGN - 1)
    return -(-rows // SLOT_CHUNK) * SLOT_CHUNK


def _slot_codes(gates, off_row, ltri):
    chosen = gates > 0.0
    rank = _dot(ltri, jnp.where(chosen, 1.0, 0.0).astype(BF16))
    code = jnp.where(chosen, off_row.astype(F32) + rank + 1.0, 0.0)
    hi = jnp.floor(code * (1.0 / CODE_BASE))
    return hi.astype(BF16), (code - CODE_BASE * hi).astype(BF16)


def _slot_expert_onehot(off_row, len_row, first, rows):
    r = first + lax.broadcasted_iota(jnp.int32, (rows, LANES), 0)
    inside = jnp.where(r >= off_row, jnp.where(r < off_row + len_row, 1.0, 0.0), 0.0)
    return inside.astype(BF16)


def _row_digits(first, shape, axis):
    code = (first + 1 + lax.broadcasted_iota(jnp.int32, shape, axis)).astype(F32)
    hi = jnp.floor(code * (1.0 / CODE_BASE))
    return hi, code - CODE_BASE * hi


def _slot_dma(src, dst, src_off, dst_off, n_units, sem, start, n_bits=SLOT_BITS):
    def piece(bit):
        size = SLOT_ALIGN << bit
        done = (n_units & ((1 << bit) - 1)) * SLOT_ALIGN

        @pl.when(((n_units >> bit) & 1) == 1)
        def _piece():
            cp = pltpu.make_async_copy(
                src.at[pl.ds(pl.multiple_of(src_off + done, SLOT_ALIGN), size)],
                dst.at[pl.ds(pl.multiple_of(dst_off + done, SLOT_ALIGN), size)], sem)
            if start:
                cp.start()
            else:
                cp.wait()

    for bit in range(min(n_bits, COMMON_BITS)):
        piece(bit)
    if n_bits > COMMON_BITS:
        @pl.when(n_units >= (1 << COMMON_BITS))
        def _long():
            for bit in range(COMMON_BITS, n_bits):
                piece(bit)


def _wait_rows(src, dst, n_units, sem):
    for bit in range((min(src.shape[0], dst.shape[0]) // SLOT_ALIGN).bit_length()):
        size = SLOT_ALIGN << bit

        @pl.when(((n_units >> bit) & 1) == 1)
        def _piece():
            pltpu.make_async_copy(src.at[pl.ds(0, size)], dst.at[pl.ds(0, size)], sem).wait()


def _dispatch_kernel(pos_ref, off_ref, len_ref, tot_ref, tpos_ref, tlen_ref,
                     v_ref, g_ref, offv_ref, lenv_ref, ltri_ref, xs_ref,
                     buf_s, zero_s, sem, *, n_experts):
    i = pl.program_id(0)
    nt = pl.num_programs(0)
    cur = i % 2
    cap = buf_s.shape[1]

    @pl.when(i == 0)
    def _zero():
        zero_s[...] = jnp.zeros_like(zero_s)

    g = g_ref[...]
    off_row, len_row = offv_ref[0], lenv_ref[0]
    hi, lo = _slot_codes(g, off_row, ltri_ref[...])
    g_hi = g.astype(BF16)
    g_lo = (g - g_hi.astype(F32)).astype(BF16)
    src = jnp.concatenate([v_ref[...], g_hi, g_lo], axis=1)
    for c in range(cap // SLOT_CHUNK):
        first = c * SLOT_CHUNK
        oh = _slot_expert_onehot(off_row, len_row, first, SLOT_CHUNK)
        rh, rl = _row_digits(first, (SLOT_CHUNK, 1), 0)
        pick = jnp.where(_dot_t(oh, hi) == rh, jnp.where(_dot_t(oh, lo) == rl, 1.0, 0.0), 0.0)
        buf_s[cur, first:first + SLOT_CHUNK, :] = _dot(pick.astype(BF16), src).astype(BF16)

    def slots(tile, which, start):
        def body(e, carry):
            k = tile * n_experts + e
            _slot_dma(buf_s.at[which], xs_ref, off_ref[k], pos_ref[k], len_ref[k], sem.at[0], start)
            return carry
        lax.fori_loop(0, n_experts, body, 0)

    def tails(start):
        def body(e, carry):
            _slot_dma(zero_s, xs_ref, 0, tpos_ref[e], tlen_ref[e], sem.at[0], start, TAIL_BITS)
            return carry
        lax.fori_loop(0, n_experts, body, 0)

    @pl.when(i > 0)
    def _drain_previous():
        _wait_rows(buf_s.at[1 - cur], xs_ref, tot_ref[i - 1], sem.at[0])

    slots(i, cur, True)

    @pl.when(i == nt - 1)
    def _last():
        tails(True)
        _wait_rows(buf_s.at[cur], xs_ref, tot_ref[i], sem.at[0])
        tails(False)


def _dispatch(v2, gates2, plan, ltri, *, n_experts, n_rows):
    t, d = v2.shape
    nt = t // ROW_TILE
    cap = _slot_cap(n_experts)
    width = d + 2 * LANES
    grid_spec = pltpu.PrefetchScalarGridSpec(
        num_scalar_prefetch=6, grid=(nt,),
        in_specs=[pl.BlockSpec((ROW_TILE, d), lambda i, *_: (i, 0)),
                  pl.BlockSpec((ROW_TILE, LANES), lambda i, *_: (i, 0)),
                  pl.BlockSpec((1, 1, LANES), lambda i, *_: (i, 0, 0)),
                  pl.BlockSpec((1, 1, LANES), lambda i, *_: (i, 0, 0)),
                  pl.BlockSpec(ltri.shape, lambda i, *_: (0, 0))],
        out_specs=pl.BlockSpec(memory_space=pl.ANY),
        scratch_shapes=[pltpu.VMEM((2, cap, width), BF16),
                        pltpu.VMEM((EXPERT_BLOCK, width), BF16),
                        pltpu.SemaphoreType.DMA((1,))])
    return pl.pallas_call(
        functools.partial(_dispatch_kernel, n_experts=n_experts),
        grid_spec=grid_spec,
        out_shape=jax.ShapeDtypeStruct((n_rows, width), BF16),
        compiler_params=_params("arbitrary"),
    )(plan["pos"], plan["off"], plan["len"], plan["tile_len"], plan["tail_pos"], plan["tail_len"],
      v2, gates2, plan["off_v"], plan["len_v"], ltri)


def _expert_kernel(be_ref, na_ref, x_ref, wg_ref, wu_ref, wd_ref, y_ref, wg_s, wu_s, wd_s):
    b = pl.program_id(0)
    e = be_ref[b]

    @pl.when(jnp.logical_or(b == 0, e != be_ref[jnp.maximum(b - 1, 0)]))
    def _new_expert():
        wg_s[...] = wg_ref[0, 0].astype(BF16)
        wu_s[...] = wu_ref[0, 0].astype(BF16)
        wd_s[...] = wd_ref[0, 0].astype(BF16)

    @pl.when(b < na_ref[0])
    def _active():
        d = wg_s.shape[0]
        x = x_ref[:, 0:d]
        gates = x_ref[:, d:d + LANES].astype(F32) + x_ref[:, d + LANES:d + 2 * LANES].astype(F32)
        lane = lax.broadcasted_iota(jnp.int32, gates.shape, 1)
        gcol = jnp.sum(jnp.where(lane == e, gates, 0.0), axis=1, keepdims=True)
        hg = _dot(x, wg_s[...])
        h = hg * jax.nn.sigmoid(hg) * _dot(x, wu_s[...])
        y_ref[...] = (_dot(h.astype(BF16), wd_s[...]) * gcol).astype(BF16)


def _experts(xs_sorted, plan, wg, wu, wd, layer):
    n_rows, width = xs_sorted.shape
    _, _, d, de = wg.shape
    nb = n_rows // EXPERT_BLOCK
    rowmap = lambda b, be, na: (jnp.minimum(b, na[0] - 1), 0)
    grid_spec = pltpu.PrefetchScalarGridSpec(
        num_scalar_prefetch=2, grid=(nb,),
        in_specs=[pl.BlockSpec((EXPERT_BLOCK, width), rowmap),
                  pl.BlockSpec((1, 1, d, de), lambda b, be, na: (layer, be[b], 0, 0)),
                  pl.BlockSpec((1, 1, d, de), lambda b, be, na: (layer, be[b], 0, 0)),
                  pl.BlockSpec((1, 1, de, d), lambda b, be, na: (layer, be[b], 0, 0))],
        out_specs=pl.BlockSpec((EXPERT_BLOCK, d), rowmap),
        scratch_shapes=[pltpu.VMEM((d, de), BF16), pltpu.VMEM((d, de), BF16), pltpu.VMEM((de, d), BF16)])
    return pl.pallas_call(
        _expert_kernel, grid_spec=grid_spec,
        out_shape=jax.ShapeDtypeStruct((n_rows, d), BF16),
        compiler_params=_params("arbitrary"),
    )(plan["block_expert"], plan["n_active"], xs_sorted, wg, wu, wd)


def _combine_kernel(pos_ref, off_ref, len_ref, tot_ref,
                    ys_ref, v_ref, g_ref, offv_ref, lenv_ref, ltri_ref, x_ref, mod_ref,
                    sg_ref, su_ref, sd_ref, gpost_ref, o_ref, buf_s, sem, *, n_experts):
    i = pl.program_id(0)
    nt = pl.num_programs(0)
    cur = i % 2
    cap = buf_s.shape[1]

    def slots(tile, which, start):
        def body(e, carry):
            k = tile * n_experts + e
            _slot_dma(ys_ref, buf_s.at[which], pos_ref[k], off_ref[k], len_ref[k], sem.at[which], start)
            return carry
        lax.fori_loop(0, n_experts, body, 0)

    @pl.when(i == 0)
    def _first():
        buf_s[...] = jnp.zeros_like(buf_s)
        slots(0, 0, True)

    @pl.when(i + 1 < nt)
    def _prefetch():
        slots(i + 1, 1 - cur, True)

    v = v_ref[...]
    h, wd = _swiglu(v, sg_ref[...], su_ref[...], sd_ref[...])
    acc = _dot(h.astype(BF16), wd)
    off_row, len_row = offv_ref[0], lenv_ref[0]
    hi, lo = _slot_codes(g_ref[...], off_row, ltri_ref[...])

    _wait_rows(ys_ref, buf_s.at[cur], tot_ref[i], sem.at[cur])
    for c in range(cap // SLOT_CHUNK):
        first = c * SLOT_CHUNK
        oh = _slot_expert_onehot(off_row, len_row, first, SLOT_CHUNK)
        rh, rl = _row_digits(first, (1, SLOT_CHUNK), 1)
        pick = jnp.where(_dot_t(hi, oh) == rh, jnp.where(_dot_t(lo, oh) == rl, 1.0, 0.0), 0.0)
        acc = acc + _dot(pick.astype(BF16), buf_s[cur, first:first + SLOT_CHUNK, :])
    o_ref[0] = x_ref[0] + mod_ref[0, 0][5:6] * _rms(acc, gpost_ref[...])


def _combine(ys_sorted, v2, gates2, plan, ltri, xs, modtab, sg, su, sd, gpost, *, n_experts, has_ctx):
    bsz, L, d = xs.shape
    tpb = L // ROW_TILE
    nt = bsz * tpb
    cap = _slot_cap(n_experts)
    full = lambda a: pl.BlockSpec(a.shape, lambda i, *_: (0,) * a.ndim)
    grid_spec = pltpu.PrefetchScalarGridSpec(
        num_scalar_prefetch=4, grid=(nt,),
        in_specs=[pl.BlockSpec(memory_space=pl.ANY),
                  pl.BlockSpec((ROW_TILE, d), lambda i, *_: (i, 0)),
                  pl.BlockSpec((ROW_TILE, LANES), lambda i, *_: (i, 0)),
                  pl.BlockSpec((1, 1, LANES), lambda i, *_: (i, 0, 0)),
                  pl.BlockSpec((1, 1, LANES), lambda i, *_: (i, 0, 0)),
                  full(ltri),
                  pl.BlockSpec((1, ROW_TILE, d), lambda i, *_: (i // tpb, i % tpb, 0)),
                  pl.BlockSpec((1, 1, 6, d),
                               lambda i, *_: (i // tpb, jnp.minimum(i % tpb, 1) if has_ctx else 1, 0, 0)),
                  full(sg), full(su), full(sd), full(gpost)],
        out_specs=pl.BlockSpec((1, ROW_TILE, d), lambda i, *_: (i // tpb, i % tpb, 0)),
        scratch_shapes=[pltpu.VMEM((2, cap, d), BF16), pltpu.SemaphoreType.DMA((2,))])
    return pl.pallas_call(
        functools.partial(_combine_kernel, n_experts=n_experts),
        grid_spec=grid_spec,
        out_shape=jax.ShapeDtypeStruct((bsz, L, d), F32),
        compiler_params=_params("arbitrary"),
    )(plan["pos"], plan["off"], plan["len"], plan["tile_len"], ys_sorted, v2, gates2, plan["off_v"], plan["len_v"], ltri,
      xs, modtab, sg, su, sd, gpost)


def _moe_plan(counts, n_experts, n_blocks):
    a = (counts + (SLOT_ALIGN - 1)) // SLOT_ALIGN * SLOT_ALIGN
    rows = jnp.sum(a, axis=0)
    region = (rows + (EXPERT_BLOCK - 1)) // EXPERT_BLOCK * EXPERT_BLOCK
    region_end = jnp.cumsum(region)
    region_start = region_end - region
    pos = region_start[None, :] + jnp.cumsum(a, axis=0) - a
    off = jnp.cumsum(a, axis=1) - a
    first_row = jnp.arange(n_blocks, dtype=jnp.int32) * EXPERT_BLOCK
    block_expert = jnp.minimum(jnp.sum(region_end[None, :n_experts] <= first_row[:, None], axis=1), n_experts - 1)
    flat = lambda t: t[:, :n_experts].reshape(-1).astype(jnp.int32)
    nt = counts.shape[0]
    return {
        "pos": flat(pos), "off": flat(off), "len": flat(a // SLOT_ALIGN),
        "tile_len": (jnp.sum(a, axis=1) // SLOT_ALIGN).astype(jnp.int32),
        "tail_pos": (region_start + rows)[:n_experts].astype(jnp.int32),
        "tail_len": ((region - rows) // SLOT_ALIGN)[:n_experts].astype(jnp.int32),
        "off_v": off.reshape(nt, 1, LANES).astype(jnp.int32),
        "len_v": a.reshape(nt, 1, LANES).astype(jnp.int32),
        "block_expert": block_expert.astype(jnp.int32),
        "n_active": (region_end[-1:] // EXPERT_BLOCK).astype(jnp.int32),
    }


def _moe(v, gates, counts, xs, modtab, wg, wu, wd, layer, sg, su, sd, gpost, *, has_ctx):
    bsz, L, d = xs.shape
    n_experts = wg.shape[1]
    t = bsz * L
    nt = t // ROW_TILE
    worst = t * TOP_K + nt * n_experts * (SLOT_ALIGN - 1) + n_experts * (EXPERT_BLOCK - 1)
    n_blocks = -(-worst // EXPERT_BLOCK)
    plan = _moe_plan(counts.reshape(nt, LANES), n_experts, n_blocks)
    ltri = jnp.tril(jnp.ones((ROW_TILE, ROW_TILE), F32), -1).astype(BF16)
    v2, gates2 = v.reshape(t, d), gates.reshape(t, LANES)
    xs_sorted = _dispatch(v2, gates2, plan, ltri, n_experts=n_experts, n_rows=n_blocks * EXPERT_BLOCK)
    ys_sorted = _experts(xs_sorted, plan, wg, wu, wd, layer)
    return _combine(ys_sorted, v2, gates2, plan, ltri, xs, modtab, sg, su, sd, gpost,
                    n_experts=n_experts, has_ctx=has_ctx)


def _tables(seq, ctx_len):
    def build(rot_dim, lead, slot):
        n = rot_dim // 4
        pos = jnp.arange(seq)
        row = (pos // GRID_W).astype(F32)
        col = (pos % GRID_W).astype(F32)
        inv = ROPE_THETA ** (-jnp.arange(n, dtype=F32) / n)
        ang = jnp.concatenate([row[:, None] * inv, col[:, None] * inv], axis=-1)
        cos, sin = jnp.cos(ang), jnp.sin(ang)
        zero = jnp.zeros_like(sin)
        tail = slot - lead - rot_dim
        one_l, zero_l = jnp.ones((seq, lead), F32), jnp.zeros((seq, lead), F32)
        one_t, zero_t = jnp.ones((seq, tail), F32), jnp.zeros((seq, tail), F32)
        c = jnp.concatenate([one_l, cos, cos, one_t], axis=1)
        s_lo = jnp.concatenate([zero_l, -sin, zero, zero_t], axis=1)
        s_hi = jnp.concatenate([zero_l, zero, sin, zero_t], axis=1)
        ctx_c = jnp.ones((ctx_len, slot), F32)
        ctx_s = jnp.zeros((ctx_len, slot), F32)
        return [jnp.concatenate([ctx_c, c], axis=0), jnp.concatenate([ctx_s, s_lo], axis=0),
                jnp.concatenate([ctx_s, s_hi], axis=0)]

    hd = [jnp.concatenate([t, t], axis=1) for t in build(HEAD_DIM, 0, HEAD_DIM)]
    mla = build(QK_ROPE, QK_NOPE, LANES)
    return hd + mla


def _dup_heads(w, n_heads):
    d = w.shape[0]
    wh = w.reshape(d, n_heads, 1, HEAD_DIM)
    return jnp.broadcast_to(wh, (d, n_heads, 2, HEAD_DIM)).reshape(d, n_heads * 2 * HEAD_DIM)


def kernel(x, c, ctx, c_ctx, w_mod, b_mod, g_pre_mix, g_post_mix, g_pre_ffn, g_post_ffn, w_in, a_q_norm, a_k_norm, conv_w, conv_b, lru_wa, lru_ba, lru_wi, lru_bi, lru_lambda, c_sink, d_q_norm, d_w_uq, d_kv_norm, d_w_ukv, w_out, router_w, router_bias, w_gate, w_up, w_down, sh_gate, sh_up, sh_down):
    bsz, seq, d = x.shape
    ctx_len = ctx.shape[1]
    depth = w_mod.shape[0]
    gw = d // 4
    lru_w = conv_w.shape[-1]
    q_lora = d_q_norm.shape[-1]
    kv_lora = d_kv_norm.shape[-1]
    n_experts = router_w.shape[-1]
    a_heads = gw // HEAD_DIM
    a_kv = (w_in.shape[-1] - (2 * gw + 2 * lru_w + q_lora + kv_lora + QK_ROPE)) // (4 * HEAD_DIM)
    mla_scale = (QK_NOPE + QK_ROPE) ** -0.5
    v_dim = gw // D_HEADS
    assert ctx_len == ROW_TILE and a_kv == 2 and a_heads == 4 and n_experts <= LANES

    cond = jnp.zeros((16, d), F32).at[:bsz].set(c).at[bsz].set(c_ctx)
    mod = _modulation(cond, w_mod, b_mod).reshape(depth, 16, 6, d)
    tabs = _tables(seq, ctx_len)
    seg = jnp.kron(jnp.eye(a_heads, dtype=F32), jnp.ones((HEAD_DIM, HEAD_DIM), F32)).astype(BF16)

    xs = jnp.concatenate([ctx, x], axis=1)
    for l in range(depth):
        modtab = jnp.stack([jnp.broadcast_to(mod[l, bsz], (bsz, 6, d)), mod[l, :bsz]], axis=1)

        offs = [0]
        for wdt in (gw, a_kv * HEAD_DIM, a_kv * HEAD_DIM, lru_w, lru_w, gw, a_kv * HEAD_DIM, a_kv * HEAD_DIM,
                    q_lora, kv_lora, QK_ROPE):
            offs.append(offs[-1] + wdt)
        col = lambda i: w_in[l][:, offs[i]:offs[i + 1]]
        zeros = lambda n: jnp.zeros((d, n), F32)
        kr_slot = jnp.concatenate([zeros(QK_NOPE), col(10), zeros(LANES - QK_NOPE - QK_ROPE)], axis=1)
        w_in_p = jnp.concatenate(
            [col(0), _dup_heads(col(1), a_kv), _dup_heads(col(2), a_kv), col(3), col(4),
             col(5), _dup_heads(col(6), a_kv), _dup_heads(col(7), a_kv),
             col(8), zeros(2 * LANES - q_lora), col(9)] + [kr_slot] * D_HEADS, axis=1).astype(BF16)

        qk = QK_NOPE + QK_ROPE
        wuq = d_w_uq[l].reshape(q_lora, D_HEADS, qk)
        wuq = jnp.pad(wuq, ((0, 2 * LANES - q_lora), (0, 0), (0, LANES - qk))).reshape(2 * LANES, D_HEADS * LANES)
        wukv = d_w_ukv[l].reshape(kv_lora, D_HEADS, QK_NOPE + v_dim)
        wk = jnp.pad(wukv[:, :, :QK_NOPE], ((0, 0), (0, 0), (0, LANES - QK_NOPE))).reshape(kv_lora, D_HEADS * LANES)
        wv = wukv[:, :, QK_NOPE:].reshape(kv_lora, D_HEADS * v_dim)
        wukv_p = jnp.concatenate([wk, wv], axis=1).astype(BF16)
        dqn = jnp.pad(d_q_norm[l], (0, 2 * LANES - q_lora)).reshape(1, 2 * LANES)

        qa, ka, va, bx, bg, qc, kc, vc, qd, kd, vd = _inproj(
            xs, modtab, g_pre_mix[l].reshape(1, d), w_in_p, seg, tabs,
            jnp.tile(a_q_norm[l], a_heads).reshape(1, gw), jnp.tile(a_k_norm[l], 2 * a_kv).reshape(1, gw),
            dqn, wuq.astype(BF16), d_kv_norm[l].reshape(1, kv_lora), wukv_p,
            q_lora=q_lora, mla_scale=mla_scale)

        need_ctx = l < depth - 1
        ya = _attention(qa, ka, va, split_q=True, ctx_len=ctx_len, need_ctx=need_ctx)
        yd = _attention(qd, kd, vd, split_q=False, ctx_len=ctx_len, need_ctx=need_ctx)
        yc = _window_attention(qc, kc, vc, c_sink[l], ctx_len=ctx_len, need_ctx=need_ctx)

        blocks = lru_wa.shape[2]
        bdiag = lambda wts: jnp.stack([jax.scipy.linalg.block_diag(*[wts[dd, h] for h in range(blocks)])
                                       for dd in range(2)]).astype(BF16)
        hf, hb = _lru(bx, conv_w[l], conv_b[l].reshape(1, lru_w), bdiag(lru_wa[l]), lru_ba[l].reshape(2, 1, lru_w),
                      bdiag(lru_wi[l]), lru_bi[l].reshape(2, 1, lru_w), lru_lambda[l].reshape(2, 1, lru_w),
                      ctx_len=ctx_len)

        rw_t = router_w[l].T.astype(BF16)
        xs_mid, v, gates, counts = _outproj(ya, hf, hb, bg, yc, yd, xs, modtab, w_out[l].astype(BF16),
                                            g_post_mix[l].reshape(1, d), g_pre_ffn[l].reshape(1, d),
                                            rw_t, router_bias[l].reshape(n_experts, 1), need_ctx=need_ctx)
        xs = _moe(v, gates, counts, xs_mid, modtab, w_gate, w_up, w_down, l,
                  sh_gate[l], sh_up[l], sh_down[l], g_post_ffn[l].reshape(1, d), has_ctx=need_ctx)
    return xs
```

```python
import functools

import jax
import jax.numpy as jnp
from jax import lax
from jax.experimental import pallas as pl
from jax.experimental.pallas import tpu as pltpu

F32 = jnp.float32
BF16 = jnp.bfloat16

GRID_W = 64
HEAD_DIM = 64
ROPE_THETA = 10000.0
NORM_EPS = 1e-6
WINDOW = 128
Q_BLOCK = 128
D_HEADS = 4
QK_NOPE = 64
QK_ROPE = 32
LRU_C = 8.0
CONV_W = 4
TOP_K = 8
N_EXPERT_GROUPS = 8
TOPK_GROUPS = 4
ROUTED_SCALE = 2.5

LANES = 128
VMEM_LIMIT = 56 * 1024 * 1024
ROW_TILE = 256

NEG_INF = float("-inf")


def _params(*sem):
    return pltpu.CompilerParams(dimension_semantics=sem, vmem_limit_bytes=VMEM_LIMIT)


def _rms(x, gain):
    return x * lax.rsqrt(jnp.mean(x * x, axis=-1, keepdims=True) + NORM_EPS) * gain


def _dot(a, b):
    return jnp.dot(a, b, preferred_element_type=F32)


def _dot_t(a, b):
    return lax.dot_general(a, b, (((1,), (1,)), ((), ())), preferred_element_type=F32)


def _full(a):
    return pl.BlockSpec(a.shape, lambda *_: (0,) * a.ndim)


def _mod_kernel(c_ref, w_ref, b_ref, o_ref):
    a = c_ref[...]
    a = a * jax.nn.sigmoid(a)
    o_ref[0] = _dot(a.astype(BF16), w_ref[0].astype(BF16)) + b_ref[0]


def _modulation(cond, w_mod, b_mod):
    depth, d, n = w_mod.shape
    tn = n // 4
    rows = cond.shape[0]
    return pl.pallas_call(
        _mod_kernel,
        grid=(depth, n // tn),
        in_specs=[pl.BlockSpec((rows, d), lambda l, j: (0, 0)),
                  pl.BlockSpec((1, d, tn), lambda l, j: (l, 0, j)),
                  pl.BlockSpec((1, 1, tn), lambda l, j: (l, 0, j))],
        out_specs=pl.BlockSpec((1, rows, tn), lambda l, j: (l, 0, j)),
        out_shape=jax.ShapeDtypeStruct((depth, rows, n), F32),
        compiler_params=_params("arbitrary", "arbitrary"),
    )(cond, w_mod, b_mod.reshape(depth, 1, n))


def _rope(t, cos, sin_lo, sin_hi, half):
    w = t.shape[-1]
    return t * cos + pltpu.roll(t, w - half, 1) * sin_lo + pltpu.roll(t, half, 1) * sin_hi


def _inproj_kernel(x_ref, mod_ref, gpre_ref, w_ref, seg_ref,
                   cos_ref, sinl_ref, sinh_ref, cosd_ref, sindl_ref, sindh_ref,
                   aqn_ref, akn_ref, dqn_ref, wuq_ref, dkvn_ref, wukv_ref,
                   qa_ref, ka_ref, va_ref, bx_ref, bg_ref, qc_ref, kc_ref, vc_ref,
                   qd_ref, kd_ref, vd_ref, *, q_lora, mla_scale):
    x = x_ref[0]
    m = mod_ref[0, 0]
    u = (_rms(x, gpre_ref[...]) * (1.0 + m[1:2]) + m[0:1]).astype(BF16)
    proj = _dot(u, w_ref[...])

    seg = seg_ref[...]
    cos = jnp.concatenate([cos_ref[...]] * 2, axis=1)
    sin_lo = jnp.concatenate([sinl_ref[...]] * 2, axis=1)
    sin_hi = jnp.concatenate([sinh_ref[...]] * 2, axis=1)
    cosd = jnp.concatenate([cosd_ref[...]] * 4, axis=1)
    sind_lo = jnp.concatenate([sindl_ref[...]] * 4, axis=1)
    sind_hi = jnp.concatenate([sindh_ref[...]] * 4, axis=1)

    def head_rms(t, gain):
        sq = t * t
        hi = sq.astype(BF16)
        lo = (sq - hi.astype(F32)).astype(BF16)
        ms = (_dot(hi, seg) + _dot(lo, seg)) * (1.0 / HEAD_DIM)
        return t * lax.rsqrt(ms + NORM_EPS) * gain

    def rope_hd(t):
        return _rope(t, cos, sin_lo, sin_hi, HEAD_DIM // 2)

    def rope_r(t):
        return _rope(t, cosd, sind_lo, sind_hi, QK_ROPE // 2)

    sc_hd = HEAD_DIM ** -0.5
    qa_ref[0] = (rope_hd(head_rms(proj[:, 0:256], aqn_ref[...])) * sc_hd).astype(BF16)
    ka_ref[0] = rope_hd(head_rms(proj[:, 256:512], akn_ref[...])).astype(BF16)
    va_ref[0] = proj[:, 512:768].astype(BF16)
    bx_ref[0] = proj[:, 768:1024]
    bg_ref[0] = proj[:, 1024:1280]
    qc_ref[0] = (rope_hd(proj[:, 1280:1536]) * sc_hd).astype(BF16)
    kc_ref[0] = rope_hd(proj[:, 1536:1792]).astype(BF16)
    vc_ref[0] = proj[:, 1792:2048].astype(BF16)
    cq = proj[:, 2048:2304]
    cq = cq * lax.rsqrt(jnp.sum(cq * cq, axis=-1, keepdims=True) * (1.0 / q_lora) + NORM_EPS) * dqn_ref[...]
    qd = _dot(cq.astype(BF16), wuq_ref[...])
    qd_ref[0] = (rope_r(qd) * mla_scale).astype(BF16)
    ckv = _rms(proj[:, 2304:2432], dkvn_ref[...])
    kv = _dot(ckv.astype(BF16), wukv_ref[...])
    kd_ref[0] = (kv[:, 0:512] + rope_r(proj[:, 2432:2944])).astype(BF16)
    vd_ref[0] = kv[:, 512:768].astype(BF16)


def _inproj(xs, modtab, gpre, w_in_p, seg, tabs, aqn, akn, dqn, wuq, dkvn, wukv, *, q_lora, mla_scale):
    bsz, L, d = xs.shape
    nt = L // ROW_TILE
    rowblk = lambda w: pl.BlockSpec((1, ROW_TILE, w), lambda b, t: (b, t, 0))
    tab = lambda a: pl.BlockSpec((ROW_TILE, a.shape[1]), lambda b, t: (t, 0))
    out_w = [(256, BF16)] * 3 + [(256, F32)] * 2 + [(256, BF16)] * 3 + [(512, BF16), (512, BF16), (256, BF16)]
    return pl.pallas_call(
        functools.partial(_inproj_kernel, q_lora=q_lora, mla_scale=mla_scale),
        grid=(bsz, nt),
        in_specs=[rowblk(d),
                  pl.BlockSpec((1, 1, 6, d), lambda b, t: (b, jnp.minimum(t, 1), 0, 0)),
                  _full(gpre), _full(w_in_p), _full(seg)] + [tab(a) for a in tabs]
                 + [_full(a) for a in (aqn, akn, dqn, wuq, dkvn, wukv)],
        out_specs=[rowblk(w) for w, _ in out_w],
        out_shape=[jax.ShapeDtypeStruct((bsz, L, w), dt) for w, dt in out_w],
        compiler_params=_params("arbitrary", "arbitrary"),
    )(xs, modtab, gpre, w_in_p, seg, *tabs, aqn, akn, dqn, wuq, dkvn, wukv)


def _attn_kernel(q_ref, k_ref, v_ref, o_ref, *, split_q, ctx_len, k_len, first_tile):
    lane = lax.broadcasted_iota(jnp.int32, (1, LANES), 1)
    low = lane < HEAD_DIM

    def run(n_keys):
        for g in range(v_ref.shape[-1] // LANES):
            outs = []
            for h in range(2):
                if split_q:
                    lanes = slice(g * LANES, (g + 1) * LANES)
                    keep = low if h == 0 else jnp.logical_not(low)
                    q = q_ref[0, :, lanes]
                    q = jnp.where(keep, q, jnp.zeros_like(q))
                else:
                    lanes = slice((2 * g + h) * LANES, (2 * g + h + 1) * LANES)
                    q = q_ref[0, :, lanes]
                s = _dot_t(q, k_ref[0, 0:n_keys, lanes])
                p = jnp.exp(s - jnp.max(s, axis=-1, keepdims=True))
                den = jnp.sum(p, axis=-1, keepdims=True)
                outs.append(_dot(p.astype(BF16), v_ref[0, 0:n_keys, g * LANES:(g + 1) * LANES]) / den)
            o_ref[0, :, g * LANES:(g + 1) * LANES] = jnp.where(low, outs[0], outs[1]).astype(o_ref.dtype)

    if first_tile > 0:
        run(k_len)
        return
    t = pl.program_id(1)

    @pl.when(t == 0)
    def _ctx():
        run(ctx_len)

    @pl.when(t > 0)
    def _lat():
        run(k_len)


def _attention(q, k, v, *, split_q, ctx_len, need_ctx):
    bsz, L, qw = q.shape
    vw = v.shape[-1]
    tq = ROW_TILE
    assert ctx_len == tq
    t0 = 0 if need_ctx else 1
    return pl.pallas_call(
        functools.partial(_attn_kernel, split_q=split_q, ctx_len=ctx_len, k_len=L, first_tile=t0),
        grid=(bsz, L // tq - t0),
        in_specs=[pl.BlockSpec((1, tq, qw), lambda b, t: (b, t + t0, 0)),
                  pl.BlockSpec((1, L, qw), lambda b, t: (b, 0, 0)),
                  pl.BlockSpec((1, L, vw), lambda b, t: (b, 0, 0))],
        out_specs=pl.BlockSpec((1, tq, vw), lambda b, t: (b, t + t0, 0)),
        out_shape=jax.ShapeDtypeStruct((bsz, L, vw), BF16),
        compiler_params=_params("arbitrary", "arbitrary"),
    )(q, k, v)


def _winattn_kernel(sink_ref, q_ref, k_ref, v_ref, o_ref, *, ctx_len, seq, first_block):
    t = pl.program_id(1) + first_block
    lane = lax.broadcasted_iota(jnp.int32, (1, LANES), 1)
    low = lane < HEAD_DIM
    ctx_blocks = ctx_len // Q_BLOCK
    win = 3 * Q_BLOCK
    n_groups = q_ref.shape[-1] // LANES

    def finish(parts, sink):
        m = sink
        for s, _ in parts:
            m = jnp.maximum(m, jnp.max(s, axis=-1, keepdims=True))
        den = jnp.exp(sink - m)
        acc = None
        for s, vv in parts:
            p = jnp.exp(s - m)
            den = den + jnp.sum(p, axis=-1, keepdims=True)
            o = _dot(p.astype(BF16), vv)
            acc = o if acc is None else acc + o
        return acc / den

    def heads(fn):
        for g in range(n_groups):
            lanes = slice(g * LANES, (g + 1) * LANES)
            outs = []
            for h in range(2):
                keep = low if h == 0 else jnp.logical_not(low)
                q = q_ref[0, :, lanes]
                q = jnp.where(keep, q, jnp.zeros_like(q))
                outs.append(fn(q, sink_ref[2 * g + h], lanes))
            o_ref[0, :, lanes] = jnp.where(low, outs[0], outs[1]).astype(o_ref.dtype)

    def ctx_queries():
        def one(q, sink, lanes):
            return finish([(_dot_t(q, k_ref[0, 0:ctx_len, lanes]), v_ref[0, 0:ctx_len, lanes])], sink)
        heads(one)

    def latent_queries():
        n = t - ctx_blocks
        start = jnp.clip((n - 1) * Q_BLOCK, 0, seq - win)
        off = pl.multiple_of(ctx_len + start, Q_BLOCK)
        qpos = n * Q_BLOCK + lax.broadcasted_iota(jnp.int32, (Q_BLOCK, win), 0)
        kpos = start + lax.broadcasted_iota(jnp.int32, (Q_BLOCK, win), 1)
        allowed = jnp.abs(kpos - qpos) <= WINDOW

        def one(q, sink, lanes):
            s_ctx = _dot_t(q, k_ref[0, 0:ctx_len, lanes])
            s_win = jnp.where(allowed, _dot_t(q, k_ref[0, pl.ds(off, win), lanes]), NEG_INF)
            return finish([(s_ctx, v_ref[0, 0:ctx_len, lanes]), (s_win, v_ref[0, pl.ds(off, win), lanes])], sink)
        heads(one)

    if first_block >= ctx_blocks:
        latent_queries()
    else:
        pl.when(t < ctx_blocks)(ctx_queries)
        pl.when(t >= ctx_blocks)(latent_queries)


def _window_attention(q, k, v, sink, *, ctx_len, need_ctx):
    bsz, L, w = q.shape
    b0 = 0 if need_ctx else ctx_len // Q_BLOCK
    return pl.pallas_call(
        functools.partial(_winattn_kernel, ctx_len=ctx_len, seq=L - ctx_len, first_block=b0),
        grid=(bsz, L // Q_BLOCK - b0),
        in_specs=[pl.BlockSpec(memory_space=pltpu.SMEM),
                  pl.BlockSpec((1, Q_BLOCK, w), lambda b, t: (b, t + b0, 0)),
                  pl.BlockSpec((1, L, w), lambda b, t: (b, 0, 0)),
                  pl.BlockSpec((1, L, w), lambda b, t: (b, 0, 0))],
        out_specs=pl.BlockSpec((1, Q_BLOCK, w), lambda b, t: (b, t + b0, 0)),
        out_shape=jax.ShapeDtypeStruct((bsz, L, w), BF16),
        compiler_params=_params("arbitrary", "arbitrary"),
    )(sink, q, k, v)


def _lru_kernel(xf_ref, xfp_ref, xfn_ref, xb_ref, xbp_ref, xbn_ref,
                cw_ref, cb_ref, wa_ref, ba_ref, wi_ref, bi_ref, lam_ref,
                hf_ref, hb_ref,
                af_s, bf_s, ab_s, bb_s, of_s, ob_s, sf_s, sb_s, *, n_chunks):
    i = pl.program_id(0)
    bsz, tc, w = xf_ref.shape
    rows = bsz * tc
    ti = lax.broadcasted_iota(jnp.int32, (bsz, tc, w), 1)

    def coeffs(x_ref, prev_ref, next_ref, chunk, d, a_s, b_s):
        has_prev = (chunk >= 2).astype(F32)
        has_next = jnp.logical_and(chunk >= 1, chunk <= n_chunks - 2).astype(F32)
        x = x_ref[...]
        p1 = prev_ref[:, 7:8, :] * has_prev
        n0 = next_ref[:, 0:1, :] * has_next
        n1 = next_ref[:, 1:2, :] * has_next
        x2 = x.reshape(rows, w)
        xm1 = jnp.where(ti == 0, p1, pltpu.roll(x2, 1, 0).reshape(bsz, tc, w))
        xp1 = jnp.where(ti == tc - 1, n0, pltpu.roll(x2, rows - 1, 0).reshape(bsz, tc, w))
        xp2 = jnp.where(ti == tc - 1, n1,
                        jnp.where(ti == tc - 2, n0, pltpu.roll(x2, rows - 2, 0).reshape(bsz, tc, w)))
        cw = cw_ref[...]
        xc = (cb_ref[...] + xm1 * cw[0:1] + x * cw[1:2] + xp1 * cw[2:3] + xp2 * cw[3:4]).reshape(rows, w)
        xcb = xc.astype(BF16)
        r = jax.nn.sigmoid(_dot(xcb, wa_ref[d]) + ba_ref[d])
        ig = jax.nn.sigmoid(_dot(xcb, wi_ref[d]) + bi_ref[d])
        log_a = (-LRU_C) * r * jax.nn.softplus(-lam_ref[d])
        a = jnp.exp(log_a)
        b = jnp.sqrt(1.0 - a * a) * (ig * xc)
        for j in range(w // LANES):
            a_s[j] = a[:, j * LANES:(j + 1) * LANES]
            b_s[j] = b[:, j * LANES:(j + 1) * LANES]

    chunk_b = jnp.where(i == 0, 0, n_chunks - i)
    coeffs(xf_ref, xfp_ref, xfn_ref, i, 0, af_s, bf_s)
    coeffs(xb_ref, xbp_ref, xbn_ref, chunk_b, 1, ab_s, bb_s)

    @pl.when(i == 0)
    def _init():
        sf_s[...] = jnp.zeros_like(sf_s)
        sb_s[...] = jnp.zeros_like(sb_s)

    nl = w // LANES

    def step(t, carry):
        fwd = pl.ds(t, bsz, stride=tc)
        bwd = pl.ds(tc - 1 - t, bsz, stride=tc)
        out = []
        for j in range(nl):
            hf = af_s[j, fwd, :] * carry[j] + bf_s[j, fwd, :]
            hb = ab_s[j, bwd, :] * carry[nl + j] + bb_s[j, bwd, :]
            of_s[j, fwd, :] = hf
            ob_s[j, bwd, :] = hb
            out.append((hf, hb))
        return tuple(o[0] for o in out) + tuple(o[1] for o in out)

    init = tuple(sf_s[j] for j in range(nl)) + tuple(sb_s[j] for j in range(nl))
    fin = lax.fori_loop(0, tc, step, init, unroll=8)
    for j in range(nl):
        sf_s[j] = fin[j]
        sb_s[j] = fin[nl + j]
    hf_ref[...] = jnp.concatenate([of_s[j] for j in range(nl)], axis=1).reshape(bsz, tc, w)
    hb_ref[...] = jnp.concatenate([ob_s[j] for j in range(nl)], axis=1).reshape(bsz, tc, w)


def _lru(bx, conv_w, conv_b, wa, ba, wi, bi, lam, *, ctx_len):
    bsz, L, w = bx.shape
    tc = ctx_len
    nc = L // tc
    hb_blocks = tc // 8
    last8 = L // 8 - 1

    def fchunk(i):
        return i

    def bchunk(i):
        return jnp.where(i == 0, 0, nc - i)

    def cur(cf):
        return pl.BlockSpec((bsz, tc, w), lambda i: (0, cf(i), 0))

    def prev(cf):
        return pl.BlockSpec((bsz, 8, w), lambda i: (0, jnp.maximum(cf(i) * hb_blocks - 1, 0), 0))

    def nxt(cf):
        return pl.BlockSpec((bsz, 8, w), lambda i: (0, jnp.minimum((cf(i) + 1) * hb_blocks, last8), 0))

    small = [conv_w, conv_b, wa, ba, wi, bi, lam]
    nl = w // LANES
    scr = [pltpu.VMEM((nl, bsz * tc, LANES), F32)] * 6 + [pltpu.VMEM((nl, bsz, LANES), F32)] * 2
    return pl.pallas_call(
        functools.partial(_lru_kernel, n_chunks=nc),
        grid=(nc,),
        in_specs=[cur(fchunk), prev(fchunk), nxt(fchunk), cur(bchunk), prev(bchunk), nxt(bchunk)]
                 + [_full(a) for a in small],
        out_specs=[cur(fchunk), cur(bchunk)],
        out_shape=[jax.ShapeDtypeStruct((bsz, L, w), F32)] * 2,
        scratch_shapes=scr,
        compiler_params=_params("arbitrary"),
    )(bx, bx, bx, bx, bx, bx, *small)


def _route(logits_t, bias, n_experts):
    per = n_experts // N_EXPERT_GROUPS
    tm = logits_t.shape[-1]
    scores = jax.nn.sigmoid(logits_t).reshape(N_EXPERT_GROUPS, per, tm)
    sel = scores + bias.reshape(N_EXPERT_GROUPS, per, 1)
    shape = sel.shape
    gi = lax.broadcasted_iota(jnp.int32, shape, 0)
    mi = lax.broadcasted_iota(jnp.int32, shape, 1)
    ei = gi * per + mi
    m1 = jnp.max(sel, axis=1, keepdims=True)
    first = jnp.min(jnp.where(sel == m1, mi, per), axis=1, keepdims=True)
    m2 = jnp.max(jnp.where(mi == first, NEG_INF, sel), axis=1, keepdims=True)
    gscore = m1 + m2
    gidx = lax.broadcasted_iota(jnp.int32, gscore.shape, 0)
    gmask = jnp.zeros(gscore.shape, F32)
    for _ in range(TOPK_GROUPS):
        m = jnp.max(gscore, axis=0, keepdims=True)
        pick = jnp.min(jnp.where(gscore == m, gidx, N_EXPERT_GROUPS), axis=0, keepdims=True)
        hit = gidx == pick
        gmask = jnp.where(hit, 1.0, gmask)
        gscore = jnp.where(hit, NEG_INF, gscore)
    cand = jnp.where(gmask > 0.0, sel, NEG_INF)
    chosen = jnp.zeros(shape, F32)
    for _ in range(TOP_K):
        m = jnp.max(jnp.max(cand, axis=1, keepdims=True), axis=0, keepdims=True)
        pick = jnp.where(cand == m, ei, n_experts)
        pick = jnp.min(jnp.min(pick, axis=1, keepdims=True), axis=0, keepdims=True)
        hit = ei == pick
        chosen = jnp.where(hit, 1.0, chosen)
        cand = jnp.where(hit, NEG_INF, cand)
    wsel = jnp.where(chosen > 0.0, scores, 0.0)
    den = jnp.sum(jnp.sum(wsel, axis=1, keepdims=True), axis=0, keepdims=True)
    return (wsel / den * ROUTED_SCALE).reshape(n_experts, tm)


def _outproj_kernel(ya_ref, hf_ref, hb_ref, bg_ref, yc_ref, yd_ref, x_ref, mod_ref,
                    wout_ref, gpost_ref, gffn_ref, rw_ref, rb_ref,
                    xo_ref, v_ref, gate_ref, cnt_ref, *, n_experts):
    gw = ya_ref.shape[-1]
    m = mod_ref[0, 0]
    yb = ((hf_ref[0] + hb_ref[0]) * jax.nn.gelu(bg_ref[0])).astype(BF16)
    y = (_dot(ya_ref[0], wout_ref[0:gw, :]) + _dot(yb, wout_ref[gw:2 * gw, :])
         + _dot(yc_ref[0], wout_ref[2 * gw:3 * gw, :]) + _dot(yd_ref[0], wout_ref[3 * gw:4 * gw, :]))
    x1 = x_ref[0] + m[2:3] * _rms(y, gpost_ref[...])
    xo_ref[0] = x1
    v = (_rms(x1, gffn_ref[...]) * (1.0 + m[4:5]) + m[3:4]).astype(BF16)
    v_ref[0] = v
    logits_t = _dot_t(rw_ref[...], v)
    gates_t = _route(logits_t, rb_ref[...], n_experts)
    pad = jnp.zeros((LANES - n_experts, gates_t.shape[1]), F32)
    gates = jnp.concatenate([gates_t, pad], axis=0).T
    gate_ref[0] = gates
    cnt_ref[0, 0] = jnp.sum(jnp.where(gates > 0.0, 1.0, 0.0), axis=0, keepdims=True).astype(jnp.int32)


def _outproj(ya, hf, hb, bg, yc, yd, xs, modtab, w_out, gpost, gffn, rw_t, rbias, *, need_ctx):
    bsz, L, d = xs.shape
    n_experts = rw_t.shape[0]
    gw = ya.shape[-1]
    t0 = 0 if need_ctx else 1
    nt = L // ROW_TILE - t0
    rows_out = nt * ROW_TILE
    blk = lambda w: pl.BlockSpec((1, ROW_TILE, w), lambda b, t: (b, t + t0, 0))
    oblk = lambda w: pl.BlockSpec((1, ROW_TILE, w), lambda b, t: (b, t, 0))
    return pl.pallas_call(
        functools.partial(_outproj_kernel, n_experts=n_experts),
        grid=(bsz, nt),
        in_specs=[blk(gw)] * 6 + [blk(d),
                  pl.BlockSpec((1, 1, 6, d), lambda b, t: (b, jnp.minimum(t + t0, 1), 0, 0)),
                  _full(w_out), _full(gpost), _full(gffn), _full(rw_t), _full(rbias)],
        out_specs=[oblk(d), oblk(d), oblk(LANES),
                   pl.BlockSpec((1, 1, 1, LANES), lambda b, t: (b, t, 0, 0))],
        out_shape=[jax.ShapeDtypeStruct((bsz, rows_out, d), F32), jax.ShapeDtypeStruct((bsz, rows_out, d), BF16),
                   jax.ShapeDtypeStruct((bsz, rows_out, LANES), F32),
                   jax.ShapeDtypeStruct((bsz, nt, 1, LANES), jnp.int32)],
        compiler_params=_params("arbitrary", "arbitrary"),
    )(ya, hf, hb, bg, yc, yd, xs, modtab, w_out, gpost, gffn, rw_t, rbias)


SLOT_ALIGN = 16
EXPERT_BLOCK = 1024
SLOT_CHUNK = 512
SHORT_BITS = 2
CODE_BASE = 64.0


def _swiglu(v, wg, wu, wd):
    hg = _dot(v, wg.astype(BF16))
    hu = _dot(v, wu.astype(BF16))
    return hg * jax.nn.sigmoid(hg) * hu, wd.astype(BF16)


def _slot_cap(n_experts):
    rows = ROW_TILE * TOP_K + n_experts * (SLOT_ALIGN - 1)
    return -(-rows // SLOT_CHUNK) * SLOT_CHUNK


def _slot_codes(gates, off_row, ltri):
    chosen = gates > 0.0
    rank = _dot(ltri, jnp.where(chosen, 1.0, 0.0).astype(BF16))
    code = jnp.where(chosen, off_row.astype(F32) + rank + 1.0, 0.0)
    hi = jnp.floor(code * (1.0 / CODE_BASE))
    return hi.astype(BF16), (code - CODE_BASE * hi).astype(BF16)


def _slot_expert_onehot(off_row, len_row, first, rows):
    r = first + lax.broadcasted_iota(jnp.int32, (rows, LANES), 0)
    inside = jnp.where(r >= off_row, jnp.where(r < off_row + len_row, 1.0, 0.0), 0.0)
    return inside.astype(BF16)


def _row_digits(first, shape, axis):
    code = (first + 1 + lax.broadcasted_iota(jnp.int32, shape, axis)).astype(F32)
    hi = jnp.floor(code * (1.0 / CODE_BASE))
    return hi, code - CODE_BASE * hi


def _slot_dma(src, dst, src_off, dst_off, n_units, sem, start):
    def piece(first, size):
        cp = pltpu.make_async_copy(
            src.at[pl.ds(pl.multiple_of(src_off + first, SLOT_ALIGN), size)],
            dst.at[pl.ds(pl.multiple_of(dst_off + first, SLOT_ALIGN), size)], sem)
        if start:
            cp.start()
        else:
            cp.wait()

    for bit in range(SHORT_BITS):
        pl.when(((n_units >> bit) & 1) == 1)(
            functools.partial(piece, (n_units & ((1 << bit) - 1)) * SLOT_ALIGN, SLOT_ALIGN << bit))

    long_rows = SLOT_ALIGN << SHORT_BITS
    short_rows = (n_units & ((1 << SHORT_BITS) - 1)) * SLOT_ALIGN

    def long_piece(j, carry):
        piece(short_rows + j * long_rows, long_rows)
        return carry
    lax.fori_loop(0, n_units >> SHORT_BITS, long_piece, 0)


def _wait_rows(src, dst, n_units, sem):
    for bit in range((min(src.shape[0], dst.shape[0]) // SLOT_ALIGN).bit_length()):
        size = SLOT_ALIGN << bit

        @pl.when(((n_units >> bit) & 1) == 1)
        def _piece():
            pltpu.make_async_copy(src.at[pl.ds(0, size)], dst.at[pl.ds(0, size)], sem).wait()


def _dispatch_kernel(pos_ref, off_ref, len_ref, tot_ref, tpos_ref, tlen_ref,
                     v_ref, g_ref, offv_ref, lenv_ref, ltri_ref, xs_ref,
                     buf_s, zero_s, sem, *, n_experts):
    i = pl.program_id(0)
    nt = pl.num_programs(0)
    cur = i % 2
    cap = buf_s.shape[1]

    @pl.when(i == 0)
    def _zero():
        zero_s[...] = jnp.zeros_like(zero_s)

    g = g_ref[...]
    off_row, len_row = offv_ref[0], lenv_ref[0]
    hi, lo = _slot_codes(g, off_row, ltri_ref[...])
    g_hi = g.astype(BF16)
    g_lo = (g - g_hi.astype(F32)).astype(BF16)
    src = jnp.concatenate([v_ref[...], g_hi, g_lo], axis=1)
    for c in range(cap // SLOT_CHUNK):
        first = c * SLOT_CHUNK
        oh = _slot_expert_onehot(off_row, len_row, first, SLOT_CHUNK)
        rh, rl = _row_digits(first, (SLOT_CHUNK, 1), 0)
        pick = jnp.where(_dot_t(oh, hi) == rh, jnp.where(_dot_t(oh, lo) == rl, 1.0, 0.0), 0.0)
        buf_s[cur, first:first + SLOT_CHUNK, :] = _dot(pick.astype(BF16), src).astype(BF16)

    def slots(tile, which, start):
        def body(e, carry):
            k = tile * n_experts + e
            _slot_dma(buf_s.at[which], xs_ref, off_ref[k], pos_ref[k], len_ref[k], sem.at[0], start)
            return carry
        lax.fori_loop(0, n_experts, body, 0)

    def tails(start):
        def body(e, carry):
            _slot_dma(zero_s, xs_ref, 0, tpos_ref[e], tlen_ref[e], sem.at[0], start)
            return carry
        lax.fori_loop(0, n_experts, body, 0)

    @pl.when(i > 0)
    def _drain_previous():
        _wait_rows(buf_s.at[1 - cur], xs_ref, tot_ref[i - 1], sem.at[0])

    slots(i, cur, True)

    @pl.when(i == nt - 1)
    def _last():
        tails(True)
        _wait_rows(buf_s.at[cur], xs_ref, tot_ref[i], sem.at[0])
        tails(False)


def _dispatch(v2, gates2, plan, ltri, *, n_experts, n_rows):
    t, d = v2.shape
    nt = t // ROW_TILE
    cap = _slot_cap(n_experts)
    width = d + 2 * LANES
    grid_spec = pltpu.PrefetchScalarGridSpec(
        num_scalar_prefetch=6, grid=(nt,),
        in_specs=[pl.BlockSpec((ROW_TILE, d), lambda i, *_: (i, 0)),
                  pl.BlockSpec((ROW_TILE, LANES), lambda i, *_: (i, 0)),
                  pl.BlockSpec((1, 1, LANES), lambda i, *_: (i, 0, 0)),
                  pl.BlockSpec((1, 1, LANES), lambda i, *_: (i, 0, 0)),
                  pl.BlockSpec(ltri.shape, lambda i, *_: (0, 0))],
        out_specs=pl.BlockSpec(memory_space=pl.ANY),
        scratch_shapes=[pltpu.VMEM((2, cap, width), BF16),
                        pltpu.VMEM((EXPERT_BLOCK, width), BF16),
                        pltpu.SemaphoreType.DMA((1,))])
    return pl.pallas_call(
        functools.partial(_dispatch_kernel, n_experts=n_experts),
        grid_spec=grid_spec,
        out_shape=jax.ShapeDtypeStruct((n_rows, width), BF16),
        compiler_params=_params("arbitrary"),
    )(plan["pos"], plan["off"], plan["len"], plan["tile_len"], plan["tail_pos"], plan["tail_len"],
      v2, gates2, plan["off_v"], plan["len_v"], ltri)


def _expert_kernel(be_ref, na_ref, x_ref, wg_ref, wu_ref, wd_ref, y_ref, wg_s, wu_s, wd_s):
    b = pl.program_id(0)
    e = be_ref[b]

    @pl.when(jnp.logical_or(b == 0, e != be_ref[jnp.maximum(b - 1, 0)]))
    def _new_expert():
        wg_s[...] = wg_ref[0, 0].astype(BF16)
        wu_s[...] = wu_ref[0, 0].astype(BF16)
        wd_s[...] = wd_ref[0, 0].astype(BF16)

    @pl.when(b < na_ref[0])
    def _active():
        d = wg_s.shape[0]
        x = x_ref[:, 0:d]
        gates = x_ref[:, d:d + LANES].astype(F32) + x_ref[:, d + LANES:d + 2 * LANES].astype(F32)
        lane = lax.broadcasted_iota(jnp.int32, gates.shape, 1)
        gcol = jnp.sum(jnp.where(lane == e, gates, 0.0), axis=1, keepdims=True)
        hg = _dot(x, wg_s[...])
        h = hg * jax.nn.sigmoid(hg) * _dot(x, wu_s[...])
        y_ref[...] = (_dot(h.astype(BF16), wd_s[...]) * gcol).astype(BF16)


def _experts(xs_sorted, plan, wg, wu, wd, layer):
    n_rows, width = xs_sorted.shape
    _, _, d, de = wg.shape
    nb = n_rows // EXPERT_BLOCK
    rowmap = lambda b, be, na: (jnp.minimum(b, na[0] - 1), 0)
    grid_spec = pltpu.PrefetchScalarGridSpec(
        num_scalar_prefetch=2, grid=(nb,),
        in_specs=[pl.BlockSpec((EXPERT_BLOCK, width), rowmap),
                  pl.BlockSpec((1, 1, d, de), lambda b, be, na: (layer, be[b], 0, 0)),
                  pl.BlockSpec((1, 1, d, de), lambda b, be, na: (layer, be[b], 0, 0)),
                  pl.BlockSpec((1, 1, de, d), lambda b, be, na: (layer, be[b], 0, 0))],
        out_specs=pl.BlockSpec((EXPERT_BLOCK, d), rowmap),
        scratch_shapes=[pltpu.VMEM((d, de), BF16), pltpu.VMEM((d, de), BF16), pltpu.VMEM((de, d), BF16)])
    return pl.pallas_call(
        _expert_kernel, grid_spec=grid_spec,
        out_shape=jax.ShapeDtypeStruct((n_rows, d), BF16),
        compiler_params=_params("arbitrary"),
    )(plan["block_expert"], plan["n_active"], xs_sorted, wg, wu, wd)


def _combine_kernel(pos_ref, off_ref, len_ref, tot_ref,
                    ys_ref, v_ref, g_ref, offv_ref, lenv_ref, ltri_ref, x_ref, mod_ref,
                    sg_ref, su_ref, sd_ref, gpost_ref, o_ref, buf_s, sem, *, n_experts):
    i = pl.program_id(0)
    nt = pl.num_programs(0)
    cur = i % 2
    cap = buf_s.shape[1]

    def slots(tile, which, start):
        def body(e, carry):
            k = tile * n_experts + e
            _slot_dma(ys_ref, buf_s.at[which], pos_ref[k], off_ref[k], len_ref[k], sem.at[which], start)
            return carry
        lax.fori_loop(0, n_experts, body, 0)

    @pl.when(i == 0)
    def _first():
        buf_s[...] = jnp.zeros_like(buf_s)
        slots(0, 0, True)

    @pl.when(i + 1 < nt)
    def _prefetch():
        slots(i + 1, 1 - cur, True)

    v = v_ref[...]
    h, wd = _swiglu(v, sg_ref[...], su_ref[...], sd_ref[...])
    acc = _dot(h.astype(BF16), wd)
    off_row, len_row = offv_ref[0], lenv_ref[0]
    hi, lo = _slot_codes(g_ref[...], off_row, ltri_ref[...])

    _wait_rows(ys_ref, buf_s.at[cur], tot_ref[i], sem.at[cur])
    for c in range(cap // SLOT_CHUNK):
        first = c * SLOT_CHUNK
        oh = _slot_expert_onehot(off_row, len_row, first, SLOT_CHUNK)
        rh, rl = _row_digits(first, (1, SLOT_CHUNK), 1)
        pick = jnp.where(_dot_t(hi, oh) == rh, jnp.where(_dot_t(lo, oh) == rl, 1.0, 0.0), 0.0)
        acc = acc + _dot(pick.astype(BF16), buf_s[cur, first:first + SLOT_CHUNK, :])
    o_ref[0] = x_ref[0] + mod_ref[0, 0][5:6] * _rms(acc, gpost_ref[...])


def _combine(ys_sorted, v2, gates2, plan, ltri, xs, modtab, sg, su, sd, gpost, *, n_experts, has_ctx):
    bsz, L, d = xs.shape
    tpb = L // ROW_TILE
    nt = bsz * tpb
    cap = _slot_cap(n_experts)
    full = lambda a: pl.BlockSpec(a.shape, lambda i, *_: (0,) * a.ndim)
    grid_spec = pltpu.PrefetchScalarGridSpec(
        num_scalar_prefetch=4, grid=(nt,),
        in_specs=[pl.BlockSpec(memory_space=pl.ANY),
                  pl.BlockSpec((ROW_TILE, d), lambda i, *_: (i, 0)),
                  pl.BlockSpec((ROW_TILE, LANES), lambda i, *_: (i, 0)),
                  pl.BlockSpec((1, 1, LANES), lambda i, *_: (i, 0, 0)),
                  pl.BlockSpec((1, 1, LANES), lambda i, *_: (i, 0, 0)),
                  full(ltri),
                  pl.BlockSpec((1, ROW_TILE, d), lambda i, *_: (i // tpb, i % tpb, 0)),
                  pl.BlockSpec((1, 1, 6, d),
                               lambda i, *_: (i // tpb, jnp.minimum(i % tpb, 1) if has_ctx else 1, 0, 0)),
                  full(sg), full(su), full(sd), full(gpost)],
        out_specs=pl.BlockSpec((1, ROW_TILE, d), lambda i, *_: (i // tpb, i % tpb, 0)),
        scratch_shapes=[pltpu.VMEM((2, cap, d), BF16), pltpu.SemaphoreType.DMA((2,))])
    return pl.pallas_call(
        functools.partial(_combine_kernel, n_experts=n_experts),
        grid_spec=grid_spec,
        out_shape=jax.ShapeDtypeStruct((bsz, L, d), F32),
        compiler_params=_params("arbitrary"),
    )(plan["pos"], plan["off"], plan["len"], plan["tile_len"], ys_sorted, v2, gates2, plan["off_v"], plan["len_v"], ltri,
      xs, modtab, sg, su, sd, gpost)


def _moe_plan(counts, n_experts, n_blocks):
    a = (counts + (SLOT_ALIGN - 1)) // SLOT_ALIGN * SLOT_ALIGN
    rows = jnp.sum(a, axis=0)
    region = (rows + (EXPERT_BLOCK - 1)) // EXPERT_BLOCK * EXPERT_BLOCK
    region_end = jnp.cumsum(region)
    region_start = region_end - region
    pos = region_start[None, :] + jnp.cumsum(a, axis=0) - a
    off = jnp.cumsum(a, axis=1) - a
    first_row = jnp.arange(n_blocks, dtype=jnp.int32) * EXPERT_BLOCK
    block_expert = jnp.minimum(jnp.sum(region_end[None, :n_experts] <= first_row[:, None], axis=1), n_experts - 1)
    flat = lambda t: t[:, :n_experts].reshape(-1).astype(jnp.int32)
    nt = counts.shape[0]
    return {
        "pos": flat(pos), "off": flat(off), "len": flat(a // SLOT_ALIGN),
        "tile_len": (jnp.sum(a, axis=1) // SLOT_ALIGN).astype(jnp.int32),
        "tail_pos": (region_start + rows)[:n_experts].astype(jnp.int32),
        "tail_len": ((region - rows) // SLOT_ALIGN)[:n_experts].astype(jnp.int32),
        "off_v": off.reshape(nt, 1, LANES).astype(jnp.int32),
        "len_v": a.reshape(nt, 1, LANES).astype(jnp.int32),
        "block_expert": block_expert.astype(jnp.int32),
        "n_active": (region_end[-1:] // EXPERT_BLOCK).astype(jnp.int32),
    }


def _moe(v, gates, counts, xs, modtab, wg, wu, wd, layer, sg, su, sd, gpost, *, has_ctx):
    bsz, L, d = xs.shape
    n_experts = wg.shape[1]
    t = bsz * L
    nt = t // ROW_TILE
    worst = t * TOP_K + nt * n_experts * (SLOT_ALIGN - 1) + n_experts * (EXPERT_BLOCK - 1)
    n_blocks = -(-worst // EXPERT_BLOCK)
    plan = _moe_plan(counts.reshape(nt, LANES), n_experts, n_blocks)
    ltri = jnp.tril(jnp.ones((ROW_TILE, ROW_TILE), F32), -1).astype(BF16)
    v2, gates2 = v.reshape(t, d), gates.reshape(t, LANES)
    xs_sorted = _dispatch(v2, gates2, plan, ltri, n_experts=n_experts, n_rows=n_blocks * EXPERT_BLOCK)
    ys_sorted = _experts(xs_sorted, plan, wg, wu, wd, layer)
    return _combine(ys_sorted, v2, gates2, plan, ltri, xs, modtab, sg, su, sd, gpost,
                    n_experts=n_experts, has_ctx=has_ctx)


def _tables(seq, ctx_len):
    def build(rot_dim, lead, slot):
        n = rot_dim // 4
        pos = jnp.arange(seq)
        row = (pos // GRID_W).astype(F32)
        col = (pos % GRID_W).astype(F32)
        inv = ROPE_THETA ** (-jnp.arange(n, dtype=F32) / n)
        ang = jnp.concatenate([row[:, None] * inv, col[:, None] * inv], axis=-1)
        cos, sin = jnp.cos(ang), jnp.sin(ang)
        zero = jnp.zeros_like(sin)
        tail = slot - lead - rot_dim
        one_l, zero_l = jnp.ones((seq, lead), F32), jnp.zeros((seq, lead), F32)
        one_t, zero_t = jnp.ones((seq, tail), F32), jnp.zeros((seq, tail), F32)
        c = jnp.concatenate([one_l, cos, cos, one_t], axis=1)
        s_lo = jnp.concatenate([zero_l, -sin, zero, zero_t], axis=1)
        s_hi = jnp.concatenate([zero_l, zero, sin, zero_t], axis=1)
        ctx_c = jnp.ones((ctx_len, slot), F32)
        ctx_s = jnp.zeros((ctx_len, slot), F32)
        return [jnp.concatenate([ctx_c, c], axis=0), jnp.concatenate([ctx_s, s_lo], axis=0),
                jnp.concatenate([ctx_s, s_hi], axis=0)]

    hd = [jnp.concatenate([t, t], axis=1) for t in build(HEAD_DIM, 0, HEAD_DIM)]
    mla = build(QK_ROPE, QK_NOPE, LANES)
    return hd + mla


def _dup_heads(w, n_heads):
    d = w.shape[0]
    wh = w.reshape(d, n_heads, 1, HEAD_DIM)
    return jnp.broadcast_to(wh, (d, n_heads, 2, HEAD_DIM)).reshape(d, n_heads * 2 * HEAD_DIM)


def kernel(x, c, ctx, c_ctx, w_mod, b_mod, g_pre_mix, g_post_mix, g_pre_ffn, g_post_ffn, w_in, a_q_norm, a_k_norm, conv_w, conv_b, lru_wa, lru_ba, lru_wi, lru_bi, lru_lambda, c_sink, d_q_norm, d_w_uq, d_kv_norm, d_w_ukv, w_out, router_w, router_bias, w_gate, w_up, w_down, sh_gate, sh_up, sh_down):
    bsz, seq, d = x.shape
    ctx_len = ctx.shape[1]
    depth = w_mod.shape[0]
    gw = d // 4
    lru_w = conv_w.shape[-1]
    q_lora = d_q_norm.shape[-1]
    kv_lora = d_kv_norm.shape[-1]
    n_experts = router_w.shape[-1]
    a_heads = gw // HEAD_DIM
    a_kv = (w_in.shape[-1] - (2 * gw + 2 * lru_w + q_lora + kv_lora + QK_ROPE)) // (4 * HEAD_DIM)
    mla_scale = (QK_NOPE + QK_ROPE) ** -0.5
    v_dim = gw // D_HEADS
    assert ctx_len == ROW_TILE and a_kv == 2 and a_heads == 4 and n_experts <= LANES

    cond = jnp.zeros((16, d), F32).at[:bsz].set(c).at[bsz].set(c_ctx)
    mod = _modulation(cond, w_mod, b_mod).reshape(depth, 16, 6, d)
    tabs = _tables(seq, ctx_len)
    seg = jnp.kron(jnp.eye(a_heads, dtype=F32), jnp.ones((HEAD_DIM, HEAD_DIM), F32)).astype(BF16)

    xs = jnp.concatenate([ctx, x], axis=1)
    for l in range(depth):
        modtab = jnp.stack([jnp.broadcast_to(mod[l, bsz], (bsz, 6, d)), mod[l, :bsz]], axis=1)

        offs = [0]
        for wdt in (gw, a_kv * HEAD_DIM, a_kv * HEAD_DIM, lru_w, lru_w, gw, a_kv * HEAD_DIM, a_kv * HEAD_DIM,
                    q_lora, kv_lora, QK_ROPE):
            offs.append(offs[-1] + wdt)
        col = lambda i: w_in[l][:, offs[i]:offs[i + 1]]
        zeros = lambda n: jnp.zeros((d, n), F32)
        kr_slot = jnp.concatenate([zeros(QK_NOPE), col(10), zeros(LANES - QK_NOPE - QK_ROPE)], axis=1)
        w_in_p = jnp.concatenate(
            [col(0), _dup_heads(col(1), a_kv), _dup_heads(col(2), a_kv), col(3), col(4),
             col(5), _dup_heads(col(6), a_kv), _dup_heads(col(7), a_kv),
             col(8), zeros(2 * LANES - q_lora), col(9)] + [kr_slot] * D_HEADS, axis=1).astype(BF16)

        qk = QK_NOPE + QK_ROPE
        wuq = d_w_uq[l].reshape(q_lora, D_HEADS, qk)
        wuq = jnp.pad(wuq, ((0, 2 * LANES - q_lora), (0, 0), (0, LANES - qk))).reshape(2 * LANES, D_HEADS * LANES)
        wukv = d_w_ukv[l].reshape(kv_lora, D_HEADS, QK_NOPE + v_dim)
        wk = jnp.pad(wukv[:, :, :QK_NOPE], ((0, 0), (0, 0), (0, LANES - QK_NOPE))).reshape(kv_lora, D_HEADS * LANES)
        wv = wukv[:, :, QK_NOPE:].reshape(kv_lora, D_HEADS * v_dim)
        wukv_p = jnp.concatenate([wk, wv], axis=1).astype(BF16)
        dqn = jnp.pad(d_q_norm[l], (0, 2 * LANES - q_lora)).reshape(1, 2 * LANES)

        qa, ka, va, bx, bg, qc, kc, vc, qd, kd, vd = _inproj(
            xs, modtab, g_pre_mix[l].reshape(1, d), w_in_p, seg, tabs,
            jnp.tile(a_q_norm[l], a_heads).reshape(1, gw), jnp.tile(a_k_norm[l], 2 * a_kv).reshape(1, gw),
            dqn, wuq.astype(BF16), d_kv_norm[l].reshape(1, kv_lora), wukv_p,
            q_lora=q_lora, mla_scale=mla_scale)

        need_ctx = l < depth - 1
        ya = _attention(qa, ka, va, split_q=True, ctx_len=ctx_len, need_ctx=need_ctx)
        yd = _attention(qd, kd, vd, split_q=False, ctx_len=ctx_len, need_ctx=need_ctx)
        yc = _window_attention(qc, kc, vc, c_sink[l], ctx_len=ctx_len, need_ctx=need_ctx)

        blocks = lru_wa.shape[2]
        bdiag = lambda wts: jnp.stack([jax.scipy.linalg.block_diag(*[wts[dd, h] for h in range(blocks)])
                                       for dd in range(2)]).astype(BF16)
        hf, hb = _lru(bx, conv_w[l], conv_b[l].reshape(1, lru_w), bdiag(lru_wa[l]), lru_ba[l].reshape(2, 1, lru_w),
                      bdiag(lru_wi[l]), lru_bi[l].reshape(2, 1, lru_w), lru_lambda[l].reshape(2, 1, lru_w),
                      ctx_len=ctx_len)

        rw_t = router_w[l].T.astype(BF16)
        xs_mid, v, gates, counts = _outproj(ya, hf, hb, bg, yc, yd, xs, modtab, w_out[l].astype(BF16),
                                            g_post_mix[l].reshape(1, d), g_pre_ffn[l].reshape(1, d),
                                            rw_t, router_bias[l].reshape(n_experts, 1), need_ctx=need_ctx)
        xs = _moe(v, gates, counts, xs_mid, modtab, w_gate, w_up, w_down, l,
                  sh_gate[l], sh_up[l], sh_down[l], g_post_ffn[l].reshape(1, d), has_ctx=need_ctx)
    return xs
```

```python
import functools

import jax
import jax.numpy as jnp
from jax import lax
from jax.experimental import pallas as pl
from jax.experimental.pallas import tpu as pltpu

F32 = jnp.float32
BF16 = jnp.bfloat16

GRID_W = 64
HEAD_DIM = 64
ROPE_THETA = 10000.0
NORM_EPS = 1e-6
WINDOW = 128
Q_BLOCK = 128
D_HEADS = 4
QK_NOPE = 64
QK_ROPE = 32
LRU_C = 8.0
CONV_W = 4
TOP_K = 8
N_EXPERT_GROUPS = 8
TOPK_GROUPS = 4
ROUTED_SCALE = 2.5

LANES = 128
VMEM_LIMIT = 56 * 1024 * 1024
ROW_TILE = 256
INPROJ_ROWS = 768

NEG_INF = float("-inf")


def _params(*sem):
    return pltpu.CompilerParams(dimension_semantics=sem, vmem_limit_bytes=VMEM_LIMIT)


def _rms(x, gain):
    return x * lax.rsqrt(jnp.mean(x * x, axis=-1, keepdims=True) + NORM_EPS) * gain


def _dot(a, b):
    return jnp.dot(a, b, preferred_element_type=F32)


def _dot_t(a, b):
    return lax.dot_general(a, b, (((1,), (1,)), ((), ())), preferred_element_type=F32)


def _full(a):
    return pl.BlockSpec(a.shape, lambda *_: (0,) * a.ndim)


def _mod_kernel(c_ref, w_ref, b_ref, o_ref):
    a = c_ref[...]
    a = a * jax.nn.sigmoid(a)
    o_ref[0] = _dot(a.astype(BF16), w_ref[0].astype(BF16)) + b_ref[0]


def _modulation(cond, w_mod, b_mod):
    depth, d, n = w_mod.shape
    tn = n // 4
    rows = cond.shape[0]
    return pl.pallas_call(
        _mod_kernel,
        grid=(depth, n // tn),
        in_specs=[pl.BlockSpec((rows, d), lambda l, j: (0, 0)),
                  pl.BlockSpec((1, d, tn), lambda l, j: (l, 0, j)),
                  pl.BlockSpec((1, 1, tn), lambda l, j: (l, 0, j))],
        out_specs=pl.BlockSpec((1, rows, tn), lambda l, j: (l, 0, j)),
        out_shape=jax.ShapeDtypeStruct((depth, rows, n), F32),
        compiler_params=_params("arbitrary", "arbitrary"),
    )(cond, w_mod, b_mod.reshape(depth, 1, n))


def _rope(t, cos, sin_lo, sin_hi, half):
    w = t.shape[-1]
    return t * cos + pltpu.roll(t, w - half, 1) * sin_lo + pltpu.roll(t, half, 1) * sin_hi


def _inproj_kernel(x_ref, mod_ref, gpre_ref, w_ref, seg_ref,
                   cos_ref, sinl_ref, sinh_ref, cosd_ref, sindl_ref, sindh_ref,
                   aqn_ref, akn_ref, dqn_ref, wuq_ref, dkvn_ref, wukv_ref,
                   qa_ref, ka_ref, va_ref, bx_ref, bg_ref, qc_ref, kc_ref, vc_ref,
                   qd_ref, kd_ref, vd_ref, *, q_lora, mla_scale, ctx_len):
    x = x_ref[0]
    rows = x.shape[0]
    mods = mod_ref[0]
    is_ctx = pl.program_id(1) * rows + lax.broadcasted_iota(jnp.int32, (rows, 1), 0) < ctx_len
    shift = jnp.where(is_ctx, mods[0, 0:1], mods[1, 0:1])
    scale = jnp.where(is_ctx, mods[0, 1:2], mods[1, 1:2])
    u = (_rms(x, gpre_ref[...]) * (1.0 + scale) + shift).astype(BF16)
    proj = _dot(u, w_ref[...])

    seg = seg_ref[...]
    cos = jnp.concatenate([cos_ref[...]] * 2, axis=1)
    sin_lo = jnp.concatenate([sinl_ref[...]] * 2, axis=1)
    sin_hi = jnp.concatenate([sinh_ref[...]] * 2, axis=1)
    cosd = jnp.concatenate([cosd_ref[...]] * 4, axis=1)
    sind_lo = jnp.concatenate([sindl_ref[...]] * 4, axis=1)
    sind_hi = jnp.concatenate([sindh_ref[...]] * 4, axis=1)

    def head_rms(t, gain):
        sq = t * t
        hi = sq.astype(BF16)
        lo = (sq - hi.astype(F32)).astype(BF16)
        ms = (_dot(hi, seg) + _dot(lo, seg)) * (1.0 / HEAD_DIM)
        return t * lax.rsqrt(ms + NORM_EPS) * gain

    def rope_hd(t):
        return _rope(t, cos, sin_lo, sin_hi, HEAD_DIM // 2)

    def rope_r(t):
        return _rope(t, cosd, sind_lo, sind_hi, QK_ROPE // 2)

    sc_hd = HEAD_DIM ** -0.5
    qa_ref[0] = (rope_hd(head_rms(proj[:, 0:256], aqn_ref[...])) * sc_hd).astype(BF16)
    ka_ref[0] = rope_hd(head_rms(proj[:, 256:512], akn_ref[...])).astype(BF16)
    va_ref[0] = proj[:, 512:768].astype(BF16)
    bx_ref[0] = proj[:, 768:1024]
    bg_ref[0] = proj[:, 1024:1280]
    qc_ref[0] = (rope_hd(proj[:, 1280:1536]) * sc_hd).astype(BF16)
    kc_ref[0] = rope_hd(proj[:, 1536:1792]).astype(BF16)
    vc_ref[0] = proj[:, 1792:2048].astype(BF16)
    cq = proj[:, 2048:2304]
    cq = cq * lax.rsqrt(jnp.sum(cq * cq, axis=-1, keepdims=True) * (1.0 / q_lora) + NORM_EPS) * dqn_ref[...]
    qd = _dot(cq.astype(BF16), wuq_ref[...])
    qd_ref[0] = (rope_r(qd) * mla_scale).astype(BF16)
    ckv = _rms(proj[:, 2304:2432], dkvn_ref[...])
    kv = _dot(ckv.astype(BF16), wukv_ref[...])
    kd_ref[0] = (kv[:, 0:512] + rope_r(proj[:, 2432:2944])).astype(BF16)
    vd_ref[0] = kv[:, 512:768].astype(BF16)


def _inproj(xs, modtab, gpre, w_in_p, seg, tabs, aqn, akn, dqn, wuq, dkvn, wukv, *, q_lora, mla_scale, ctx_len):
    bsz, L, d = xs.shape
    rows = INPROJ_ROWS
    nt = L // rows
    rowblk = lambda w: pl.BlockSpec((1, rows, w), lambda b, t: (b, t, 0))
    tab = lambda a: pl.BlockSpec((rows, a.shape[1]), lambda b, t: (t, 0))
    out_w = [(256, BF16)] * 3 + [(256, F32)] * 2 + [(256, BF16)] * 3 + [(512, BF16), (512, BF16), (256, BF16)]
    return pl.pallas_call(
        functools.partial(_inproj_kernel, q_lora=q_lora, mla_scale=mla_scale, ctx_len=ctx_len),
        grid=(bsz, nt),
        in_specs=[rowblk(d),
                  pl.BlockSpec((1, 2, 6, d), lambda b, t: (b, 0, 0, 0)),
                  _full(gpre), _full(w_in_p), _full(seg)] + [tab(a) for a in tabs]
                 + [_full(a) for a in (aqn, akn, dqn, wuq, dkvn, wukv)],
        out_specs=[rowblk(w) for w, _ in out_w],
        out_shape=[jax.ShapeDtypeStruct((bsz, L, w), dt) for w, dt in out_w],
        compiler_params=_params("arbitrary", "arbitrary"),
    )(xs, modtab, gpre, w_in_p, seg, *tabs, aqn, akn, dqn, wuq, dkvn, wukv)


def _attn_kernel(q_ref, k_ref, v_ref, o_ref, *, split_q, ctx_len, k_len, first_tile):
    lane = lax.broadcasted_iota(jnp.int32, (1, LANES), 1)
    low = lane < HEAD_DIM

    def run(n_keys):
        for g in range(v_ref.shape[-1] // LANES):
            outs = []
            for h in range(2):
                if split_q:
                    lanes = slice(g * LANES, (g + 1) * LANES)
                    keep = low if h == 0 else jnp.logical_not(low)
                    q = q_ref[0, :, lanes]
                    q = jnp.where(keep, q, jnp.zeros_like(q))
                else:
                    lanes = slice((2 * g + h) * LANES, (2 * g + h + 1) * LANES)
                    q = q_ref[0, :, lanes]
                s = _dot_t(q, k_ref[0, 0:n_keys, lanes])
                p = jnp.exp(s - jnp.max(s, axis=-1, keepdims=True))
                den = jnp.sum(p, axis=-1, keepdims=True)
                outs.append(_dot(p.astype(BF16), v_ref[0, 0:n_keys, g * LANES:(g + 1) * LANES]) / den)
            o_ref[0, :, g * LANES:(g + 1) * LANES] = jnp.where(low, outs[0], outs[1]).astype(o_ref.dtype)

    if first_tile > 0:
        run(k_len)
        return
    t = pl.program_id(1)

    @pl.when(t == 0)
    def _ctx():
        run(ctx_len)

    @pl.when(t > 0)
    def _lat():
        run(k_len)


def _attention(q, k, v, *, split_q, ctx_len, need_ctx):
    bsz, L, qw = q.shape
    vw = v.shape[-1]
    tq = ROW_TILE
    assert ctx_len == tq
    t0 = 0 if need_ctx else 1
    return pl.pallas_call(
        functools.partial(_attn_kernel, split_q=split_q, ctx_len=ctx_len, k_len=L, first_tile=t0),
        grid=(bsz, L // tq - t0),
        in_specs=[pl.BlockSpec((1, tq, qw), lambda b, t: (b, t + t0, 0)),
                  pl.BlockSpec((1, L, qw), lambda b, t: (b, 0, 0)),
                  pl.BlockSpec((1, L, vw), lambda b, t: (b, 0, 0))],
        out_specs=pl.BlockSpec((1, tq, vw), lambda b, t: (b, t + t0, 0)),
        out_shape=jax.ShapeDtypeStruct((bsz, L, vw), BF16),
        compiler_params=_params("arbitrary", "arbitrary"),
    )(q, k, v)


def _winattn_kernel(sink_ref, q_ref, k_ref, v_ref, o_ref, *, ctx_len, seq, first_block):
    t = pl.program_id(1) + first_block
    lane = lax.broadcasted_iota(jnp.int32, (1, LANES), 1)
    low = lane < HEAD_DIM
    ctx_blocks = ctx_len // Q_BLOCK
    win = 3 * Q_BLOCK
    n_groups = q_ref.shape[-1] // LANES

    def finish(parts, sink):
        m = sink
        for s, _ in parts:
            m = jnp.maximum(m, jnp.max(s, axis=-1, keepdims=True))
        den = jnp.exp(sink - m)
        acc = None
        for s, vv in parts:
            p = jnp.exp(s - m)
            den = den + jnp.sum(p, axis=-1, keepdims=True)
            o = _dot(p.astype(BF16), vv)
            acc = o if acc is None else acc + o
        return acc / den

    def heads(fn):
        for g in range(n_groups):
            lanes = slice(g * LANES, (g + 1) * LANES)
            outs = []
            for h in range(2):
                keep = low if h == 0 else jnp.logical_not(low)
                q = q_ref[0, :, lanes]
                q = jnp.where(keep, q, jnp.zeros_like(q))
                outs.append(fn(q, sink_ref[2 * g + h], lanes))
            o_ref[0, :, lanes] = jnp.where(low, outs[0], outs[1]).astype(o_ref.dtype)

    def ctx_queries():
        def one(q, sink, lanes):
            return finish([(_dot_t(q, k_ref[0, 0:ctx_len, lanes]), v_ref[0, 0:ctx_len, lanes])], sink)
        heads(one)

    def latent_queries():
        n = t - ctx_blocks
        start = jnp.clip((n - 1) * Q_BLOCK, 0, seq - win)
        off = pl.multiple_of(ctx_len + start, Q_BLOCK)
        qpos = n * Q_BLOCK + lax.broadcasted_iota(jnp.int32, (Q_BLOCK, win), 0)
        kpos = start + lax.broadcasted_iota(jnp.int32, (Q_BLOCK, win), 1)
        allowed = jnp.abs(kpos - qpos) <= WINDOW

        def one(q, sink, lanes):
            s_ctx = _dot_t(q, k_ref[0, 0:ctx_len, lanes])
            s_win = jnp.where(allowed, _dot_t(q, k_ref[0, pl.ds(off, win), lanes]), NEG_INF)
            return finish([(s_ctx, v_ref[0, 0:ctx_len, lanes]), (s_win, v_ref[0, pl.ds(off, win), lanes])], sink)
        heads(one)

    if first_block >= ctx_blocks:
        latent_queries()
    else:
        pl.when(t < ctx_blocks)(ctx_queries)
        pl.when(t >= ctx_blocks)(latent_queries)


def _window_attention(q, k, v, sink, *, ctx_len, need_ctx):
    bsz, L, w = q.shape
    b0 = 0 if need_ctx else ctx_len // Q_BLOCK
    return pl.pallas_call(
        functools.partial(_winattn_kernel, ctx_len=ctx_len, seq=L - ctx_len, first_block=b0),
        grid=(bsz, L // Q_BLOCK - b0),
        in_specs=[pl.BlockSpec(memory_space=pltpu.SMEM),
                  pl.BlockSpec((1, Q_BLOCK, w), lambda b, t: (b, t + b0, 0)),
                  pl.BlockSpec((1, L, w), lambda b, t: (b, 0, 0)),
                  pl.BlockSpec((1, L, w), lambda b, t: (b, 0, 0))],
        out_specs=pl.BlockSpec((1, Q_BLOCK, w), lambda b, t: (b, t + b0, 0)),
        out_shape=jax.ShapeDtypeStruct((bsz, L, w), BF16),
        compiler_params=_params("arbitrary", "arbitrary"),
    )(sink, q, k, v)


def _lru_kernel(xf_ref, xfp_ref, xfn_ref, xb_ref, xbp_ref, xbn_ref,
                cw_ref, cb_ref, wa_ref, ba_ref, wi_ref, bi_ref, lam_ref,
                hf_ref, hb_ref,
                af_s, bf_s, ab_s, bb_s, of_s, ob_s, sf_s, sb_s, *, n_chunks):
    i = pl.program_id(0)
    bsz, tc, w = xf_ref.shape
    rows = bsz * tc
    ti = lax.broadcasted_iota(jnp.int32, (bsz, tc, w), 1)

    def coeffs(x_ref, prev_ref, next_ref, chunk, d, a_s, b_s):
        has_prev = (chunk >= 2).astype(F32)
        has_next = jnp.logical_and(chunk >= 1, chunk <= n_chunks - 2).astype(F32)
        x = x_ref[...]
        p1 = prev_ref[:, 7:8, :] * has_prev
        n0 = next_ref[:, 0:1, :] * has_next
        n1 = next_ref[:, 1:2, :] * has_next
        x2 = x.reshape(rows, w)
        xm1 = jnp.where(ti == 0, p1, pltpu.roll(x2, 1, 0).reshape(bsz, tc, w))
        xp1 = jnp.where(ti == tc - 1, n0, pltpu.roll(x2, rows - 1, 0).reshape(bsz, tc, w))
        xp2 = jnp.where(ti == tc - 1, n1,
                        jnp.where(ti == tc - 2, n0, pltpu.roll(x2, rows - 2, 0).reshape(bsz, tc, w)))
        cw = cw_ref[...]
        xc = (cb_ref[...] + xm1 * cw[0:1] + x * cw[1:2] + xp1 * cw[2:3] + xp2 * cw[3:4]).reshape(rows, w)
        xcb = xc.astype(BF16)
        r = jax.nn.sigmoid(_dot(xcb, wa_ref[d]) + ba_ref[d])
        ig = jax.nn.sigmoid(_dot(xcb, wi_ref[d]) + bi_ref[d])
        log_a = (-LRU_C) * r * jax.nn.softplus(-lam_ref[d])
        a = jnp.exp(log_a)
        b = jnp.sqrt(1.0 - a * a) * (ig * xc)
        for j in range(w // LANES):
            a_s[j] = a[:, j * LANES:(j + 1) * LANES]
            b_s[j] = b[:, j * LANES:(j + 1) * LANES]

    chunk_b = jnp.where(i == 0, 0, n_chunks - i)
    coeffs(xf_ref, xfp_ref, xfn_ref, i, 0, af_s, bf_s)
    coeffs(xb_ref, xbp_ref, xbn_ref, chunk_b, 1, ab_s, bb_s)

    @pl.when(i == 0)
    def _init():
        sf_s[...] = jnp.zeros_like(sf_s)
        sb_s[...] = jnp.zeros_like(sb_s)

    nl = w // LANES

    def step(t, carry):
        fwd = pl.ds(t, bsz, stride=tc)
        bwd = pl.ds(tc - 1 - t, bsz, stride=tc)
        out = []
        for j in range(nl):
            hf = af_s[j, fwd, :] * carry[j] + bf_s[j, fwd, :]
            hb = ab_s[j, bwd, :] * carry[nl + j] + bb_s[j, bwd, :]
            of_s[j, fwd, :] = hf
            ob_s[j, bwd, :] = hb
            out.append((hf, hb))
        return tuple(o[0] for o in out) + tuple(o[1] for o in out)

    init = tuple(sf_s[j] for j in range(nl)) + tuple(sb_s[j] for j in range(nl))
    fin = lax.fori_loop(0, tc, step, init, unroll=8)
    for j in range(nl):
        sf_s[j] = fin[j]
        sb_s[j] = fin[nl + j]
    hf_ref[...] = jnp.concatenate([of_s[j] for j in range(nl)], axis=1).reshape(bsz, tc, w)
    hb_ref[...] = jnp.concatenate([ob_s[j] for j in range(nl)], axis=1).reshape(bsz, tc, w)


def _lru(bx, conv_w, conv_b, wa, ba, wi, bi, lam, *, ctx_len):
    bsz, L, w = bx.shape
    tc = ctx_len
    nc = L // tc
    hb_blocks = tc // 8
    last8 = L // 8 - 1

    def fchunk(i):
        return i

    def bchunk(i):
        return jnp.where(i == 0, 0, nc - i)

    def cur(cf):
        return pl.BlockSpec((bsz, tc, w), lambda i: (0, cf(i), 0))

    def prev(cf):
        return pl.BlockSpec((bsz, 8, w), lambda i: (0, jnp.maximum(cf(i) * hb_blocks - 1, 0), 0))

    def nxt(cf):
        return pl.BlockSpec((bsz, 8, w), lambda i: (0, jnp.minimum((cf(i) + 1) * hb_blocks, last8), 0))

    small = [conv_w, conv_b, wa, ba, wi, bi, lam]
    nl = w // LANES
    scr = [pltpu.VMEM((nl, bsz * tc, LANES), F32)] * 6 + [pltpu.VMEM((nl, bsz, LANES), F32)] * 2
    return pl.pallas_call(
        functools.partial(_lru_kernel, n_chunks=nc),
        grid=(nc,),
        in_specs=[cur(fchunk), prev(fchunk), nxt(fchunk), cur(bchunk), prev(bchunk), nxt(bchunk)]
                 + [_full(a) for a in small],
        out_specs=[cur(fchunk), cur(bchunk)],
        out_shape=[jax.ShapeDtypeStruct((bsz, L, w), F32)] * 2,
        scratch_shapes=scr,
        compiler_params=_params("arbitrary"),
    )(bx, bx, bx, bx, bx, bx, *small)


def _route(logits_t, bias, n_experts):
    per = n_experts // N_EXPERT_GROUPS
    tm = logits_t.shape[-1]
    scores = jax.nn.sigmoid(logits_t).reshape(N_EXPERT_GROUPS, per, tm)
    sel = scores + bias.reshape(N_EXPERT_GROUPS, per, 1)
    shape = sel.shape
    gi = lax.broadcasted_iota(jnp.int32, shape, 0)
    mi = lax.broadcasted_iota(jnp.int32, shape, 1)
    ei = gi * per + mi
    m1 = jnp.max(sel, axis=1, keepdims=True)
    first = jnp.min(jnp.where(sel == m1, mi, per), axis=1, keepdims=True)
    m2 = jnp.max(jnp.where(mi == first, NEG_INF, sel), axis=1, keepdims=True)
    gscore = m1 + m2
    gidx = lax.broadcasted_iota(jnp.int32, gscore.shape, 0)
    gmask = jnp.zeros(gscore.shape, F32)
    for _ in range(TOPK_GROUPS):
        m = jnp.max(gscore, axis=0, keepdims=True)
        pick = jnp.min(jnp.where(gscore == m, gidx, N_EXPERT_GROUPS), axis=0, keepdims=True)
        hit = gidx == pick
        gmask = jnp.where(hit, 1.0, gmask)
        gscore = jnp.where(hit, NEG_INF, gscore)
    cand = jnp.where(gmask > 0.0, sel, NEG_INF)
    chosen = jnp.zeros(shape, F32)
    for _ in range(TOP_K):
        m = jnp.max(jnp.max(cand, axis=1, keepdims=True), axis=0, keepdims=True)
        pick = jnp.where(cand == m, ei, n_experts)
        pick = jnp.min(jnp.min(pick, axis=1, keepdims=True), axis=0, keepdims=True)
        hit = ei == pick
        chosen = jnp.where(hit, 1.0, chosen)
        cand = jnp.where(hit, NEG_INF, cand)
    wsel = jnp.where(chosen > 0.0, scores, 0.0)
    den = jnp.sum(jnp.sum(wsel, axis=1, keepdims=True), axis=0, keepdims=True)
    return (wsel / den * ROUTED_SCALE).reshape(n_experts, tm)


def _outproj_kernel(ya_ref, hf_ref, hb_ref, bg_ref, yc_ref, yd_ref, x_ref, mod_ref,
                    wout_ref, gpost_ref, gffn_ref, rw_ref, rb_ref,
                    xo_ref, v_ref, gate_ref, cnt_ref, *, n_experts):
    gw = ya_ref.shape[-1]
    m = mod_ref[0, 0]
    yb = ((hf_ref[0] + hb_ref[0]) * jax.nn.gelu(bg_ref[0])).astype(BF16)
    y = (_dot(ya_ref[0], wout_ref[0:gw, :]) + _dot(yb, wout_ref[gw:2 * gw, :])
         + _dot(yc_ref[0], wout_ref[2 * gw:3 * gw, :]) + _dot(yd_ref[0], wout_ref[3 * gw:4 * gw, :]))
    x1 = x_ref[0] + m[2:3] * _rms(y, gpost_ref[...])
    xo_ref[0] = x1
    v = (_rms(x1, gffn_ref[...]) * (1.0 + m[4:5]) + m[3:4]).astype(BF16)
    v_ref[0] = v
    logits_t = _dot_t(rw_ref[...], v)
    gates_t = _route(logits_t, rb_ref[...], n_experts)
    pad = jnp.zeros((LANES - n_experts, gates_t.shape[1]), F32)
    gates = jnp.concatenate([gates_t, pad], axis=0).T
    gate_ref[0] = gates
    cnt_ref[0, 0] = jnp.sum(jnp.where(gates > 0.0, 1.0, 0.0), axis=0, keepdims=True).astype(jnp.int32)


def _outproj(ya, hf, hb, bg, yc, yd, xs, modtab, w_out, gpost, gffn, rw_t, rbias, *, need_ctx):
    bsz, L, d = xs.shape
    n_experts = rw_t.shape[0]
    gw = ya.shape[-1]
    t0 = 0 if need_ctx else 1
    nt = L // ROW_TILE - t0
    rows_out = nt * ROW_TILE
    blk = lambda w: pl.BlockSpec((1, ROW_TILE, w), lambda b, t: (b, t + t0, 0))
    oblk = lambda w: pl.BlockSpec((1, ROW_TILE, w), lambda b, t: (b, t, 0))
    return pl.pallas_call(
        functools.partial(_outproj_kernel, n_experts=n_experts),
        grid=(bsz, nt),
        in_specs=[blk(gw)] * 6 + [blk(d),
                  pl.BlockSpec((1, 1, 6, d), lambda b, t: (b, jnp.minimum(t + t0, 1), 0, 0)),
                  _full(w_out), _full(gpost), _full(gffn), _full(rw_t), _full(rbias)],
        out_specs=[oblk(d), oblk(d), oblk(LANES),
                   pl.BlockSpec((1, 1, 1, LANES), lambda b, t: (b, t, 0, 0))],
        out_shape=[jax.ShapeDtypeStruct((bsz, rows_out, d), F32), jax.ShapeDtypeStruct((bsz, rows_out, d), BF16),
                   jax.ShapeDtypeStruct((bsz, rows_out, LANES), F32),
                   jax.ShapeDtypeStruct((bsz, nt, 1, LANES), jnp.int32)],
        compiler_params=_params("arbitrary", "arbitrary"),
    )(ya, hf, hb, bg, yc, yd, xs, modtab, w_out, gpost, gffn, rw_t, rbias)


SLOT_ALIGN = 16
EXPERT_BLOCK = 512
SLOT_CHUNK = 512
SHORT_BITS = 2
CODE_BASE = 64.0


def _swiglu(v, wg, wu, wd):
    hg = _dot(v, wg.astype(BF16))
    hu = _dot(v, wu.astype(BF16))
    return hg * jax.nn.sigmoid(hg) * hu, wd.astype(BF16)


def _slot_cap(n_experts):
    rows = ROW_TILE * TOP_K + n_experts * (SLOT_ALIGN - 1)
    return -(-rows // SLOT_CHUNK) * SLOT_CHUNK


def _slot_codes(gates, off_row, ltri):
    chosen = gates > 0.0
    rank = _dot(ltri, jnp.where(chosen, 1.0, 0.0).astype(BF16))
    code = jnp.where(chosen, off_row.astype(F32) + rank + 1.0, 0.0)
    hi = jnp.floor(code * (1.0 / CODE_BASE))
    return hi.astype(BF16), (code - CODE_BASE * hi).astype(BF16)


def _slot_expert_onehot(off_row, len_row, first, rows):
    r = first + lax.broadcasted_iota(jnp.int32, (rows, LANES), 0)
    inside = jnp.where(r >= off_row, jnp.where(r < off_row + len_row, 1.0, 0.0), 0.0)
    return inside.astype(BF16)


def _row_digits(first, shape, axis):
    code = (first + 1 + lax.broadcasted_iota(jnp.int32, shape, axis)).astype(F32)
    hi = jnp.floor(code * (1.0 / CODE_BASE))
    return hi, code - CODE_BASE * hi


def _slot_dma(src, dst, src_off, dst_off, n_units, sem, start):
    def piece(first, size):
        cp = pltpu.make_async_copy(
            src.at[pl.ds(pl.multiple_of(src_off + first, SLOT_ALIGN), size)],
            dst.at[pl.ds(pl.multiple_of(dst_off + first, SLOT_ALIGN), size)], sem)
        if start:
            cp.start()
        else:
            cp.wait()

    for bit in range(SHORT_BITS):
        pl.when(((n_units >> bit) & 1) == 1)(
            functools.partial(piece, (n_units & ((1 << bit) - 1)) * SLOT_ALIGN, SLOT_ALIGN << bit))

    long_rows = SLOT_ALIGN << SHORT_BITS
    short_rows = (n_units & ((1 << SHORT_BITS) - 1)) * SLOT_ALIGN

    def long_piece(j, carry):
        piece(short_rows + j * long_rows, long_rows)
        return carry
    lax.fori_loop(0, n_units >> SHORT_BITS, long_piece, 0)


def _wait_rows(src, dst, n_units, sem):
    for bit in range((min(src.shape[0], dst.shape[0]) // SLOT_ALIGN).bit_length()):
        size = SLOT_ALIGN << bit

        @pl.when(((n_units >> bit) & 1) == 1)
        def _piece():
            pltpu.make_async_copy(src.at[pl.ds(0, size)], dst.at[pl.ds(0, size)], sem).wait()


def _dispatch_kernel(pos_ref, off_ref, len_ref, tot_ref, tpos_ref, tlen_ref,
                     v_ref, g_ref, offv_ref, lenv_ref, ltri_ref, xs_ref,
                     buf_s, zero_s, sem, *, n_experts):
    i = pl.program_id(0)
    nt = pl.num_programs(0)
    cur = i % 2
    cap = buf_s.shape[1]

    @pl.when(i == 0)
    def _zero():
        zero_s[...] = jnp.zeros_like(zero_s)

    g = g_ref[...]
    off_row, len_row = offv_ref[0], lenv_ref[0]
    hi, lo = _slot_codes(g, off_row, ltri_ref[...])
    g_hi = g.astype(BF16)
    g_lo = (g - g_hi.astype(F32)).astype(BF16)
    src = jnp.concatenate([v_ref[...], g_hi, g_lo], axis=1)
    used = tot_ref[i] * SLOT_ALIGN
    for c in range(cap // SLOT_CHUNK):
        first = c * SLOT_CHUNK

        @pl.when(first < used)
        def _chunk():
            oh = _slot_expert_onehot(off_row, len_row, first, SLOT_CHUNK)
            rh, rl = _row_digits(first, (SLOT_CHUNK, 1), 0)
            pick = jnp.where(_dot_t(oh, hi) == rh, jnp.where(_dot_t(oh, lo) == rl, 1.0, 0.0), 0.0)
            buf_s[cur, first:first + SLOT_CHUNK, :] = _dot(pick.astype(BF16), src).astype(BF16)

    def slots(tile, which, start):
        def body(e, carry):
            k = tile * n_experts + e
            _slot_dma(buf_s.at[which], xs_ref, off_ref[k], pos_ref[k], len_ref[k], sem.at[0], start)
            return carry
        lax.fori_loop(0, n_experts, body, 0)

    def tails(start):
        def body(e, carry):
            _slot_dma(zero_s, xs_ref, 0, tpos_ref[e], tlen_ref[e], sem.at[0], start)
            return carry
        lax.fori_loop(0, n_experts, body, 0)

    @pl.when(i > 0)
    def _drain_previous():
        _wait_rows(buf_s.at[1 - cur], xs_ref, tot_ref[i - 1], sem.at[0])

    slots(i, cur, True)

    @pl.when(i == nt - 1)
    def _last():
        tails(True)
        _wait_rows(buf_s.at[cur], xs_ref, tot_ref[i], sem.at[0])
        tails(False)


def _dispatch(v2, gates2, plan, ltri, *, n_experts, n_rows):
    t, d = v2.shape
    nt = t // ROW_TILE
    cap = _slot_cap(n_experts)
    width = d + 2 * LANES
    grid_spec = pltpu.PrefetchScalarGridSpec(
        num_scalar_prefetch=6, grid=(nt,),
        in_specs=[pl.BlockSpec((ROW_TILE, d), lambda i, *_: (i, 0)),
                  pl.BlockSpec((ROW_TILE, LANES), lambda i, *_: (i, 0)),
                  pl.BlockSpec((1, 1, LANES), lambda i, *_: (i, 0, 0)),
                  pl.BlockSpec((1, 1, LANES), lambda i, *_: (i, 0, 0)),
                  pl.BlockSpec(ltri.shape, lambda i, *_: (0, 0))],
        out_specs=pl.BlockSpec(memory_space=pl.ANY),
        scratch_shapes=[pltpu.VMEM((2, cap, width), BF16),
                        pltpu.VMEM((EXPERT_BLOCK, width), BF16),
                        pltpu.SemaphoreType.DMA((1,))])
    return pl.pallas_call(
        functools.partial(_dispatch_kernel, n_experts=n_experts),
        grid_spec=grid_spec,
        out_shape=jax.ShapeDtypeStruct((n_rows, width), BF16),
        compiler_params=_params("arbitrary"),
    )(plan["pos"], plan["off"], plan["len"], plan["tile_len"], plan["tail_pos"], plan["tail_len"],
      v2, gates2, plan["off_v"], plan["len_v"], ltri)


def _expert_kernel(be_ref, na_ref, x_ref, wg_ref, wu_ref, wd_ref, y_ref, wg_s, wu_s, wd_s):
    b = pl.program_id(0)
    e = be_ref[b]

    @pl.when(jnp.logical_or(b == 0, e != be_ref[jnp.maximum(b - 1, 0)]))
    def _new_expert():
        wg_s[...] = wg_ref[0, 0].astype(BF16)
        wu_s[...] = wu_ref[0, 0].astype(BF16)
        wd_s[...] = wd_ref[0, 0].astype(BF16)

    @pl.when(b < na_ref[0])
    def _active():
        d = wg_s.shape[0]
        x = x_ref[:, 0:d]
        gates = x_ref[:, d:d + LANES].astype(F32) + x_ref[:, d + LANES:d + 2 * LANES].astype(F32)
        lane = lax.broadcasted_iota(jnp.int32, gates.shape, 1)
        gcol = jnp.sum(jnp.where(lane == e, gates, 0.0), axis=1, keepdims=True)
        hg = _dot(x, wg_s[...])
        h = hg * jax.nn.sigmoid(hg) * _dot(x, wu_s[...])
        y_ref[...] = (_dot(h.astype(BF16), wd_s[...]) * gcol).astype(BF16)


def _experts(xs_sorted, plan, wg, wu, wd, layer):
    n_rows, width = xs_sorted.shape
    _, _, d, de = wg.shape
    nb = n_rows // EXPERT_BLOCK
    rowmap = lambda b, be, na: (jnp.minimum(b, na[0] - 1), 0)
    grid_spec = pltpu.PrefetchScalarGridSpec(
        num_scalar_prefetch=2, grid=(nb,),
        in_specs=[pl.BlockSpec((EXPERT_BLOCK, width), rowmap),
                  pl.BlockSpec((1, 1, d, de), lambda b, be, na: (layer, be[b], 0, 0)),
                  pl.BlockSpec((1, 1, d, de), lambda b, be, na: (layer, be[b], 0, 0)),
                  pl.BlockSpec((1, 1, de, d), lambda b, be, na: (layer, be[b], 0, 0))],
        out_specs=pl.BlockSpec((EXPERT_BLOCK, d), rowmap),
        scratch_shapes=[pltpu.VMEM((d, de), BF16), pltpu.VMEM((d, de), BF16), pltpu.VMEM((de, d), BF16)])
    return pl.pallas_call(
        _expert_kernel, grid_spec=grid_spec,
        out_shape=jax.ShapeDtypeStruct((n_rows, d), BF16),
        compiler_params=_params("arbitrary"),
    )(plan["block_expert"], plan["n_active"], xs_sorted, wg, wu, wd)


def _combine_kernel(pos_ref, off_ref, len_ref, tot_ref,
                    ys_ref, v_ref, g_ref, offv_ref, lenv_ref, ltri_ref, x_ref, mod_ref,
                    sg_ref, su_ref, sd_ref, gpost_ref, o_ref, buf_s, acc_s, sem, *, n_experts):
    i = pl.program_id(0)
    nt = pl.num_programs(0)
    cur = i % 2
    cap = buf_s.shape[1]

    def slots(tile, which, start):
        def body(e, carry):
            k = tile * n_experts + e
            _slot_dma(ys_ref, buf_s.at[which], pos_ref[k], off_ref[k], len_ref[k], sem.at[which], start)
            return carry
        lax.fori_loop(0, n_experts, body, 0)

    @pl.when(i == 0)
    def _first():
        buf_s[...] = jnp.zeros_like(buf_s)
        slots(0, 0, True)

    @pl.when(i + 1 < nt)
    def _prefetch():
        slots(i + 1, 1 - cur, True)

    v = v_ref[...]
    h, wd = _swiglu(v, sg_ref[...], su_ref[...], sd_ref[...])
    acc = _dot(h.astype(BF16), wd)
    off_row, len_row = offv_ref[0], lenv_ref[0]
    hi, lo = _slot_codes(g_ref[...], off_row, ltri_ref[...])

    acc_s[...] = acc
    _wait_rows(ys_ref, buf_s.at[cur], tot_ref[i], sem.at[cur])
    used = tot_ref[i] * SLOT_ALIGN
    for c in range(cap // SLOT_CHUNK):
        first = c * SLOT_CHUNK

        @pl.when(first < used)
        def _chunk():
            oh = _slot_expert_onehot(off_row, len_row, first, SLOT_CHUNK)
            rh, rl = _row_digits(first, (1, SLOT_CHUNK), 1)
            pick = jnp.where(_dot_t(hi, oh) == rh, jnp.where(_dot_t(lo, oh) == rl, 1.0, 0.0), 0.0)
            acc_s[...] += _dot(pick.astype(BF16), buf_s[cur, first:first + SLOT_CHUNK, :])
    o_ref[0] = x_ref[0] + mod_ref[0, 0][5:6] * _rms(acc_s[...], gpost_ref[...])


def _combine(ys_sorted, v2, gates2, plan, ltri, xs, modtab, sg, su, sd, gpost, *, n_experts, has_ctx):
    bsz, L, d = xs.shape
    tpb = L // ROW_TILE
    nt = bsz * tpb
    cap = _slot_cap(n_experts)
    full = lambda a: pl.BlockSpec(a.shape, lambda i, *_: (0,) * a.ndim)
    grid_spec = pltpu.PrefetchScalarGridSpec(
        num_scalar_prefetch=4, grid=(nt,),
        in_specs=[pl.BlockSpec(memory_space=pl.ANY),
                  pl.BlockSpec((ROW_TILE, d), lambda i, *_: (i, 0)),
                  pl.BlockSpec((ROW_TILE, LANES), lambda i, *_: (i, 0)),
                  pl.BlockSpec((1, 1, LANES), lambda i, *_: (i, 0, 0)),
                  pl.BlockSpec((1, 1, LANES), lambda i, *_: (i, 0, 0)),
                  full(ltri),
                  pl.BlockSpec((1, ROW_TILE, d), lambda i, *_: (i // tpb, i % tpb, 0)),
                  pl.BlockSpec((1, 1, 6, d),
                               lambda i, *_: (i // tpb, jnp.minimum(i % tpb, 1) if has_ctx else 1, 0, 0)),
                  full(sg), full(su), full(sd), full(gpost)],
        out_specs=pl.BlockSpec((1, ROW_TILE, d), lambda i, *_: (i // tpb, i % tpb, 0)),
        scratch_shapes=[pltpu.VMEM((2, cap, d), BF16), pltpu.VMEM((ROW_TILE, d), F32),
                        pltpu.SemaphoreType.DMA((2,))])
    return pl.pallas_call(
        functools.partial(_combine_kernel, n_experts=n_experts),
        grid_spec=grid_spec,
        out_shape=jax.ShapeDtypeStruct((bsz, L, d), F32),
        compiler_params=_params("arbitrary"),
    )(plan["pos"], plan["off"], plan["len"], plan["tile_len"], ys_sorted, v2, gates2, plan["off_v"], plan["len_v"], ltri,
      xs, modtab, sg, su, sd, gpost)


def _moe_plan(counts, n_experts, n_blocks):
    a = (counts + (SLOT_ALIGN - 1)) // SLOT_ALIGN * SLOT_ALIGN
    rows = jnp.sum(a, axis=0)
    region = (rows + (EXPERT_BLOCK - 1)) // EXPERT_BLOCK * EXPERT_BLOCK
    region_end = jnp.cumsum(region)
    region_start = region_end - region
    pos = region_start[None, :] + jnp.cumsum(a, axis=0) - a
    off = jnp.cumsum(a, axis=1) - a
    first_row = jnp.arange(n_blocks, dtype=jnp.int32) * EXPERT_BLOCK
    block_expert = jnp.minimum(jnp.sum(region_end[None, :n_experts] <= first_row[:, None], axis=1), n_experts - 1)
    flat = lambda t: t[:, :n_experts].reshape(-1).astype(jnp.int32)
    nt = counts.shape[0]
    return {
        "pos": flat(pos), "off": flat(off), "len": flat(a // SLOT_ALIGN),
        "tile_len": (jnp.sum(a, axis=1) // SLOT_ALIGN).astype(jnp.int32),
        "tail_pos": (region_start + rows)[:n_experts].astype(jnp.int32),
        "tail_len": ((region - rows) // SLOT_ALIGN)[:n_experts].astype(jnp.int32),
        "off_v": off.reshape(nt, 1, LANES).astype(jnp.int32),
        "len_v": a.reshape(nt, 1, LANES).astype(jnp.int32),
        "block_expert": block_expert.astype(jnp.int32),
        "n_active": (region_end[-1:] // EXPERT_BLOCK).astype(jnp.int32),
    }


def _moe(v, gates, counts, xs, modtab, wg, wu, wd, layer, sg, su, sd, gpost, *, has_ctx):
    bsz, L, d = xs.shape
    n_experts = wg.shape[1]
    t = bsz * L
    nt = t // ROW_TILE
    worst = t * TOP_K + nt * n_experts * (SLOT_ALIGN - 1) + n_experts * (EXPERT_BLOCK - 1)
    n_blocks = -(-worst // EXPERT_BLOCK)
    plan = _moe_plan(counts.reshape(nt, LANES), n_experts, n_blocks)
    ltri = jnp.tril(jnp.ones((ROW_TILE, ROW_TILE), F32), -1).astype(BF16)
    v2, gates2 = v.reshape(t, d), gates.reshape(t, LANES)
    xs_sorted = _dispatch(v2, gates2, plan, ltri, n_experts=n_experts, n_rows=n_blocks * EXPERT_BLOCK)
    ys_sorted = _experts(xs_sorted, plan, wg, wu, wd, layer)
    return _combine(ys_sorted, v2, gates2, plan, ltri, xs, modtab, sg, su, sd, gpost,
                    n_experts=n_experts, has_ctx=has_ctx)


def _tables(seq, ctx_len):
    def build(rot_dim, lead, slot):
        n = rot_dim // 4
        pos = jnp.arange(seq)
        row = (pos // GRID_W).astype(F32)
        col = (pos % GRID_W).astype(F32)
        inv = ROPE_THETA ** (-jnp.arange(n, dtype=F32) / n)
        ang = jnp.concatenate([row[:, None] * inv, col[:, None] * inv], axis=-1)
        cos, sin = jnp.cos(ang), jnp.sin(ang)
        zero = jnp.zeros_like(sin)
        tail = slot - lead - rot_dim
        one_l, zero_l = jnp.ones((seq, lead), F32), jnp.zeros((seq, lead), F32)
        one_t, zero_t = jnp.ones((seq, tail), F32), jnp.zeros((seq, tail), F32)
        c = jnp.concatenate([one_l, cos, cos, one_t], axis=1)
        s_lo = jnp.concatenate([zero_l, -sin, zero, zero_t], axis=1)
        s_hi = jnp.concatenate([zero_l, zero, sin, zero_t], axis=1)
        ctx_c = jnp.ones((ctx_len, slot), F32)
        ctx_s = jnp.zeros((ctx_len, slot), F32)
        return [jnp.concatenate([ctx_c, c], axis=0), jnp.concatenate([ctx_s, s_lo], axis=0),
                jnp.concatenate([ctx_s, s_hi], axis=0)]

    hd = [jnp.concatenate([t, t], axis=1) for t in build(HEAD_DIM, 0, HEAD_DIM)]
    mla = build(QK_ROPE, QK_NOPE, LANES)
    return hd + mla


def _dup_heads(w, n_heads):
    d = w.shape[0]
    wh = w.reshape(d, n_heads, 1, HEAD_DIM)
    return jnp.broadcast_to(wh, (d, n_heads, 2, HEAD_DIM)).reshape(d, n_heads * 2 * HEAD_DIM)


def kernel(x, c, ctx, c_ctx, w_mod, b_mod, g_pre_mix, g_post_mix, g_pre_ffn, g_post_ffn, w_in, a_q_norm, a_k_norm, conv_w, conv_b, lru_wa, lru_ba, lru_wi, lru_bi, lru_lambda, c_sink, d_q_norm, d_w_uq, d_kv_norm, d_w_ukv, w_out, router_w, router_bias, w_gate, w_up, w_down, sh_gate, sh_up, sh_down):
    bsz, seq, d = x.shape
    ctx_len = ctx.shape[1]
    depth = w_mod.shape[0]
    gw = d // 4
    lru_w = conv_w.shape[-1]
    q_lora = d_q_norm.shape[-1]
    kv_lora = d_kv_norm.shape[-1]
    n_experts = router_w.shape[-1]
    a_heads = gw // HEAD_DIM
    a_kv = (w_in.shape[-1] - (2 * gw + 2 * lru_w + q_lora + kv_lora + QK_ROPE)) // (4 * HEAD_DIM)
    mla_scale = (QK_NOPE + QK_ROPE) ** -0.5
    v_dim = gw // D_HEADS
    assert ctx_len == ROW_TILE and a_kv == 2 and a_heads == 4 and n_experts <= LANES

    cond = jnp.zeros((16, d), F32).at[:bsz].set(c).at[bsz].set(c_ctx)
    mod = _modulation(cond, w_mod, b_mod).reshape(depth, 16, 6, d)
    tabs = _tables(seq, ctx_len)
    seg = jnp.kron(jnp.eye(a_heads, dtype=F32), jnp.ones((HEAD_DIM, HEAD_DIM), F32)).astype(BF16)

    xs = jnp.concatenate([ctx, x], axis=1)
    for l in range(depth):
        modtab = jnp.stack([jnp.broadcast_to(mod[l, bsz], (bsz, 6, d)), mod[l, :bsz]], axis=1)

        offs = [0]
        for wdt in (gw, a_kv * HEAD_DIM, a_kv * HEAD_DIM, lru_w, lru_w, gw, a_kv * HEAD_DIM, a_kv * HEAD_DIM,
                    q_lora, kv_lora, QK_ROPE):
            offs.append(offs[-1] + wdt)
        col = lambda i: w_in[l][:, offs[i]:offs[i + 1]]
        zeros = lambda n: jnp.zeros((d, n), F32)
        kr_slot = jnp.concatenate([zeros(QK_NOPE), col(10), zeros(LANES - QK_NOPE - QK_ROPE)], axis=1)
        w_in_p = jnp.concatenate(
            [col(0), _dup_heads(col(1), a_kv), _dup_heads(col(2), a_kv), col(3), col(4),
             col(5), _dup_heads(col(6), a_kv), _dup_heads(col(7), a_kv),
             col(8), zeros(2 * LANES - q_lora), col(9)] + [kr_slot] * D_HEADS, axis=1).astype(BF16)

        qk = QK_NOPE + QK_ROPE
        wuq = d_w_uq[l].reshape(q_lora, D_HEADS, qk)
        wuq = jnp.pad(wuq, ((0, 2 * LANES - q_lora), (0, 0), (0, LANES - qk))).reshape(2 * LANES, D_HEADS * LANES)
        wukv = d_w_ukv[l].reshape(kv_lora, D_HEADS, QK_NOPE + v_dim)
        wk = jnp.pad(wukv[:, :, :QK_NOPE], ((0, 0), (0, 0), (0, LANES - QK_NOPE))).reshape(kv_lora, D_HEADS * LANES)
        wv = wukv[:, :, QK_NOPE:].reshape(kv_lora, D_HEADS * v_dim)
        wukv_p = jnp.concatenate([wk, wv], axis=1).astype(BF16)
        dqn = jnp.pad(d_q_norm[l], (0, 2 * LANES - q_lora)).reshape(1, 2 * LANES)

        qa, ka, va, bx, bg, qc, kc, vc, qd, kd, vd = _inproj(
            xs, modtab, g_pre_mix[l].reshape(1, d), w_in_p, seg, tabs,
            jnp.tile(a_q_norm[l], a_heads).reshape(1, gw), jnp.tile(a_k_norm[l], 2 * a_kv).reshape(1, gw),
            dqn, wuq.astype(BF16), d_kv_norm[l].reshape(1, kv_lora), wukv_p,
            q_lora=q_lora, mla_scale=mla_scale, ctx_len=ctx_len)

        need_ctx = l < depth - 1
        ya = _attention(qa, ka, va, split_q=True, ctx_len=ctx_len, need_ctx=need_ctx)
        yd = _attention(qd, kd, vd, split_q=False, ctx_len=ctx_len, need_ctx=need_ctx)
        yc = _window_attention(qc, kc, vc, c_sink[l], ctx_len=ctx_len, need_ctx=need_ctx)

        blocks = lru_wa.shape[2]
        bdiag = lambda wts: jnp.stack([jax.scipy.linalg.block_diag(*[wts[dd, h] for h in range(blocks)])
                                       for dd in range(2)]).astype(BF16)
        hf, hb = _lru(bx, conv_w[l], conv_b[l].reshape(1, lru_w), bdiag(lru_wa[l]), lru_ba[l].reshape(2, 1, lru_w),
                      bdiag(lru_wi[l]), lru_bi[l].reshape(2, 1, lru_w), lru_lambda[l].reshape(2, 1, lru_w),
                      ctx_len=ctx_len)

        rw_t = router_w[l].T.astype(BF16)
        xs_mid, v, gates, counts = _outproj(ya, hf, hb, bg, yc, yd, xs, modtab, w_out[l].astype(BF16),
                                            g_post_mix[l].reshape(1, d), g_pre_ffn[l].reshape(1, d),
                                            rw_t, router_bias[l].reshape(n_experts, 1), need_ctx=need_ctx)
        xs = _moe(v, gates, counts, xs_mid, modtab, w_gate, w_up, w_down, l,
                  sh_gate[l], sh_up[l], sh_down[l], g_post_ffn[l].reshape(1, d), has_ctx=need_ctx)
    return xs
```

```python
import functools

import jax
import jax.numpy as jnp
from jax import lax
from jax.experimental import pallas as pl
from jax.experimental.pallas import tpu as pltpu

F32 = jnp.float32
BF16 = jnp.bfloat16

GRID_W = 64
HEAD_DIM = 64
ROPE_THETA = 10000.0
NORM_EPS = 1e-6
WINDOW = 128
WIN_QUERIES = 256
D_HEADS = 4
QK_NOPE = 64
QK_ROPE = 32
LRU_C = 8.0
CONV_W = 4
TOP_K = 8
N_EXPERT_GROUPS = 8
TOPK_GROUPS = 4
ROUTED_SCALE = 2.5

LANES = 128
VMEM_LIMIT = 56 * 1024 * 1024
ROW_TILE = 256
INPROJ_ROWS = 768

NEG_INF = float("-inf")


def _params(*sem):
    return pltpu.CompilerParams(dimension_semantics=sem, vmem_limit_bytes=VMEM_LIMIT)


def _rms(x, gain):
    return x * lax.rsqrt(jnp.mean(x * x, axis=-1, keepdims=True) + NORM_EPS) * gain


def _dot(a, b):
    return jnp.dot(a, b, preferred_element_type=F32)


def _dot_t(a, b):
    return lax.dot_general(a, b, (((1,), (1,)), ((), ())), preferred_element_type=F32)


def _full(a):
    return pl.BlockSpec(a.shape, lambda *_: (0,) * a.ndim)


def _mod_kernel(c_ref, w_ref, b_ref, o_ref):
    a = c_ref[...]
    a = a * jax.nn.sigmoid(a)
    o_ref[0] = _dot(a.astype(BF16), w_ref[0].astype(BF16)) + b_ref[0]


def _modulation(cond, w_mod, b_mod):
    depth, d, n = w_mod.shape
    tn = n // 4
    rows = cond.shape[0]
    return pl.pallas_call(
        _mod_kernel,
        grid=(depth, n // tn),
        in_specs=[pl.BlockSpec((rows, d), lambda l, j: (0, 0)),
                  pl.BlockSpec((1, d, tn), lambda l, j: (l, 0, j)),
                  pl.BlockSpec((1, 1, tn), lambda l, j: (l, 0, j))],
        out_specs=pl.BlockSpec((1, rows, tn), lambda l, j: (l, 0, j)),
        out_shape=jax.ShapeDtypeStruct((depth, rows, n), F32),
        compiler_params=_params("arbitrary", "arbitrary"),
    )(cond, w_mod, b_mod.reshape(depth, 1, n))


def _rope(t, cos, sin_lo, sin_hi, half):
    w = t.shape[-1]
    return t * cos + pltpu.roll(t, w - half, 1) * sin_lo + pltpu.roll(t, half, 1) * sin_hi


def _inproj_kernel(x_ref, mod_ref, gpre_ref, w_ref, seg_ref,
                   cos_ref, sinl_ref, sinh_ref, cosd_ref, sindl_ref, sindh_ref,
                   aqn_ref, akn_ref, dqn_ref, wuq_ref, dkvn_ref, wukv_ref,
                   qa_ref, ka_ref, va_ref, bx_ref, bg_ref, qc_ref, kc_ref, vc_ref,
                   qd_ref, kd_ref, vd_ref, *, q_lora, mla_scale, ctx_len):
    x = x_ref[0]
    rows = x.shape[0]
    mods = mod_ref[0]
    is_ctx = pl.program_id(1) * rows + lax.broadcasted_iota(jnp.int32, (rows, 1), 0) < ctx_len
    shift = jnp.where(is_ctx, mods[0, 0:1], mods[1, 0:1])
    scale = jnp.where(is_ctx, mods[0, 1:2], mods[1, 1:2])
    u = (_rms(x, gpre_ref[...]) * (1.0 + scale) + shift).astype(BF16)
    proj = _dot(u, w_ref[...])

    seg = seg_ref[...]
    cos = jnp.concatenate([cos_ref[...]] * 2, axis=1)
    sin_lo = jnp.concatenate([sinl_ref[...]] * 2, axis=1)
    sin_hi = jnp.concatenate([sinh_ref[...]] * 2, axis=1)
    cosd = jnp.concatenate([cosd_ref[...]] * 4, axis=1)
    sind_lo = jnp.concatenate([sindl_ref[...]] * 4, axis=1)
    sind_hi = jnp.concatenate([sindh_ref[...]] * 4, axis=1)

    def head_rms(t, gain):
        sq = t * t
        hi = sq.astype(BF16)
        lo = (sq - hi.astype(F32)).astype(BF16)
        ms = (_dot(hi, seg) + _dot(lo, seg)) * (1.0 / HEAD_DIM)
        return t * lax.rsqrt(ms + NORM_EPS) * gain

    def rope_hd(t):
        return _rope(t, cos, sin_lo, sin_hi, HEAD_DIM // 2)

    def rope_r(t):
        return _rope(t, cosd, sind_lo, sind_hi, QK_ROPE // 2)

    sc_hd = HEAD_DIM ** -0.5
    qa_ref[0] = (rope_hd(head_rms(proj[:, 0:256], aqn_ref[...])) * sc_hd).astype(BF16)
    ka_ref[0] = rope_hd(head_rms(proj[:, 256:512], akn_ref[...])).astype(BF16)
    va_ref[0] = proj[:, 512:768].astype(BF16)
    bx_ref[0] = proj[:, 768:1024]
    bg_ref[0] = proj[:, 1024:1280]
    qc_ref[0] = (rope_hd(proj[:, 1280:1536]) * sc_hd).astype(BF16)
    kc_ref[0] = rope_hd(proj[:, 1536:1792]).astype(BF16)
    vc_ref[0] = proj[:, 1792:2048].astype(BF16)
    cq = proj[:, 2048:2304]
    cq = cq * lax.rsqrt(jnp.sum(cq * cq, axis=-1, keepdims=True) * (1.0 / q_lora) + NORM_EPS) * dqn_ref[...]
    qd = _dot(cq.astype(BF16), wuq_ref[...])
    qd_ref[0] = (rope_r(qd) * mla_scale).astype(BF16)
    ckv = _rms(proj[:, 2304:2432], dkvn_ref[...])
    kv = _dot(ckv.astype(BF16), wukv_ref[...])
    kd_ref[0] = (kv[:, 0:512] + rope_r(proj[:, 2432:2944])).astype(BF16)
    vd_ref[0] = kv[:, 512:768].astype(BF16)


def _inproj(xs, modtab, gpre, w_in_p, seg, tabs, aqn, akn, dqn, wuq, dkvn, wukv, *, q_lora, mla_scale, ctx_len):
    bsz, L, d = xs.shape
    rows = INPROJ_ROWS
    nt = L // rows
    rowblk = lambda w: pl.BlockSpec((1, rows, w), lambda b, t: (b, t, 0))
    tab = lambda a: pl.BlockSpec((rows, a.shape[1]), lambda b, t: (t, 0))
    out_w = [(256, BF16)] * 3 + [(256, F32)] * 2 + [(256, BF16)] * 3 + [(512, BF16), (512, BF16), (256, BF16)]
    return pl.pallas_call(
        functools.partial(_inproj_kernel, q_lora=q_lora, mla_scale=mla_scale, ctx_len=ctx_len),
        grid=(bsz, nt),
        in_specs=[rowblk(d),
                  pl.BlockSpec((1, 2, 6, d), lambda b, t: (b, 0, 0, 0)),
                  _full(gpre), _full(w_in_p), _full(seg)] + [tab(a) for a in tabs]
                 + [_full(a) for a in (aqn, akn, dqn, wuq, dkvn, wukv)],
        out_specs=[rowblk(w) for w, _ in out_w],
        out_shape=[jax.ShapeDtypeStruct((bsz, L, w), dt) for w, dt in out_w],
        compiler_params=_params("arbitrary", "arbitrary"),
    )(xs, modtab, gpre, w_in_p, seg, *tabs, aqn, akn, dqn, wuq, dkvn, wukv)


def _attn_kernel(q_ref, k_ref, v_ref, o_ref, *, split_q, ctx_len, k_len, first_tile):
    lane = lax.broadcasted_iota(jnp.int32, (1, LANES), 1)
    low = lane < HEAD_DIM

    def run(n_keys):
        for g in range(v_ref.shape[-1] // LANES):
            outs = []
            for h in range(2):
                if split_q:
                    lanes = slice(g * LANES, (g + 1) * LANES)
                    keep = low if h == 0 else jnp.logical_not(low)
                    q = q_ref[0, :, lanes]
                    q = jnp.where(keep, q, jnp.zeros_like(q))
                else:
                    lanes = slice((2 * g + h) * LANES, (2 * g + h + 1) * LANES)
                    q = q_ref[0, :, lanes]
                s = _dot_t(q, k_ref[0, 0:n_keys, lanes])
                p = jnp.exp(s - jnp.max(s, axis=-1, keepdims=True))
                den = jnp.sum(p, axis=-1, keepdims=True)
                outs.append(_dot(p.astype(BF16), v_ref[0, 0:n_keys, g * LANES:(g + 1) * LANES]) / den)
            o_ref[0, :, g * LANES:(g + 1) * LANES] = jnp.where(low, outs[0], outs[1]).astype(o_ref.dtype)

    if first_tile > 0:
        run(k_len)
        return
    t = pl.program_id(1)

    @pl.when(t == 0)
    def _ctx():
        run(ctx_len)

    @pl.when(t > 0)
    def _lat():
        run(k_len)


def _attention(q, k, v, *, split_q, ctx_len, need_ctx):
    bsz, L, qw = q.shape
    vw = v.shape[-1]
    tq = ROW_TILE
    assert ctx_len == tq
    t0 = 0 if need_ctx else 1
    return pl.pallas_call(
        functools.partial(_attn_kernel, split_q=split_q, ctx_len=ctx_len, k_len=L, first_tile=t0),
        grid=(bsz, L // tq - t0),
        in_specs=[pl.BlockSpec((1, tq, qw), lambda b, t: (b, t + t0, 0)),
                  pl.BlockSpec((1, L, qw), lambda b, t: (b, 0, 0)),
                  pl.BlockSpec((1, L, vw), lambda b, t: (b, 0, 0))],
        out_specs=pl.BlockSpec((1, tq, vw), lambda b, t: (b, t + t0, 0)),
        out_shape=jax.ShapeDtypeStruct((bsz, L, vw), BF16),
        compiler_params=_params("arbitrary", "arbitrary"),
    )(q, k, v)


def _winattn_kernel(sink_ref, q_ref, k_ref, v_ref, o_ref, *, ctx_len, seq, first_block):
    t = pl.program_id(1) + first_block
    lane = lax.broadcasted_iota(jnp.int32, (1, LANES), 1)
    low = lane < HEAD_DIM
    ctx_blocks = ctx_len // WIN_QUERIES
    win = WIN_QUERIES + 2 * WINDOW
    n_groups = q_ref.shape[-1] // LANES

    def finish(parts, sink):
        m = sink
        for s, _ in parts:
            m = jnp.maximum(m, jnp.max(s, axis=-1, keepdims=True))
        den = jnp.exp(sink - m)
        acc = None
        for s, vv in parts:
            p = jnp.exp(s - m)
            den = den + jnp.sum(p, axis=-1, keepdims=True)
            o = _dot(p.astype(BF16), vv)
            acc = o if acc is None else acc + o
        return acc / den

    def heads(fn):
        for g in range(n_groups):
            lanes = slice(g * LANES, (g + 1) * LANES)
            outs = []
            for h in range(2):
                keep = low if h == 0 else jnp.logical_not(low)
                q = q_ref[0, :, lanes]
                q = jnp.where(keep, q, jnp.zeros_like(q))
                outs.append(fn(q, sink_ref[2 * g + h], lanes))
            o_ref[0, :, lanes] = jnp.where(low, outs[0], outs[1]).astype(o_ref.dtype)

    def ctx_queries():
        def one(q, sink, lanes):
            return finish([(_dot_t(q, k_ref[0, 0:ctx_len, lanes]), v_ref[0, 0:ctx_len, lanes])], sink)
        heads(one)

    def latent_queries():
        n = t - ctx_blocks
        start = jnp.clip(n * WIN_QUERIES - WINDOW, 0, seq - win)
        off = pl.multiple_of(ctx_len + start, WINDOW)
        qpos = n * WIN_QUERIES + lax.broadcasted_iota(jnp.int32, (WIN_QUERIES, win), 0)
        kpos = start + lax.broadcasted_iota(jnp.int32, (WIN_QUERIES, win), 1)
        allowed = jnp.abs(kpos - qpos) <= WINDOW

        def one(q, sink, lanes):
            s_ctx = _dot_t(q, k_ref[0, 0:ctx_len, lanes])
            s_win = jnp.where(allowed, _dot_t(q, k_ref[0, pl.ds(off, win), lanes]), NEG_INF)
            return finish([(s_ctx, v_ref[0, 0:ctx_len, lanes]), (s_win, v_ref[0, pl.ds(off, win), lanes])], sink)
        heads(one)

    if first_block >= ctx_blocks:
        latent_queries()
    else:
        pl.when(t < ctx_blocks)(ctx_queries)
        pl.when(t >= ctx_blocks)(latent_queries)


def _window_attention(q, k, v, sink, *, ctx_len, need_ctx):
    bsz, L, w = q.shape
    b0 = 0 if need_ctx else ctx_len // WIN_QUERIES
    return pl.pallas_call(
        functools.partial(_winattn_kernel, ctx_len=ctx_len, seq=L - ctx_len, first_block=b0),
        grid=(bsz, L // WIN_QUERIES - b0),
        in_specs=[pl.BlockSpec(memory_space=pltpu.SMEM),
                  pl.BlockSpec((1, WIN_QUERIES, w), lambda b, t: (b, t + b0, 0)),
                  pl.BlockSpec((1, L, w), lambda b, t: (b, 0, 0)),
                  pl.BlockSpec((1, L, w), lambda b, t: (b, 0, 0))],
        out_specs=pl.BlockSpec((1, WIN_QUERIES, w), lambda b, t: (b, t + b0, 0)),
        out_shape=jax.ShapeDtypeStruct((bsz, L, w), BF16),
        compiler_params=_params("arbitrary", "arbitrary"),
    )(sink, q, k, v)


def _lru_kernel(xf_ref, xfp_ref, xfn_ref, xb_ref, xbp_ref, xbn_ref,
                cw_ref, cb_ref, wa_ref, ba_ref, wi_ref, bi_ref, lam_ref,
                hf_ref, hb_ref,
                af_s, bf_s, ab_s, bb_s, of_s, ob_s, sf_s, sb_s, *, n_chunks):
    i = pl.program_id(0)
    bsz, tc, w = xf_ref.shape
    rows = bsz * tc
    ti = lax.broadcasted_iota(jnp.int32, (bsz, tc, w), 1)

    def coeffs(x_ref, prev_ref, next_ref, chunk, d, a_s, b_s):
        has_prev = (chunk >= 2).astype(F32)
        has_next = jnp.logical_and(chunk >= 1, chunk <= n_chunks - 2).astype(F32)
        x = x_ref[...]
        p1 = prev_ref[:, 7:8, :] * has_prev
        n0 = next_ref[:, 0:1, :] * has_next
        n1 = next_ref[:, 1:2, :] * has_next
        x2 = x.reshape(rows, w)
        xm1 = jnp.where(ti == 0, p1, pltpu.roll(x2, 1, 0).reshape(bsz, tc, w))
        xp1 = jnp.where(ti == tc - 1, n0, pltpu.roll(x2, rows - 1, 0).reshape(bsz, tc, w))
        xp2 = jnp.where(ti == tc - 1, n1,
                        jnp.where(ti == tc - 2, n0, pltpu.roll(x2, rows - 2, 0).reshape(bsz, tc, w)))
        cw = cw_ref[...]
        xc = (cb_ref[...] + xm1 * cw[0:1] + x * cw[1:2] + xp1 * cw[2:3] + xp2 * cw[3:4]).reshape(rows, w)
        xcb = xc.astype(BF16)
        r = 0.5 * jnp.tanh(0.5 * (_dot(xcb, wa_ref[d]) + ba_ref[d])) + 0.5
        ig = 0.5 * jnp.tanh(0.5 * (_dot(xcb, wi_ref[d]) + bi_ref[d])) + 0.5
        log_a = (-LRU_C) * r * jax.nn.softplus(-lam_ref[d])
        a = jnp.exp(log_a)
        b = jnp.sqrt(1.0 - a * a) * (ig * xc)
        for j in range(w // LANES):
            a_s[j] = a[:, j * LANES:(j + 1) * LANES]
            b_s[j] = b[:, j * LANES:(j + 1) * LANES]

    chunk_b = jnp.where(i == 0, 0, n_chunks - i)
    coeffs(xf_ref, xfp_ref, xfn_ref, i, 0, af_s, bf_s)
    coeffs(xb_ref, xbp_ref, xbn_ref, chunk_b, 1, ab_s, bb_s)

    @pl.when(i == 0)
    def _init():
        sf_s[...] = jnp.zeros_like(sf_s)
        sb_s[...] = jnp.zeros_like(sb_s)

    nl = w // LANES

    def step(t, carry):
        fwd = pl.ds(t, bsz, stride=tc)
        bwd = pl.ds(tc - 1 - t, bsz, stride=tc)
        out = []
        for j in range(nl):
            hf = af_s[j, fwd, :] * carry[j] + bf_s[j, fwd, :]
            hb = ab_s[j, bwd, :] * carry[nl + j] + bb_s[j, bwd, :]
            of_s[j, fwd, :] = hf
            ob_s[j, bwd, :] = hb
            out.append((hf, hb))
        return tuple(o[0] for o in out) + tuple(o[1] for o in out)

    init = tuple(sf_s[j] for j in range(nl)) + tuple(sb_s[j] for j in range(nl))
    fin = lax.fori_loop(0, tc, step, init, unroll=8)
    for j in range(nl):
        sf_s[j] = fin[j]
        sb_s[j] = fin[nl + j]
    hf_ref[...] = jnp.concatenate([of_s[j] for j in range(nl)], axis=1).reshape(bsz, tc, w)
    hb_ref[...] = jnp.concatenate([ob_s[j] for j in range(nl)], axis=1).reshape(bsz, tc, w)


def _lru(bx, conv_w, conv_b, wa, ba, wi, bi, lam, *, ctx_len):
    bsz, L, w = bx.shape
    tc = ctx_len
    nc = L // tc
    hb_blocks = tc // 8
    last8 = L // 8 - 1

    def fchunk(i):
        return i

    def bchunk(i):
        return jnp.where(i == 0, 0, nc - i)

    def cur(cf):
        return pl.BlockSpec((bsz, tc, w), lambda i: (0, cf(i), 0))

    def prev(cf):
        return pl.BlockSpec((bsz, 8, w), lambda i: (0, jnp.maximum(cf(i) * hb_blocks - 1, 0), 0))

    def nxt(cf):
        return pl.BlockSpec((bsz, 8, w), lambda i: (0, jnp.minimum((cf(i) + 1) * hb_blocks, last8), 0))

    small = [conv_w, conv_b, wa, ba, wi, bi, lam]
    nl = w // LANES
    scr = [pltpu.VMEM((nl, bsz * tc, LANES), F32)] * 6 + [pltpu.VMEM((nl, bsz, LANES), F32)] * 2
    return pl.pallas_call(
        functools.partial(_lru_kernel, n_chunks=nc),
        grid=(nc,),
        in_specs=[cur(fchunk), prev(fchunk), nxt(fchunk), cur(bchunk), prev(bchunk), nxt(bchunk)]
                 + [_full(a) for a in small],
        out_specs=[cur(fchunk), cur(bchunk)],
        out_shape=[jax.ShapeDtypeStruct((bsz, L, w), F32)] * 2,
        scratch_shapes=scr,
        compiler_params=_params("arbitrary"),
    )(bx, bx, bx, bx, bx, bx, *small)


def _route(logits_t, bias, n_experts):
    per = n_experts // N_EXPERT_GROUPS
    tm = logits_t.shape[-1]
    scores = jax.nn.sigmoid(logits_t).reshape(N_EXPERT_GROUPS, per, tm)
    sel = scores + bias.reshape(N_EXPERT_GROUPS, per, 1)
    shape = sel.shape
    gi = lax.broadcasted_iota(jnp.int32, shape, 0)
    mi = lax.broadcasted_iota(jnp.int32, shape, 1)
    ei = gi * per + mi
    m1 = jnp.max(sel, axis=1, keepdims=True)
    first = jnp.min(jnp.where(sel == m1, mi, per), axis=1, keepdims=True)
    m2 = jnp.max(jnp.where(mi == first, NEG_INF, sel), axis=1, keepdims=True)
    gscore = m1 + m2
    gidx = lax.broadcasted_iota(jnp.int32, gscore.shape, 0)
    gmask = jnp.zeros(gscore.shape, F32)
    for _ in range(TOPK_GROUPS):
        m = jnp.max(gscore, axis=0, keepdims=True)
        pick = jnp.min(jnp.where(gscore == m, gidx, N_EXPERT_GROUPS), axis=0, keepdims=True)
        hit = gidx == pick
        gmask = jnp.where(hit, 1.0, gmask)
        gscore = jnp.where(hit, NEG_INF, gscore)
    cand = jnp.where(gmask > 0.0, sel, NEG_INF)
    chosen = jnp.zeros(shape, F32)
    for _ in range(TOP_K):
        m = jnp.max(jnp.max(cand, axis=1, keepdims=True), axis=0, keepdims=True)
        pick = jnp.where(cand == m, ei, n_experts)
        pick = jnp.min(jnp.min(pick, axis=1, keepdims=True), axis=0, keepdims=True)
        hit = ei == pick
        chosen = jnp.where(hit, 1.0, chosen)
        cand = jnp.where(hit, NEG_INF, cand)
    wsel = jnp.where(chosen > 0.0, scores, 0.0)
    den = jnp.sum(jnp.sum(wsel, axis=1, keepdims=True), axis=0, keepdims=True)
    return (wsel / den * ROUTED_SCALE).reshape(n_experts, tm)


def _outproj_kernel(ya_ref, hf_ref, hb_ref, bg_ref, yc_ref, yd_ref, x_ref, mod_ref,
                    wout_ref, gpost_ref, gffn_ref, rw_ref, rb_ref,
                    xo_ref, v_ref, gate_ref, cnt_ref, *, n_experts):
    gw = ya_ref.shape[-1]
    m = mod_ref[0, 0]
    yb = ((hf_ref[0] + hb_ref[0]) * jax.nn.gelu(bg_ref[0])).astype(BF16)
    y = (_dot(ya_ref[0], wout_ref[0:gw, :]) + _dot(yb, wout_ref[gw:2 * gw, :])
         + _dot(yc_ref[0], wout_ref[2 * gw:3 * gw, :]) + _dot(yd_ref[0], wout_ref[3 * gw:4 * gw, :]))
    x1 = x_ref[0] + m[2:3] * _rms(y, gpost_ref[...])
    xo_ref[0] = x1
    v = (_rms(x1, gffn_ref[...]) * (1.0 + m[4:5]) + m[3:4]).astype(BF16)
    v_ref[0] = v
    logits_t = _dot_t(rw_ref[...], v)
    gates_t = _route(logits_t, rb_ref[...], n_experts)
    pad = jnp.zeros((LANES - n_experts, gates_t.shape[1]), F32)
    gates = jnp.concatenate([gates_t, pad], axis=0).T
    gate_ref[0] = gates
    cnt_ref[0, 0] = jnp.sum(jnp.where(gates > 0.0, 1.0, 0.0), axis=0, keepdims=True).astype(jnp.int32)


def _outproj(ya, hf, hb, bg, yc, yd, xs, modtab, w_out, gpost, gffn, rw_t, rbias, *, need_ctx):
    bsz, L, d = xs.shape
    n_experts = rw_t.shape[0]
    gw = ya.shape[-1]
    t0 = 0 if need_ctx else 1
    nt = L // ROW_TILE - t0
    rows_out = nt * ROW_TILE
    blk = lambda w: pl.BlockSpec((1, ROW_TILE, w), lambda b, t: (b, t + t0, 0))
    oblk = lambda w: pl.BlockSpec((1, ROW_TILE, w), lambda b, t: (b, t, 0))
    return pl.pallas_call(
        functools.partial(_outproj_kernel, n_experts=n_experts),
        grid=(bsz, nt),
        in_specs=[blk(gw)] * 6 + [blk(d),
                  pl.BlockSpec((1, 1, 6, d), lambda b, t: (b, jnp.minimum(t + t0, 1), 0, 0)),
                  _full(w_out), _full(gpost), _full(gffn), _full(rw_t), _full(rbias)],
        out_specs=[oblk(d), oblk(d), oblk(LANES),
                   pl.BlockSpec((1, 1, 1, LANES), lambda b, t: (b, t, 0, 0))],
        out_shape=[jax.ShapeDtypeStruct((bsz, rows_out, d), F32), jax.ShapeDtypeStruct((bsz, rows_out, d), BF16),
                   jax.ShapeDtypeStruct((bsz, rows_out, LANES), F32),
                   jax.ShapeDtypeStruct((bsz, nt, 1, LANES), jnp.int32)],
        compiler_params=_params("arbitrary", "arbitrary"),
    )(ya, hf, hb, bg, yc, yd, xs, modtab, w_out, gpost, gffn, rw_t, rbias)


SLOT_ALIGN = 16
EXPERT_BLOCK = 1024
SLOT_CHUNK = 512
SHORT_BITS = 2
CODE_BASE = 64.0


def _swiglu(v, wg, wu, wd):
    hg = _dot(v, wg.astype(BF16))
    hu = _dot(v, wu.astype(BF16))
    return hg * jax.nn.sigmoid(hg) * hu, wd.astype(BF16)


def _slot_cap(n_experts):
    rows = ROW_TILE * TOP_K + n_experts * (SLOT_ALIGN - 1)
    return -(-rows // SLOT_CHUNK) * SLOT_CHUNK


def _slot_codes(gates, off_row, ltri):
    chosen = gates > 0.0
    rank = _dot(ltri, jnp.where(chosen, 1.0, 0.0).astype(BF16))
    code = jnp.where(chosen, off_row.astype(F32) + rank + 1.0, 0.0)
    hi = jnp.floor(code * (1.0 / CODE_BASE))
    return hi.astype(BF16), (code - CODE_BASE * hi).astype(BF16)


def _slot_expert_onehot(off_row, len_row, first, rows):
    r = first + lax.broadcasted_iota(jnp.int32, (rows, LANES), 0)
    inside = jnp.where(r >= off_row, jnp.where(r < off_row + len_row, 1.0, 0.0), 0.0)
    return inside.astype(BF16)


def _row_digits(first, shape, axis):
    code = (first + 1 + lax.broadcasted_iota(jnp.int32, shape, axis)).astype(F32)
    hi = jnp.floor(code * (1.0 / CODE_BASE))
    return hi, code - CODE_BASE * hi


def _slot_dma(src, dst, src_off, dst_off, n_units, sem, start):
    def piece(first, size):
        cp = pltpu.make_async_copy(
            src.at[pl.ds(pl.multiple_of(src_off + first, SLOT_ALIGN), size)],
            dst.at[pl.ds(pl.multiple_of(dst_off + first, SLOT_ALIGN), size)], sem)
        if start:
            cp.start()
        else:
            cp.wait()

    for bit in range(SHORT_BITS):
        pl.when(((n_units >> bit) & 1) == 1)(
            functools.partial(piece, (n_units & ((1 << bit) - 1)) * SLOT_ALIGN, SLOT_ALIGN << bit))

    long_rows = SLOT_ALIGN << SHORT_BITS
    short_rows = (n_units & ((1 << SHORT_BITS) - 1)) * SLOT_ALIGN

    def long_piece(j, carry):
        piece(short_rows + j * long_rows, long_rows)
        return carry
    lax.fori_loop(0, n_units >> SHORT_BITS, long_piece, 0)


def _wait_rows(src, dst, n_units, sem):
    for bit in range((min(src.shape[0], dst.shape[0]) // SLOT_ALIGN).bit_length()):
        size = SLOT_ALIGN << bit

        @pl.when(((n_units >> bit) & 1) == 1)
        def _piece():
            pltpu.make_async_copy(src.at[pl.ds(0, size)], dst.at[pl.ds(0, size)], sem).wait()


def _dispatch_kernel(pos_ref, off_ref, len_ref, tot_ref, tpos_ref, tlen_ref,
                     v_ref, g_ref, offv_ref, lenv_ref, ltri_ref, xs_ref,
                     buf_s, zero_s, sem, *, n_experts):
    i = pl.program_id(0)
    nt = pl.num_programs(0)
    cur = i % 2
    cap = buf_s.shape[1]

    @pl.when(i == 0)
    def _zero():
        zero_s[...] = jnp.zeros_like(zero_s)

    g = g_ref[...]
    off_row, len_row = offv_ref[0], lenv_ref[0]
    hi, lo = _slot_codes(g, off_row, ltri_ref[...])
    g_hi = g.astype(BF16)
    g_lo = (g - g_hi.astype(F32)).astype(BF16)
    half = LANES // 2
    src = jnp.concatenate([v_ref[...], g_hi[:, :half], g_lo[:, :half]], axis=1)
    for c in range(cap // SLOT_CHUNK):
        first = c * SLOT_CHUNK
        oh = _slot_expert_onehot(off_row, len_row, first, SLOT_CHUNK)
        rh, rl = _row_digits(first, (SLOT_CHUNK, 1), 0)
        pick = jnp.where(_dot_t(oh, hi) == rh, jnp.where(_dot_t(oh, lo) == rl, 1.0, 0.0), 0.0)
        buf_s[cur, first:first + SLOT_CHUNK, :] = _dot(pick.astype(BF16), src).astype(BF16)

    def slots(tile, which, start):
        def body(e, carry):
            k = tile * n_experts + e
            _slot_dma(buf_s.at[which], xs_ref, off_ref[k], pos_ref[k], len_ref[k], sem.at[0], start)
            return carry
        lax.fori_loop(0, n_experts, body, 0)

    def tails(start):
        def body(e, carry):
            _slot_dma(zero_s, xs_ref, 0, tpos_ref[e], tlen_ref[e], sem.at[0], start)
            return carry
        lax.fori_loop(0, n_experts, body, 0)

    @pl.when(i > 0)
    def _drain_previous():
        _wait_rows(buf_s.at[1 - cur], xs_ref, tot_ref[i - 1], sem.at[0])

    slots(i, cur, True)

    @pl.when(i == nt - 1)
    def _last():
        tails(True)
        _wait_rows(buf_s.at[cur], xs_ref, tot_ref[i], sem.at[0])
        tails(False)


def _dispatch(v2, gates2, plan, ltri, *, n_experts, n_rows):
    t, d = v2.shape
    nt = t // ROW_TILE
    cap = _slot_cap(n_experts)
    assert n_experts <= LANES // 2
    width = d + LANES
    grid_spec = pltpu.PrefetchScalarGridSpec(
        num_scalar_prefetch=6, grid=(nt,),
        in_specs=[pl.BlockSpec((ROW_TILE, d), lambda i, *_: (i, 0)),
                  pl.BlockSpec((ROW_TILE, LANES), lambda i, *_: (i, 0)),
                  pl.BlockSpec((1, 1, LANES), lambda i, *_: (i, 0, 0)),
                  pl.BlockSpec((1, 1, LANES), lambda i, *_: (i, 0, 0)),
                  pl.BlockSpec(ltri.shape, lambda i, *_: (0, 0))],
        out_specs=pl.BlockSpec(memory_space=pl.ANY),
        scratch_shapes=[pltpu.VMEM((2, cap, width), BF16),
                        pltpu.VMEM((EXPERT_BLOCK, width), BF16),
                        pltpu.SemaphoreType.DMA((1,))])
    return pl.pallas_call(
        functools.partial(_dispatch_kernel, n_experts=n_experts),
        grid_spec=grid_spec,
        out_shape=jax.ShapeDtypeStruct((n_rows, width), BF16),
        compiler_params=_params("arbitrary"),
    )(plan["pos"], plan["off"], plan["len"], plan["tile_len"], plan["tail_pos"], plan["tail_len"],
      v2, gates2, plan["off_v"], plan["len_v"], ltri)


def _expert_kernel(be_ref, na_ref, x_ref, wg_ref, wu_ref, wd_ref, y_ref, wg_s, wu_s, wd_s):
    b = pl.program_id(0)
    e = be_ref[b]

    @pl.when(jnp.logical_or(b == 0, e != be_ref[jnp.maximum(b - 1, 0)]))
    def _new_expert():
        wg_s[...] = wg_ref[0, 0].astype(BF16)
        wu_s[...] = wu_ref[0, 0].astype(BF16)
        wd_s[...] = wd_ref[0, 0].astype(BF16)

    @pl.when(b < na_ref[0])
    def _active():
        d = wg_s.shape[0]
        x = x_ref[:, 0:d]
        gates = x_ref[:, d:d + LANES].astype(F32)
        lane = lax.broadcasted_iota(jnp.int32, gates.shape, 1) & (LANES // 2 - 1)
        gcol = jnp.sum(jnp.where(lane == e, gates, 0.0), axis=1, keepdims=True)
        hg = _dot(x, wg_s[...])
        h = hg * jax.nn.sigmoid(hg) * _dot(x, wu_s[...])
        y_ref[...] = (_dot(h.astype(BF16), wd_s[...]) * gcol).astype(BF16)


def _experts(xs_sorted, plan, wg, wu, wd, layer):
    n_rows, width = xs_sorted.shape
    _, _, d, de = wg.shape
    nb = n_rows // EXPERT_BLOCK
    rowmap = lambda b, be, na: (jnp.minimum(b, na[0] - 1), 0)
    grid_spec = pltpu.PrefetchScalarGridSpec(
        num_scalar_prefetch=2, grid=(nb,),
        in_specs=[pl.BlockSpec((EXPERT_BLOCK, width), rowmap),
                  pl.BlockSpec((1, 1, d, de), lambda b, be, na: (layer, be[b], 0, 0)),
                  pl.BlockSpec((1, 1, d, de), lambda b, be, na: (layer, be[b], 0, 0)),
                  pl.BlockSpec((1, 1, de, d), lambda b, be, na: (layer, be[b], 0, 0))],
        out_specs=pl.BlockSpec((EXPERT_BLOCK, d), rowmap),
        scratch_shapes=[pltpu.VMEM((d, de), BF16), pltpu.VMEM((d, de), BF16), pltpu.VMEM((de, d), BF16)])
    return pl.pallas_call(
        _expert_kernel, grid_spec=grid_spec,
        out_shape=jax.ShapeDtypeStruct((n_rows, d), BF16),
        compiler_params=_params("arbitrary"),
    )(plan["block_expert"], plan["n_active"], xs_sorted, wg, wu, wd)


def _combine_kernel(pos_ref, off_ref, len_ref, tot_ref,
                    ys_ref, v_ref, g_ref, offv_ref, lenv_ref, ltri_ref, x_ref, mod_ref,
                    sg_ref, su_ref, sd_ref, gpost_ref, o_ref, buf_s, sem, *, n_experts):
    i = pl.program_id(0)
    nt = pl.num_programs(0)
    cur = i % 2
    cap = buf_s.shape[1]

    def slots(tile, which, start):
        def body(e, carry):
            k = tile * n_experts + e
            _slot_dma(ys_ref, buf_s.at[which], pos_ref[k], off_ref[k], len_ref[k], sem.at[which], start)
            return carry
        lax.fori_loop(0, n_experts, body, 0)

    @pl.when(i == 0)
    def _first():
        buf_s[...] = jnp.zeros_like(buf_s)
        slots(0, 0, True)

    @pl.when(i + 1 < nt)
    def _prefetch():
        slots(i + 1, 1 - cur, True)

    v = v_ref[...]
    h, wd = _swiglu(v, sg_ref[...], su_ref[...], sd_ref[...])
    acc = _dot(h.astype(BF16), wd)
    off_row, len_row = offv_ref[0], lenv_ref[0]
    hi, lo = _slot_codes(g_ref[...], off_row, ltri_ref[...])

    _wait_rows(ys_ref, buf_s.at[cur], tot_ref[i], sem.at[cur])
    for c in range(cap // SLOT_CHUNK):
        first = c * SLOT_CHUNK
        oh = _slot_expert_onehot(off_row, len_row, first, SLOT_CHUNK)
        rh, rl = _row_digits(first, (1, SLOT_CHUNK), 1)
        pick = jnp.where(_dot_t(hi, oh) == rh, jnp.where(_dot_t(lo, oh) == rl, 1.0, 0.0), 0.0)
        acc = acc + _dot(pick.astype(BF16), buf_s[cur, first:first + SLOT_CHUNK, :])
    o_ref[0] = x_ref[0] + mod_ref[0, 0][5:6] * _rms(acc, gpost_ref[...])


def _combine(ys_sorted, v2, gates2, plan, ltri, xs, modtab, sg, su, sd, gpost, *, n_experts, has_ctx):
    bsz, L, d = xs.shape
    tpb = L // ROW_TILE
    nt = bsz * tpb
    cap = _slot_cap(n_experts)
    full = lambda a: pl.BlockSpec(a.shape, lambda i, *_: (0,) * a.ndim)
    grid_spec = pltpu.PrefetchScalarGridSpec(
        num_scalar_prefetch=4, grid=(nt,),
        in_specs=[pl.BlockSpec(memory_space=pl.ANY),
                  pl.BlockSpec((ROW_TILE, d), lambda i, *_: (i, 0)),
                  pl.BlockSpec((ROW_TILE, LANES), lambda i, *_: (i, 0)),
                  pl.BlockSpec((1, 1, LANES), lambda i, *_: (i, 0, 0)),
                  pl.BlockSpec((1, 1, LANES), lambda i, *_: (i, 0, 0)),
                  full(ltri),
                  pl.BlockSpec((1, ROW_TILE, d), lambda i, *_: (i // tpb, i % tpb, 0)),
                  pl.BlockSpec((1, 1, 6, d),
                               lambda i, *_: (i // tpb, jnp.minimum(i % tpb, 1) if has_ctx else 1, 0, 0)),
                  full(sg), full(su), full(sd), full(gpost)],
        out_specs=pl.BlockSpec((1, ROW_TILE, d), lambda i, *_: (i // tpb, i % tpb, 0)),
        scratch_shapes=[pltpu.VMEM((2, cap, d), BF16), pltpu.SemaphoreType.DMA((2,))])
    return pl.pallas_call(
        functools.partial(_combine_kernel, n_experts=n_experts),
        grid_spec=grid_spec,
        out_shape=jax.ShapeDtypeStruct((bsz, L, d), F32),
        compiler_params=_params("arbitrary"),
    )(plan["pos"], plan["off"], plan["len"], plan["tile_len"], ys_sorted, v2, gates2, plan["off_v"], plan["len_v"], ltri,
      xs, modtab, sg, su, sd, gpost)


def _moe_plan(counts, n_experts, n_blocks):
    a = (counts + (SLOT_ALIGN - 1)) // SLOT_ALIGN * SLOT_ALIGN
    rows = jnp.sum(a, axis=0)
    region = (rows + (EXPERT_BLOCK - 1)) // EXPERT_BLOCK * EXPERT_BLOCK
    region_end = jnp.cumsum(region)
    region_start = region_end - region
    pos = region_start[None, :] + jnp.cumsum(a, axis=0) - a
    off = jnp.cumsum(a, axis=1) - a
    first_row = jnp.arange(n_blocks, dtype=jnp.int32) * EXPERT_BLOCK
    block_expert = jnp.minimum(jnp.sum(region_end[None, :n_experts] <= first_row[:, None], axis=1), n_experts - 1)
    flat = lambda t: t[:, :n_experts].reshape(-1).astype(jnp.int32)
    nt = counts.shape[0]
    return {
        "pos": flat(pos), "off": flat(off), "len": flat(a // SLOT_ALIGN),
        "tile_len": (jnp.sum(a, axis=1) // SLOT_ALIGN).astype(jnp.int32),
        "tail_pos": (region_start + rows)[:n_experts].astype(jnp.int32),
        "tail_len": ((region - rows) // SLOT_ALIGN)[:n_experts].astype(jnp.int32),
        "off_v": off.reshape(nt, 1, LANES).astype(jnp.int32),
        "len_v": a.reshape(nt, 1, LANES).astype(jnp.int32),
        "block_expert": block_expert.astype(jnp.int32),
        "n_active": (region_end[-1:] // EXPERT_BLOCK).astype(jnp.int32),
    }


def _moe(v, gates, counts, xs, modtab, wg, wu, wd, layer, sg, su, sd, gpost, *, has_ctx):
    bsz, L, d = xs.shape
    n_experts = wg.shape[1]
    t = bsz * L
    nt = t // ROW_TILE
    worst = t * TOP_K + nt * n_experts * (SLOT_ALIGN - 1) + n_experts * (EXPERT_BLOCK - 1)
    n_blocks = -(-worst // EXPERT_BLOCK)
    plan = _moe_plan(counts.reshape(nt, LANES), n_experts, n_blocks)
    ltri = jnp.tril(jnp.ones((ROW_TILE, ROW_TILE), F32), -1).astype(BF16)
    v2, gates2 = v.reshape(t, d), gates.reshape(t, LANES)
    xs_sorted = _dispatch(v2, gates2, plan, ltri, n_experts=n_experts, n_rows=n_blocks * EXPERT_BLOCK)
    ys_sorted = _experts(xs_sorted, plan, wg, wu, wd, layer)
    return _combine(ys_sorted, v2, gates2, plan, ltri, xs, modtab, sg, su, sd, gpost,
                    n_experts=n_experts, has_ctx=has_ctx)


def _tables(seq, ctx_len):
    def build(rot_dim, lead, slot):
        n = rot_dim // 4
        pos = jnp.arange(seq)
        row = (pos // GRID_W).astype(F32)
        col = (pos % GRID_W).astype(F32)
        inv = ROPE_THETA ** (-jnp.arange(n, dtype=F32) / n)
        ang = jnp.concatenate([row[:, None] * inv, col[:, None] * inv], axis=-1)
        cos, sin = jnp.cos(ang), jnp.sin(ang)
        zero = jnp.zeros_like(sin)
        tail = slot - lead - rot_dim
        one_l, zero_l = jnp.ones((seq, lead), F32), jnp.zeros((seq, lead), F32)
        one_t, zero_t = jnp.ones((seq, tail), F32), jnp.zeros((seq, tail), F32)
        c = jnp.concatenate([one_l, cos, cos, one_t], axis=1)
        s_lo = jnp.concatenate([zero_l, -sin, zero, zero_t], axis=1)
        s_hi = jnp.concatenate([zero_l, zero, sin, zero_t], axis=1)
        ctx_c = jnp.ones((ctx_len, slot), F32)
        ctx_s = jnp.zeros((ctx_len, slot), F32)
        return [jnp.concatenate([ctx_c, c], axis=0), jnp.concatenate([ctx_s, s_lo], axis=0),
                jnp.concatenate([ctx_s, s_hi], axis=0)]

    hd = [jnp.concatenate([t, t], axis=1) for t in build(HEAD_DIM, 0, HEAD_DIM)]
    mla = build(QK_ROPE, QK_NOPE, LANES)
    return hd + mla


def _dup_heads(w, n_heads):
    d = w.shape[0]
    wh = w.reshape(d, n_heads, 1, HEAD_DIM)
    return jnp.broadcast_to(wh, (d, n_heads, 2, HEAD_DIM)).reshape(d, n_heads * 2 * HEAD_DIM)


def kernel(x, c, ctx, c_ctx, w_mod, b_mod, g_pre_mix, g_post_mix, g_pre_ffn, g_post_ffn, w_in, a_q_norm, a_k_norm, conv_w, conv_b, lru_wa, lru_ba, lru_wi, lru_bi, lru_lambda, c_sink, d_q_norm, d_w_uq, d_kv_norm, d_w_ukv, w_out, router_w, router_bias, w_gate, w_up, w_down, sh_gate, sh_up, sh_down):
    bsz, seq, d = x.shape
    ctx_len = ctx.shape[1]
    depth = w_mod.shape[0]
    gw = d // 4
    lru_w = conv_w.shape[-1]
    q_lora = d_q_norm.shape[-1]
    kv_lora = d_kv_norm.shape[-1]
    n_experts = router_w.shape[-1]
    a_heads = gw // HEAD_DIM
    a_kv = (w_in.shape[-1] - (2 * gw + 2 * lru_w + q_lora + kv_lora + QK_ROPE)) // (4 * HEAD_DIM)
    mla_scale = (QK_NOPE + QK_ROPE) ** -0.5
    v_dim = gw // D_HEADS
    assert ctx_len == ROW_TILE and a_kv == 2 and a_heads == 4 and n_experts <= LANES

    cond = jnp.zeros((16, d), F32).at[:bsz].set(c).at[bsz].set(c_ctx)
    mod = _modulation(cond, w_mod, b_mod).reshape(depth, 16, 6, d)
    tabs = _tables(seq, ctx_len)
    seg = jnp.kron(jnp.eye(a_heads, dtype=F32), jnp.ones((HEAD_DIM, HEAD_DIM), F32)).astype(BF16)

    xs = jnp.concatenate([ctx, x], axis=1)
    for l in range(depth):
        modtab = jnp.stack([jnp.broadcast_to(mod[l, bsz], (bsz, 6, d)), mod[l, :bsz]], axis=1)

        offs = [0]
        for wdt in (gw, a_kv * HEAD_DIM, a_kv * HEAD_DIM, lru_w, lru_w, gw, a_kv * HEAD_DIM, a_kv * HEAD_DIM,
                    q_lora, kv_lora, QK_ROPE):
            offs.append(offs[-1] + wdt)
        col = lambda i: w_in[l][:, offs[i]:offs[i + 1]]
        zeros = lambda n: jnp.zeros((d, n), F32)
        kr_slot = jnp.concatenate([zeros(QK_NOPE), col(10), zeros(LANES - QK_NOPE - QK_ROPE)], axis=1)
        w_in_p = jnp.concatenate(
            [col(0), _dup_heads(col(1), a_kv), _dup_heads(col(2), a_kv), col(3), col(4),
             col(5), _dup_heads(col(6), a_kv), _dup_heads(col(7), a_kv),
             col(8), zeros(2 * LANES - q_lora), col(9)] + [kr_slot] * D_HEADS, axis=1).astype(BF16)

        qk = QK_NOPE + QK_ROPE
        wuq = d_w_uq[l].reshape(q_lora, D_HEADS, qk)
        wuq = jnp.pad(wuq, ((0, 2 * LANES - q_lora), (0, 0), (0, LANES - qk))).reshape(2 * LANES, D_HEADS * LANES)
        wukv = d_w_ukv[l].reshape(kv_lora, D_HEADS, QK_NOPE + v_dim)
        wk = jnp.pad(wukv[:, :, :QK_NOPE], ((0, 0), (0, 0), (0, LANES - QK_NOPE))).reshape(kv_lora, D_HEADS * LANES)
        wv = wukv[:, :, QK_NOPE:].reshape(kv_lora, D_HEADS * v_dim)
        wukv_p = jnp.concatenate([wk, wv], axis=1).astype(BF16)
        dqn = jnp.pad(d_q_norm[l], (0, 2 * LANES - q_lora)).reshape(1, 2 * LANES)

        qa, ka, va, bx, bg, qc, kc, vc, qd, kd, vd = _inproj(
            xs, modtab, g_pre_mix[l].reshape(1, d), w_in_p, seg, tabs,
            jnp.tile(a_q_norm[l], a_heads).reshape(1, gw), jnp.tile(a_k_norm[l], 2 * a_kv).reshape(1, gw),
            dqn, wuq.astype(BF16), d_kv_norm[l].reshape(1, kv_lora), wukv_p,
            q_lora=q_lora, mla_scale=mla_scale, ctx_len=ctx_len)

        need_ctx = l < depth - 1
        ya = _attention(qa, ka, va, split_q=True, ctx_len=ctx_len, need_ctx=need_ctx)
        yd = _attention(qd, kd, vd, split_q=False, ctx_len=ctx_len, need_ctx=need_ctx)
        yc = _window_attention(qc, kc, vc, c_sink[l], ctx_len=ctx_len, need_ctx=need_ctx)

        blocks = lru_wa.shape[2]
        bdiag = lambda wts: jnp.stack([jax.scipy.linalg.block_diag(*[wts[dd, h] for h in range(blocks)])
                                       for dd in range(2)]).astype(BF16)
        hf, hb = _lru(bx, conv_w[l], conv_b[l].reshape(1, lru_w), bdiag(lru_wa[l]), lru_ba[l].reshape(2, 1, lru_w),
                      bdiag(lru_wi[l]), lru_bi[l].reshape(2, 1, lru_w), lru_lambda[l].reshape(2, 1, lru_w),
                      ctx_len=ctx_len)

        rw_t = router_w[l].T.astype(BF16)
        xs_mid, v, gates, counts = _outproj(ya, hf, hb, bg, yc, yd, xs, modtab, w_out[l].astype(BF16),
                                            g_post_mix[l].reshape(1, d), g_pre_ffn[l].reshape(1, d),
                                            rw_t, router_bias[l].reshape(n_experts, 1), need_ctx=need_ctx)
        xs = _moe(v, gates, counts, xs_mid, modtab, w_gate, w_up, w_down, l,
                  sh_gate[l], sh_up[l], sh_down[l], g_post_ffn[l].reshape(1, d), has_ctx=need_ctx)
    return xs
```

```python
import functools

import jax
import jax.numpy as jnp
from jax import lax
from jax.experimental import pallas as pl
from jax.experimental.pallas import tpu as pltpu

F32 = jnp.float32
BF16 = jnp.bfloat16

GRID_W = 64
HEAD_DIM = 64
ROPE_THETA = 10000.0
NORM_EPS = 1e-6
WINDOW = 128
WIN_QUERIES = 256
D_HEADS = 4
QK_NOPE = 64
QK_ROPE = 32
LRU_C = 8.0
CONV_W = 4
TOP_K = 8
N_EXPERT_GROUPS = 8
TOPK_GROUPS = 4
ROUTED_SCALE = 2.5

LANES = 128
VMEM_LIMIT = 56 * 1024 * 1024
ROW_TILE = 256
INPROJ_ROWS = 768

NEG_INF = float("-inf")


def _params(*sem):
    return pltpu.CompilerParams(dimension_semantics=sem, vmem_limit_bytes=VMEM_LIMIT)


def _rms(x, gain):
    return x * lax.rsqrt(jnp.mean(x * x, axis=-1, keepdims=True) + NORM_EPS) * gain


def _dot(a, b):
    return jnp.dot(a, b, preferred_element_type=F32)


def _dot_t(a, b):
    return lax.dot_general(a, b, (((1,), (1,)), ((), ())), preferred_element_type=F32)


def _full(a):
    return pl.BlockSpec(a.shape, lambda *_: (0,) * a.ndim)


def _mod_kernel(c_ref, w_ref, b_ref, o_ref):
    a = c_ref[...]
    a = a * jax.nn.sigmoid(a)
    o_ref[0] = _dot(a.astype(BF16), w_ref[0].astype(BF16)) + b_ref[0]


def _modulation(cond, w_mod, b_mod):
    depth, d, n = w_mod.shape
    tn = n // 4
    rows = cond.shape[0]
    return pl.pallas_call(
        _mod_kernel,
        grid=(depth, n // tn),
        in_specs=[pl.BlockSpec((rows, d), lambda l, j: (0, 0)),
                  pl.BlockSpec((1, d, tn), lambda l, j: (l, 0, j)),
                  pl.BlockSpec((1, 1, tn), lambda l, j: (l, 0, j))],
        out_specs=pl.BlockSpec((1, rows, tn), lambda l, j: (l, 0, j)),
        out_shape=jax.ShapeDtypeStruct((depth, rows, n), F32),
        compiler_params=_params("arbitrary", "arbitrary"),
    )(cond, w_mod, b_mod.reshape(depth, 1, n))


def _rope(t, cos, sin_lo, sin_hi, half):
    w = t.shape[-1]
    return t * cos + pltpu.roll(t, w - half, 1) * sin_lo + pltpu.roll(t, half, 1) * sin_hi


def _inproj_kernel(x_ref, mod_ref, gpre_ref, w_ref, seg_ref,
                   cos_ref, sinl_ref, sinh_ref, cosd_ref, sindl_ref, sindh_ref,
                   aqn_ref, akn_ref, dqn_ref, wuq_ref, dkvn_ref, wukv_ref,
                   qa_ref, ka_ref, va_ref, bx_ref, bg_ref, qc_ref, kc_ref, vc_ref,
                   qd_ref, kd_ref, vd_ref, *, q_lora, mla_scale, ctx_len):
    x = x_ref[0]
    rows = x.shape[0]
    mods = mod_ref[0]
    is_ctx = pl.program_id(1) * rows + lax.broadcasted_iota(jnp.int32, (rows, 1), 0) < ctx_len
    shift = jnp.where(is_ctx, mods[0, 0:1], mods[1, 0:1])
    scale = jnp.where(is_ctx, mods[0, 1:2], mods[1, 1:2])
    u = (_rms(x, gpre_ref[...]) * (1.0 + scale) + shift).astype(BF16)
    proj = _dot(u, w_ref[...])

    seg = seg_ref[...]
    lane = lax.broadcasted_iota(jnp.int32, (1, LANES), 1)
    low = lane < HEAD_DIM
    cos1, sin_lo1, sin_hi1 = cos_ref[...], sinl_ref[...], sinh_ref[...]
    cosd1, sind_lo1, sind_hi1 = cosd_ref[...], sindl_ref[...], sindh_ref[...]

    def head_rms(t, gain):
        w = t.shape[-1]
        sq = t * t
        hi = sq.astype(BF16)
        lo = (sq - hi.astype(F32)).astype(BF16)
        ms = (_dot(hi, seg[0:w, 0:w]) + _dot(lo, seg[0:w, 0:w])) * (1.0 / HEAD_DIM)
        return t * lax.rsqrt(ms + NORM_EPS) * gain

    def rope_hd(t):
        n = t.shape[-1] // LANES
        return _rope(t, *(jnp.concatenate([a] * n, axis=1) for a in (cos1, sin_lo1, sin_hi1)), HEAD_DIM // 2)

    def rope_r(t):
        n = t.shape[-1] // LANES
        return _rope(t, *(jnp.concatenate([a] * n, axis=1) for a in (cosd1, sind_lo1, sind_hi1)), QK_ROPE // 2)

    def per_query_head(t):
        r = pltpu.roll(t, HEAD_DIM, 1)
        return jnp.concatenate([jnp.where(low, t, r), jnp.where(low, r, t)], axis=1)

    sc_hd = HEAD_DIM ** -0.5
    gw = qa_ref.shape[-1]
    kvw = gw // 2
    aq, ak, av = proj[:, 0:gw], proj[:, gw:gw + kvw], proj[:, gw + kvw:2 * gw]
    bx, bg = proj[:, 2 * gw:3 * gw], proj[:, 3 * gw:4 * gw]
    cq_, ck, cv = proj[:, 4 * gw:5 * gw], proj[:, 5 * gw:5 * gw + kvw], proj[:, 5 * gw + kvw:6 * gw]
    dq, dkv, dkr = proj[:, 6 * gw:7 * gw], proj[:, 7 * gw:7 * gw + LANES], proj[:, 7 * gw + LANES:8 * gw]
    qa_ref[0] = (rope_hd(head_rms(aq, aqn_ref[...])) * sc_hd).astype(BF16)
    ka_ref[0] = per_query_head(rope_hd(head_rms(ak, akn_ref[...]))).astype(BF16)
    va_ref[0] = per_query_head(av).astype(BF16)
    bx_ref[0] = bx
    bg_ref[0] = bg
    qc_ref[0] = (rope_hd(cq_) * sc_hd).astype(BF16)
    kc_ref[0] = per_query_head(rope_hd(ck)).astype(BF16)
    vc_ref[0] = per_query_head(cv).astype(BF16)
    cq = dq * lax.rsqrt(jnp.sum(dq * dq, axis=-1, keepdims=True) * (1.0 / q_lora) + NORM_EPS) * dqn_ref[...]
    qd = _dot(cq.astype(BF16), wuq_ref[...])
    qd_ref[0] = (rope_r(qd) * mla_scale).astype(BF16)
    ckv = _rms(dkv, dkvn_ref[...])
    kv = _dot(ckv.astype(BF16), wukv_ref[...])
    kr = rope_r(pltpu.roll(dkr, QK_NOPE, 1))
    kd_ref[0] = (kv[:, 0:512] + jnp.concatenate([kr] * D_HEADS, axis=1)).astype(BF16)
    vd_ref[0] = kv[:, 512:768].astype(BF16)


def _inproj(xs, modtab, gpre, w_in_p, seg, tabs, aqn, akn, dqn, wuq, dkvn, wukv, *, q_lora, mla_scale, ctx_len):
    bsz, L, d = xs.shape
    rows = INPROJ_ROWS
    nt = L // rows
    rowblk = lambda w: pl.BlockSpec((1, rows, w), lambda b, t: (b, t, 0))
    tab = lambda a: pl.BlockSpec((rows, a.shape[1]), lambda b, t: (t, 0))
    out_w = [(256, BF16)] * 3 + [(256, F32)] * 2 + [(256, BF16)] * 3 + [(512, BF16), (512, BF16), (256, BF16)]
    return pl.pallas_call(
        functools.partial(_inproj_kernel, q_lora=q_lora, mla_scale=mla_scale, ctx_len=ctx_len),
        grid=(bsz, nt),
        in_specs=[rowblk(d),
                  pl.BlockSpec((1, 2, 6, d), lambda b, t: (b, 0, 0, 0)),
                  _full(gpre), _full(w_in_p), _full(seg)] + [tab(a) for a in tabs]
                 + [_full(a) for a in (aqn, akn, dqn, wuq, dkvn, wukv)],
        out_specs=[rowblk(w) for w, _ in out_w],
        out_shape=[jax.ShapeDtypeStruct((bsz, L, w), dt) for w, dt in out_w],
        compiler_params=_params("arbitrary", "arbitrary"),
    )(xs, modtab, gpre, w_in_p, seg, *tabs, aqn, akn, dqn, wuq, dkvn, wukv)


def _attn_kernel(q_ref, k_ref, v_ref, o_ref, *, split_q, ctx_len, k_len, first_tile):
    lane = lax.broadcasted_iota(jnp.int32, (1, LANES), 1)
    low = lane < HEAD_DIM

    def run(n_keys):
        for g in range(v_ref.shape[-1] // LANES):
            outs = []
            for h in range(2):
                if split_q:
                    lanes = slice(g * LANES, (g + 1) * LANES)
                    keep = low if h == 0 else jnp.logical_not(low)
                    q = q_ref[0, :, lanes]
                    q = jnp.where(keep, q, jnp.zeros_like(q))
                else:
                    lanes = slice((2 * g + h) * LANES, (2 * g + h + 1) * LANES)
                    q = q_ref[0, :, lanes]
                s = _dot_t(q, k_ref[0, 0:n_keys, lanes])
                p = jnp.exp(s - jnp.max(s, axis=-1, keepdims=True))
                den = jnp.sum(p, axis=-1, keepdims=True)
                outs.append(_dot(p.astype(BF16), v_ref[0, 0:n_keys, g * LANES:(g + 1) * LANES]) / den)
            o_ref[0, :, g * LANES:(g + 1) * LANES] = jnp.where(low, outs[0], outs[1]).astype(o_ref.dtype)

    if first_tile > 0:
        run(k_len)
        return
    t = pl.program_id(1)

    @pl.when(t == 0)
    def _ctx():
        run(ctx_len)

    @pl.when(t > 0)
    def _lat():
        run(k_len)


def _attention(q, k, v, *, split_q, ctx_len, need_ctx):
    bsz, L, qw = q.shape
    vw = v.shape[-1]
    tq = ROW_TILE
    assert ctx_len == tq
    t0 = 0 if need_ctx else 1
    return pl.pallas_call(
        functools.partial(_attn_kernel, split_q=split_q, ctx_len=ctx_len, k_len=L, first_tile=t0),
        grid=(bsz, L // tq - t0),
        in_specs=[pl.BlockSpec((1, tq, qw), lambda b, t: (b, t + t0, 0)),
                  pl.BlockSpec((1, L, qw), lambda b, t: (b, 0, 0)),
                  pl.BlockSpec((1, L, vw), lambda b, t: (b, 0, 0))],
        out_specs=pl.BlockSpec((1, tq, vw), lambda b, t: (b, t + t0, 0)),
        out_shape=jax.ShapeDtypeStruct((bsz, L, vw), BF16),
        compiler_params=_params("arbitrary", "arbitrary"),
    )(q, k, v)


def _winattn_kernel(sink_ref, q_ref, k_ref, v_ref, o_ref, *, ctx_len, seq, first_block):
    t = pl.program_id(1) + first_block
    lane = lax.broadcasted_iota(jnp.int32, (1, LANES), 1)
    low = lane < HEAD_DIM
    ctx_blocks = ctx_len // WIN_QUERIES
    win = WIN_QUERIES + 2 * WINDOW
    n_groups = q_ref.shape[-1] // LANES

    def finish(parts, sink):
        m = sink
        for s, _ in parts:
            m = jnp.maximum(m, jnp.max(s, axis=-1, keepdims=True))
        den = jnp.exp(sink - m)
        acc = None
        for s, vv in parts:
            p = jnp.exp(s - m)
            den = den + jnp.sum(p, axis=-1, keepdims=True)
            o = _dot(p.astype(BF16), vv)
            acc = o if acc is None else acc + o
        return acc / den

    def heads(fn):
        tq = q_ref.shape[1]
        first_head = lax.broadcasted_iota(jnp.int32, (2 * tq, 1), 0) < tq
        for g in range(n_groups):
            lanes = slice(g * LANES, (g + 1) * LANES)
            q = q_ref[0, :, lanes]
            zero = jnp.zeros_like(q)
            q2 = jnp.concatenate([jnp.where(low, q, zero), jnp.where(low, zero, q)], axis=0)
            sink = jnp.where(first_head, sink_ref[2 * g], sink_ref[2 * g + 1])
            o2 = fn(q2, sink, lanes)
            o_ref[0, :, lanes] = jnp.where(low, o2[0:tq], o2[tq:2 * tq]).astype(o_ref.dtype)

    def ctx_queries():
        def one(q, sink, lanes):
            return finish([(_dot_t(q, k_ref[0, 0:ctx_len, lanes]), v_ref[0, 0:ctx_len, lanes])], sink)
        heads(one)

    def latent_queries():
        n = t - ctx_blocks
        start = jnp.clip(n * WIN_QUERIES - WINDOW, 0, seq - win)
        off = pl.multiple_of(ctx_len + start, WINDOW)
        qrow = lax.broadcasted_iota(jnp.int32, (2 * WIN_QUERIES, win), 0)
        qpos = n * WIN_QUERIES + jnp.where(qrow < WIN_QUERIES, qrow, qrow - WIN_QUERIES)
        kpos = start + lax.broadcasted_iota(jnp.int32, (2 * WIN_QUERIES, win), 1)
        allowed = jnp.abs(kpos - qpos) <= WINDOW

        def one(q, sink, lanes):
            s_ctx = _dot_t(q, k_ref[0, 0:ctx_len, lanes])
            s_win = jnp.where(allowed, _dot_t(q, k_ref[0, pl.ds(off, win), lanes]), NEG_INF)
            return finish([(s_ctx, v_ref[0, 0:ctx_len, lanes]), (s_win, v_ref[0, pl.ds(off, win), lanes])], sink)
        heads(one)

    if first_block >= ctx_blocks:
        latent_queries()
    else:
        pl.when(t < ctx_blocks)(ctx_queries)
        pl.when(t >= ctx_blocks)(latent_queries)


def _window_attention(q, k, v, sink, *, ctx_len, need_ctx):
    bsz, L, w = q.shape
    b0 = 0 if need_ctx else ctx_len // WIN_QUERIES
    return pl.pallas_call(
        functools.partial(_winattn_kernel, ctx_len=ctx_len, seq=L - ctx_len, first_block=b0),
        grid=(bsz, L // WIN_QUERIES - b0),
        in_specs=[pl.BlockSpec(memory_space=pltpu.SMEM),
                  pl.BlockSpec((1, WIN_QUERIES, w), lambda b, t: (b, t + b0, 0)),
                  pl.BlockSpec((1, L, w), lambda b, t: (b, 0, 0)),
                  pl.BlockSpec((1, L, w), lambda b, t: (b, 0, 0))],
        out_specs=pl.BlockSpec((1, WIN_QUERIES, w), lambda b, t: (b, t + b0, 0)),
        out_shape=jax.ShapeDtypeStruct((bsz, L, w), BF16),
        compiler_params=_params("arbitrary", "arbitrary"),
    )(sink, q, k, v)


def _lru_kernel(xf_ref, xfp_ref, xfn_ref, xb_ref, xbp_ref, xbn_ref,
                cw_ref, cb_ref, wa_ref, ba_ref, wi_ref, bi_ref, lam_ref,
                hf_ref, hb_ref,
                af_s, bf_s, ab_s, bb_s, of_s, ob_s, sf_s, sb_s, *, n_chunks):
    i = pl.program_id(0)
    bsz, tc, w = xf_ref.shape
    rows = bsz * tc
    ti = lax.broadcasted_iota(jnp.int32, (bsz, tc, w), 1)

    def coeffs(x_ref, prev_ref, next_ref, chunk, d, a_s, b_s):
        has_prev = (chunk >= 2).astype(F32)
        has_next = jnp.logical_and(chunk >= 1, chunk <= n_chunks - 2).astype(F32)
        x = x_ref[...]
        p1 = prev_ref[:, 7:8, :] * has_prev
        n0 = next_ref[:, 0:1, :] * has_next
        n1 = next_ref[:, 1:2, :] * has_next
        x2 = x.reshape(rows, w)
        xm1 = jnp.where(ti == 0, p1, pltpu.roll(x2, 1, 0).reshape(bsz, tc, w))
        xp1 = jnp.where(ti == tc - 1, n0, pltpu.roll(x2, rows - 1, 0).reshape(bsz, tc, w))
        xp2 = jnp.where(ti == tc - 1, n1,
                        jnp.where(ti == tc - 2, n0, pltpu.roll(x2, rows - 2, 0).reshape(bsz, tc, w)))
        cw = cw_ref[...]
        xc = (cb_ref[...] + xm1 * cw[0:1] + x * cw[1:2] + xp1 * cw[2:3] + xp2 * cw[3:4]).reshape(rows, w)
        xcb = xc.astype(BF16)
        r = 0.5 * jnp.tanh(0.5 * (_dot(xcb, wa_ref[d]) + ba_ref[d])) + 0.5
        ig = 0.5 * jnp.tanh(0.5 * (_dot(xcb, wi_ref[d]) + bi_ref[d])) + 0.5
        log_a = (-LRU_C) * r * jax.nn.softplus(-lam_ref[d])
        a = jnp.exp(log_a)
        b = jnp.sqrt(1.0 - a * a) * (ig * xc)
        for j in range(w // LANES):
            a_s[j] = a[:, j * LANES:(j + 1) * LANES]
            b_s[j] = b[:, j * LANES:(j + 1) * LANES]

    chunk_b = jnp.where(i == 0, 0, n_chunks - i)
    coeffs(xf_ref, xfp_ref, xfn_ref, i, 0, af_s, bf_s)
    coeffs(xb_ref, xbp_ref, xbn_ref, chunk_b, 1, ab_s, bb_s)

    @pl.when(i == 0)
    def _init():
        sf_s[...] = jnp.zeros_like(sf_s)
        sb_s[...] = jnp.zeros_like(sb_s)

    nl = w // LANES

    def step(t, carry):
        fwd = pl.ds(t, bsz, stride=tc)
        bwd = pl.ds(tc - 1 - t, bsz, stride=tc)
        out = []
        for j in range(nl):
            hf = af_s[j, fwd, :] * carry[j] + bf_s[j, fwd, :]
            hb = ab_s[j, bwd, :] * carry[nl + j] + bb_s[j, bwd, :]
            of_s[j, fwd, :] = hf
            ob_s[j, bwd, :] = hb
            out.append((hf, hb))
        return tuple(o[0] for o in out) + tuple(o[1] for o in out)

    init = tuple(sf_s[j] for j in range(nl)) + tuple(sb_s[j] for j in range(nl))
    fin = lax.fori_loop(0, tc, step, init, unroll=8)
    for j in range(nl):
        sf_s[j] = fin[j]
        sb_s[j] = fin[nl + j]
    hf_ref[...] = jnp.concatenate([of_s[j] for j in range(nl)], axis=1).reshape(bsz, tc, w)
    hb_ref[...] = jnp.concatenate([ob_s[j] for j in range(nl)], axis=1).reshape(bsz, tc, w)


def _lru(bx, conv_w, conv_b, wa, ba, wi, bi, lam, *, ctx_len):
    bsz, L, w = bx.shape
    tc = ctx_len
    nc = L // tc
    hb_blocks = tc // 8
    last8 = L // 8 - 1

    def fchunk(i):
        return i

    def bchunk(i):
        return jnp.where(i == 0, 0, nc - i)

    def cur(cf):
        return pl.BlockSpec((bsz, tc, w), lambda i: (0, cf(i), 0))

    def prev(cf):
        return pl.BlockSpec((bsz, 8, w), lambda i: (0, jnp.maximum(cf(i) * hb_blocks - 1, 0), 0))

    def nxt(cf):
        return pl.BlockSpec((bsz, 8, w), lambda i: (0, jnp.minimum((cf(i) + 1) * hb_blocks, last8), 0))

    small = [conv_w, conv_b, wa, ba, wi, bi, lam]
    nl = w // LANES
    scr = [pltpu.VMEM((nl, bsz * tc, LANES), F32)] * 6 + [pltpu.VMEM((nl, bsz, LANES), F32)] * 2
    return pl.pallas_call(
        functools.partial(_lru_kernel, n_chunks=nc),
        grid=(nc,),
        in_specs=[cur(fchunk), prev(fchunk), nxt(fchunk), cur(bchunk), prev(bchunk), nxt(bchunk)]
                 + [_full(a) for a in small],
        out_specs=[cur(fchunk), cur(bchunk)],
        out_shape=[jax.ShapeDtypeStruct((bsz, L, w), F32)] * 2,
        scratch_shapes=scr,
        compiler_params=_params("arbitrary"),
    )(bx, bx, bx, bx, bx, bx, *small)


def _route(logits_t, bias, n_experts):
    per = n_experts // N_EXPERT_GROUPS
    tm = logits_t.shape[-1]
    scores = jax.nn.sigmoid(logits_t).reshape(N_EXPERT_GROUPS, per, tm)
    sel = scores + bias.reshape(N_EXPERT_GROUPS, per, 1)
    shape = sel.shape
    gi = lax.broadcasted_iota(jnp.int32, shape, 0)
    mi = lax.broadcasted_iota(jnp.int32, shape, 1)
    ei = gi * per + mi
    m1 = jnp.max(sel, axis=1, keepdims=True)
    first = jnp.min(jnp.where(sel == m1, mi, per), axis=1, keepdims=True)
    m2 = jnp.max(jnp.where(mi == first, NEG_INF, sel), axis=1, keepdims=True)
    gscore = m1 + m2
    gidx = lax.broadcasted_iota(jnp.int32, gscore.shape, 0)
    gmask = jnp.zeros(gscore.shape, F32)
    for _ in range(TOPK_GROUPS):
        m = jnp.max(gscore, axis=0, keepdims=True)
        pick = jnp.min(jnp.where(gscore == m, gidx, N_EXPERT_GROUPS), axis=0, keepdims=True)
        hit = gidx == pick
        gmask = jnp.where(hit, 1.0, gmask)
        gscore = jnp.where(hit, NEG_INF, gscore)
    cand = jnp.where(gmask > 0.0, sel, NEG_INF)
    chosen = jnp.zeros(shape, F32)
    for _ in range(TOP_K):
        m = jnp.max(jnp.max(cand, axis=1, keepdims=True), axis=0, keepdims=True)
        pick = jnp.where(cand == m, ei, n_experts)
        pick = jnp.min(jnp.min(pick, axis=1, keepdims=True), axis=0, keepdims=True)
        hit = ei == pick
        chosen = jnp.where(hit, 1.0, chosen)
        cand = jnp.where(hit, NEG_INF, cand)
    wsel = jnp.where(chosen > 0.0, scores, 0.0)
    den = jnp.sum(jnp.sum(wsel, axis=1, keepdims=True), axis=0, keepdims=True)
    return (wsel / den * ROUTED_SCALE).reshape(n_experts, tm)


def _outproj_kernel(ya_ref, hf_ref, hb_ref, bg_ref, yc_ref, yd_ref, x_ref, mod_ref,
                    wout_ref, gpost_ref, gffn_ref, rw_ref, rb_ref,
                    xo_ref, v_ref, gate_ref, cnt_ref, *, n_experts):
    gw = ya_ref.shape[-1]
    m = mod_ref[0, 0]
    yb = ((hf_ref[0] + hb_ref[0]) * jax.nn.gelu(bg_ref[0])).astype(BF16)
    y = (_dot(ya_ref[0], wout_ref[0:gw, :]) + _dot(yb, wout_ref[gw:2 * gw, :])
         + _dot(yc_ref[0], wout_ref[2 * gw:3 * gw, :]) + _dot(yd_ref[0], wout_ref[3 * gw:4 * gw, :]))
    x1 = x_ref[0] + m[2:3] * _rms(y, gpost_ref[...])
    xo_ref[0] = x1
    v = (_rms(x1, gffn_ref[...]) * (1.0 + m[4:5]) + m[3:4]).astype(BF16)
    v_ref[0] = v
    logits_t = _dot_t(rw_ref[...], v)
    gates_t = _route(logits_t, rb_ref[...], n_experts)
    pad = jnp.zeros((LANES - n_experts, gates_t.shape[1]), F32)
    gates = jnp.concatenate([gates_t, pad], axis=0).T
    gate_ref[0] = gates
    cnt_ref[0, 0] = jnp.sum(jnp.where(gates > 0.0, 1.0, 0.0), axis=0, keepdims=True).astype(jnp.int32)


def _outproj(ya, hf, hb, bg, yc, yd, xs, modtab, w_out, gpost, gffn, rw_t, rbias, *, need_ctx):
    bsz, L, d = xs.shape
    n_experts = rw_t.shape[0]
    gw = ya.shape[-1]
    t0 = 0 if need_ctx else 1
    nt = L // ROW_TILE - t0
    rows_out = nt * ROW_TILE
    blk = lambda w: pl.BlockSpec((1, ROW_TILE, w), lambda b, t: (b, t + t0, 0))
    oblk = lambda w: pl.BlockSpec((1, ROW_TILE, w), lambda b, t: (b, t, 0))
    return pl.pallas_call(
        functools.partial(_outproj_kernel, n_experts=n_experts),
        grid=(bsz, nt),
        in_specs=[blk(gw)] * 6 + [blk(d),
                  pl.BlockSpec((1, 1, 6, d), lambda b, t: (b, jnp.minimum(t + t0, 1), 0, 0)),
                  _full(w_out), _full(gpost), _full(gffn), _full(rw_t), _full(rbias)],
        out_specs=[oblk(d), oblk(d), oblk(LANES),
                   pl.BlockSpec((1, 1, 1, LANES), lambda b, t: (b, t, 0, 0))],
        out_shape=[jax.ShapeDtypeStruct((bsz, rows_out, d), F32), jax.ShapeDtypeStruct((bsz, rows_out, d), BF16),
                   jax.ShapeDtypeStruct((bsz, rows_out, LANES), F32),
                   jax.ShapeDtypeStruct((bsz, nt, 1, LANES), jnp.int32)],
        compiler_params=_params("arbitrary", "arbitrary"),
    )(ya, hf, hb, bg, yc, yd, xs, modtab, w_out, gpost, gffn, rw_t, rbias)


SLOT_ALIGN = 16
EXPERT_BLOCK = 1024
SLOT_CHUNK = 1536
SHORT_BITS = 2
CODE_BASE = 64.0


def _swiglu(v, wg, wu, wd):
    hg = _dot(v, wg.astype(BF16))
    hu = _dot(v, wu.astype(BF16))
    return hg * jax.nn.sigmoid(hg) * hu, wd.astype(BF16)


def _slot_cap(n_experts):
    rows = ROW_TILE * TOP_K + n_experts * (SLOT_ALIGN - 1)
    return -(-rows // SLOT_CHUNK) * SLOT_CHUNK


def _slot_codes(gates, off_row, ltri):
    chosen = gates > 0.0
    rank = _dot(ltri, jnp.where(chosen, 1.0, 0.0).astype(BF16))
    code = jnp.where(chosen, off_row.astype(F32) + rank + 1.0, 0.0)
    hi = jnp.floor(code * (1.0 / CODE_BASE))
    return hi.astype(BF16), (code - CODE_BASE * hi).astype(BF16)


def _slot_expert_onehot(off_row, len_row, first, rows):
    r = first + lax.broadcasted_iota(jnp.int32, (rows, LANES), 0)
    inside = jnp.where(r >= off_row, jnp.where(r < off_row + len_row, 1.0, 0.0), 0.0)
    return inside.astype(BF16)


def _row_digits(first, shape, axis):
    code = (first + 1 + lax.broadcasted_iota(jnp.int32, shape, axis)).astype(F32)
    hi = jnp.floor(code * (1.0 / CODE_BASE))
    return hi, code - CODE_BASE * hi


def _slot_dma(src, dst, src_off, dst_off, n_units, sem, start):
    def piece(first, size):
        cp = pltpu.make_async_copy(
            src.at[pl.ds(pl.multiple_of(src_off + first, SLOT_ALIGN), size)],
            dst.at[pl.ds(pl.multiple_of(dst_off + first, SLOT_ALIGN), size)], sem)
        if start:
            cp.start()
        else:
            cp.wait()

    for bit in range(SHORT_BITS):
        pl.when(((n_units >> bit) & 1) == 1)(
            functools.partial(piece, (n_units & ((1 << bit) - 1)) * SLOT_ALIGN, SLOT_ALIGN << bit))

    long_rows = SLOT_ALIGN << SHORT_BITS
    short_rows = (n_units & ((1 << SHORT_BITS) - 1)) * SLOT_ALIGN

    def long_piece(j, carry):
        piece(short_rows + j * long_rows, long_rows)
        return carry
    lax.fori_loop(0, n_units >> SHORT_BITS, long_piece, 0)


def _wait_rows(src, dst, n_units, sem):
    for bit in range((min(src.shape[0], dst.shape[0]) // SLOT_ALIGN).bit_length()):
        size = SLOT_ALIGN << bit

        @pl.when(((n_units >> bit) & 1) == 1)
        def _piece():
            pltpu.make_async_copy(src.at[pl.ds(0, size)], dst.at[pl.ds(0, size)], sem).wait()


def _dispatch_kernel(pos_ref, off_ref, len_ref, tot_ref, tpos_ref, tlen_ref,
                     v_ref, g_ref, offv_ref, lenv_ref, ltri_ref, xs_ref,
                     buf_s, zero_s, sem, *, n_experts):
    i = pl.program_id(0)
    nt = pl.num_programs(0)
    cur = i % 2
    cap = buf_s.shape[1]

    @pl.when(i == 0)
    def _zero():
        zero_s[...] = jnp.zeros_like(zero_s)

    g = g_ref[...]
    off_row, len_row = offv_ref[0], lenv_ref[0]
    hi, lo = _slot_codes(g, off_row, ltri_ref[...])
    g_hi = g.astype(BF16)
    g_lo = (g - g_hi.astype(F32)).astype(BF16)
    half = LANES // 2
    src = jnp.concatenate([v_ref[...], g_hi[:, :half], g_lo[:, :half]], axis=1)
    for c in range(cap // SLOT_CHUNK):
        first = c * SLOT_CHUNK
        oh = _slot_expert_onehot(off_row, len_row, first, SLOT_CHUNK)
        rh, rl = _row_digits(first, (SLOT_CHUNK, 1), 0)
        pick = jnp.where(_dot_t(oh, hi) == rh, jnp.where(_dot_t(oh, lo) == rl, 1.0, 0.0), 0.0)
        buf_s[cur, first:first + SLOT_CHUNK, :] = _dot(pick.astype(BF16), src).astype(BF16)

    def slots(tile, which, start):
        def body(e, carry):
            k = tile * n_experts + e
            _slot_dma(buf_s.at[which], xs_ref, off_ref[k], pos_ref[k], len_ref[k], sem.at[0], start)
            return carry
        lax.fori_loop(0, n_experts, body, 0)

    def tails(start):
        def body(e, carry):
            _slot_dma(zero_s, xs_ref, 0, tpos_ref[e], tlen_ref[e], sem.at[0], start)
            return carry
        lax.fori_loop(0, n_experts, body, 0)

    @pl.when(i > 0)
    def _drain_previous():
        _wait_rows(buf_s.at[1 - cur], xs_ref, tot_ref[i - 1], sem.at[0])

    slots(i, cur, True)

    @pl.when(i == nt - 1)
    def _last():
        tails(True)
        _wait_rows(buf_s.at[cur], xs_ref, tot_ref[i], sem.at[0])
        tails(False)


def _dispatch(v2, gates2, plan, ltri, *, n_experts, n_rows):
    t, d = v2.shape
    nt = t // ROW_TILE
    cap = _slot_cap(n_experts)
    assert n_experts <= LANES // 2
    width = d + LANES
    grid_spec = pltpu.PrefetchScalarGridSpec(
        num_scalar_prefetch=6, grid=(nt,),
        in_specs=[pl.BlockSpec((ROW_TILE, d), lambda i, *_: (i, 0)),
                  pl.BlockSpec((ROW_TILE, LANES), lambda i, *_: (i, 0)),
                  pl.BlockSpec((1, 1, LANES), lambda i, *_: (i, 0, 0)),
                  pl.BlockSpec((1, 1, LANES), lambda i, *_: (i, 0, 0)),
                  pl.BlockSpec(ltri.shape, lambda i, *_: (0, 0))],
        out_specs=pl.BlockSpec(memory_space=pl.ANY),
        scratch_shapes=[pltpu.VMEM((2, cap, width), BF16),
                        pltpu.VMEM((EXPERT_BLOCK, width), BF16),
                        pltpu.SemaphoreType.DMA((1,))])
    return pl.pallas_call(
        functools.partial(_dispatch_kernel, n_experts=n_experts),
        grid_spec=grid_spec,
        out_shape=jax.ShapeDtypeStruct((n_rows, width), BF16),
        compiler_params=_params("arbitrary"),
    )(plan["pos"], plan["off"], plan["len"], plan["tile_len"], plan["tail_pos"], plan["tail_len"],
      v2, gates2, plan["off_v"], plan["len_v"], ltri)


def _expert_kernel(be_ref, na_ref, x_ref, wg_ref, wu_ref, wd_ref, y_ref, wg_s, wu_s, wd_s):
    b = pl.program_id(0)
    e = be_ref[b]

    @pl.when(jnp.logical_or(b == 0, e != be_ref[jnp.maximum(b - 1, 0)]))
    def _new_expert():
        wg_s[...] = wg_ref[0, 0].astype(BF16)
        wu_s[...] = wu_ref[0, 0].astype(BF16)
        wd_s[...] = wd_ref[0, 0].astype(BF16)

    @pl.when(b < na_ref[0])
    def _active():
        d = wg_s.shape[0]
        x = x_ref[:, 0:d]
        gates = x_ref[:, d:d + LANES].astype(F32)
        lane = lax.broadcasted_iota(jnp.int32, gates.shape, 1) & (LANES // 2 - 1)
        gcol = jnp.sum(jnp.where(lane == e, gates, 0.0), axis=1, keepdims=True)
        hg = _dot(x, wg_s[...])
        h = hg * jax.nn.sigmoid(hg) * _dot(x, wu_s[...])
        y_ref[...] = (_dot(h.astype(BF16), wd_s[...]) * gcol).astype(BF16)


def _experts(xs_sorted, plan, wg, wu, wd, layer):
    n_rows, width = xs_sorted.shape
    _, _, d, de = wg.shape
    nb = n_rows // EXPERT_BLOCK
    rowmap = lambda b, be, na: (jnp.minimum(b, na[0] - 1), 0)
    grid_spec = pltpu.PrefetchScalarGridSpec(
        num_scalar_prefetch=2, grid=(nb,),
        in_specs=[pl.BlockSpec((EXPERT_BLOCK, width), rowmap),
                  pl.BlockSpec((1, 1, d, de), lambda b, be, na: (layer, be[b], 0, 0)),
                  pl.BlockSpec((1, 1, d, de), lambda b, be, na: (layer, be[b], 0, 0)),
                  pl.BlockSpec((1, 1, de, d), lambda b, be, na: (layer, be[b], 0, 0))],
        out_specs=pl.BlockSpec((EXPERT_BLOCK, d), rowmap),
        scratch_shapes=[pltpu.VMEM((d, de), BF16), pltpu.VMEM((d, de), BF16), pltpu.VMEM((de, d), BF16)])
    return pl.pallas_call(
        _expert_kernel, grid_spec=grid_spec,
        out_shape=jax.ShapeDtypeStruct((n_rows, d), BF16),
        compiler_params=_params("arbitrary"),
    )(plan["block_expert"], plan["n_active"], xs_sorted, wg, wu, wd)


def _combine_kernel(pos_ref, off_ref, len_ref, tot_ref,
                    ys_ref, v_ref, g_ref, offv_ref, lenv_ref, ltri_ref, x_ref, mod_ref,
                    sg_ref, su_ref, sd_ref, gpost_ref, o_ref, buf_s, sem, *, n_experts):
    i = pl.program_id(0)
    nt = pl.num_programs(0)
    cur = i % 2
    cap = buf_s.shape[1]

    def slots(tile, which, start):
        def body(e, carry):
            k = tile * n_experts + e
            _slot_dma(ys_ref, buf_s.at[which], pos_ref[k], off_ref[k], len_ref[k], sem.at[which], start)
            return carry
        lax.fori_loop(0, n_experts, body, 0)

    @pl.when(i == 0)
    def _first():
        buf_s[...] = jnp.zeros_like(buf_s)
        slots(0, 0, True)

    @pl.when(i + 1 < nt)
    def _prefetch():
        slots(i + 1, 1 - cur, True)

    v = v_ref[...]
    h, wd = _swiglu(v, sg_ref[...], su_ref[...], sd_ref[...])
    acc = _dot(h.astype(BF16), wd)
    off_row, len_row = offv_ref[0], lenv_ref[0]
    hi, lo = _slot_codes(g_ref[...], off_row, ltri_ref[...])

    _wait_rows(ys_ref, buf_s.at[cur], tot_ref[i], sem.at[cur])
    for c in range(cap // SLOT_CHUNK):
        first = c * SLOT_CHUNK
        oh = _slot_expert_onehot(off_row, len_row, first, SLOT_CHUNK)
        rh, rl = _row_digits(first, (1, SLOT_CHUNK), 1)
        pick = jnp.where(_dot_t(hi, oh) == rh, jnp.where(_dot_t(lo, oh) == rl, 1.0, 0.0), 0.0)
        acc = acc + _dot(pick.astype(BF16), buf_s[cur, first:first + SLOT_CHUNK, :])
    o_ref[0] = x_ref[0] + mod_ref[0, 0][5:6] * _rms(acc, gpost_ref[...])


def _combine(ys_sorted, v2, gates2, plan, ltri, xs, modtab, sg, su, sd, gpost, *, n_experts, has_ctx):
    bsz, L, d = xs.shape
    tpb = L // ROW_TILE
    nt = bsz * tpb
    cap = _slot_cap(n_experts)
    full = lambda a: pl.BlockSpec(a.shape, lambda i, *_: (0,) * a.ndim)
    grid_spec = pltpu.PrefetchScalarGridSpec(
        num_scalar_prefetch=4, grid=(nt,),
        in_specs=[pl.BlockSpec(memory_space=pl.ANY),
                  pl.BlockSpec((ROW_TILE, d), lambda i, *_: (i, 0)),
                  pl.BlockSpec((ROW_TILE, LANES), lambda i, *_: (i, 0)),
                  pl.BlockSpec((1, 1, LANES), lambda i, *_: (i, 0, 0)),
                  pl.BlockSpec((1, 1, LANES), lambda i, *_: (i, 0, 0)),
                  full(ltri),
                  pl.BlockSpec((1, ROW_TILE, d), lambda i, *_: (i // tpb, i % tpb, 0)),
                  pl.BlockSpec((1, 1, 6, d),
                               lambda i, *_: (i // tpb, jnp.minimum(i % tpb, 1) if has_ctx else 1, 0, 0)),
                  full(sg), full(su), full(sd), full(gpost)],
        out_specs=pl.BlockSpec((1, ROW_TILE, d), lambda i, *_: (i // tpb, i % tpb, 0)),
        scratch_shapes=[pltpu.VMEM((2, cap, d), BF16), pltpu.SemaphoreType.DMA((2,))])
    return pl.pallas_call(
        functools.partial(_combine_kernel, n_experts=n_experts),
        grid_spec=grid_spec,
        out_shape=jax.ShapeDtypeStruct((bsz, L, d), F32),
        compiler_params=_params("arbitrary"),
    )(plan["pos"], plan["off"], plan["len"], plan["tile_len"], ys_sorted, v2, gates2, plan["off_v"], plan["len_v"], ltri,
      xs, modtab, sg, su, sd, gpost)


def _moe_plan(counts, n_experts, n_blocks):
    a = (counts + (SLOT_ALIGN - 1)) // SLOT_ALIGN * SLOT_ALIGN
    rows = jnp.sum(a, axis=0)
    region = (rows + (EXPERT_BLOCK - 1)) // EXPERT_BLOCK * EXPERT_BLOCK
    region_end = jnp.cumsum(region)
    region_start = region_end - region
    pos = region_start[None, :] + jnp.cumsum(a, axis=0) - a
    off = jnp.cumsum(a, axis=1) - a
    first_row = jnp.arange(n_blocks, dtype=jnp.int32) * EXPERT_BLOCK
    block_expert = jnp.minimum(jnp.sum(region_end[None, :n_experts] <= first_row[:, None], axis=1), n_experts - 1)
    flat = lambda t: t[:, :n_experts].reshape(-1).astype(jnp.int32)
    nt = counts.shape[0]
    return {
        "pos": flat(pos), "off": flat(off), "len": flat(a // SLOT_ALIGN),
        "tile_len": (jnp.sum(a, axis=1) // SLOT_ALIGN).astype(jnp.int32),
        "tail_pos": (region_start + rows)[:n_experts].astype(jnp.int32),
        "tail_len": ((region - rows) // SLOT_ALIGN)[:n_experts].astype(jnp.int32),
        "off_v": off.reshape(nt, 1, LANES).astype(jnp.int32),
        "len_v": a.reshape(nt, 1, LANES).astype(jnp.int32),
        "block_expert": block_expert.astype(jnp.int32),
        "n_active": (region_end[-1:] // EXPERT_BLOCK).astype(jnp.int32),
    }


def _moe(v, gates, counts, xs, modtab, wg, wu, wd, layer, sg, su, sd, gpost, *, has_ctx):
    bsz, L, d = xs.shape
    n_experts = wg.shape[1]
    t = bsz * L
    nt = t // ROW_TILE
    worst = t * TOP_K + nt * n_experts * (SLOT_ALIGN - 1) + n_experts * (EXPERT_BLOCK - 1)
    n_blocks = -(-worst // EXPERT_BLOCK)
    plan = _moe_plan(counts.reshape(nt, LANES), n_experts, n_blocks)
    ltri = jnp.tril(jnp.ones((ROW_TILE, ROW_TILE), F32), -1).astype(BF16)
    v2, gates2 = v.reshape(t, d), gates.reshape(t, LANES)
    xs_sorted = _dispatch(v2, gates2, plan, ltri, n_experts=n_experts, n_rows=n_blocks * EXPERT_BLOCK)
    ys_sorted = _experts(xs_sorted, plan, wg, wu, wd, layer)
    return _combine(ys_sorted, v2, gates2, plan, ltri, xs, modtab, sg, su, sd, gpost,
                    n_experts=n_experts, has_ctx=has_ctx)


def _tables(seq, ctx_len):
    def build(rot_dim, lead, slot):
        n = rot_dim // 4
        pos = jnp.arange(seq)
        row = (pos // GRID_W).astype(F32)
        col = (pos % GRID_W).astype(F32)
        inv = ROPE_THETA ** (-jnp.arange(n, dtype=F32) / n)
        ang = jnp.concatenate([row[:, None] * inv, col[:, None] * inv], axis=-1)
        cos, sin = jnp.cos(ang), jnp.sin(ang)
        zero = jnp.zeros_like(sin)
        tail = slot - lead - rot_dim
        one_l, zero_l = jnp.ones((seq, lead), F32), jnp.zeros((seq, lead), F32)
        one_t, zero_t = jnp.ones((seq, tail), F32), jnp.zeros((seq, tail), F32)
        c = jnp.concatenate([one_l, cos, cos, one_t], axis=1)
        s_lo = jnp.concatenate([zero_l, -sin, zero, zero_t], axis=1)
        s_hi = jnp.concatenate([zero_l, zero, sin, zero_t], axis=1)
        ctx_c = jnp.ones((ctx_len, slot), F32)
        ctx_s = jnp.zeros((ctx_len, slot), F32)
        return [jnp.concatenate([ctx_c, c], axis=0), jnp.concatenate([ctx_s, s_lo], axis=0),
                jnp.concatenate([ctx_s, s_hi], axis=0)]

    hd = [jnp.concatenate([t, t], axis=1) for t in build(HEAD_DIM, 0, HEAD_DIM)]
    mla = build(QK_ROPE, QK_NOPE, LANES)
    return hd + mla


def kernel(x, c, ctx, c_ctx, w_mod, b_mod, g_pre_mix, g_post_mix, g_pre_ffn, g_post_ffn, w_in, a_q_norm, a_k_norm, conv_w, conv_b, lru_wa, lru_ba, lru_wi, lru_bi, lru_lambda, c_sink, d_q_norm, d_w_uq, d_kv_norm, d_w_ukv, w_out, router_w, router_bias, w_gate, w_up, w_down, sh_gate, sh_up, sh_down):
    bsz, seq, d = x.shape
    ctx_len = ctx.shape[1]
    depth = w_mod.shape[0]
    gw = d // 4
    lru_w = conv_w.shape[-1]
    q_lora = d_q_norm.shape[-1]
    kv_lora = d_kv_norm.shape[-1]
    n_experts = router_w.shape[-1]
    a_heads = gw // HEAD_DIM
    a_kv = (w_in.shape[-1] - (2 * gw + 2 * lru_w + q_lora + kv_lora + QK_ROPE)) // (4 * HEAD_DIM)
    mla_scale = (QK_NOPE + QK_ROPE) ** -0.5
    v_dim = gw // D_HEADS
    assert ctx_len == ROW_TILE and a_kv == 2 and a_heads == 4 and n_experts <= LANES

    cond = jnp.zeros((16, d), F32).at[:bsz].set(c).at[bsz].set(c_ctx)
    mod = _modulation(cond, w_mod, b_mod).reshape(depth, 16, 6, d)
    tabs = _tables(seq, ctx_len)
    seg = jnp.kron(jnp.eye(a_heads, dtype=F32), jnp.ones((HEAD_DIM, HEAD_DIM), F32)).astype(BF16)

    xs = jnp.concatenate([ctx, x], axis=1)
    for l in range(depth):
        modtab = jnp.stack([jnp.broadcast_to(mod[l, bsz], (bsz, 6, d)), mod[l, :bsz]], axis=1)

        dq_end = 2 * gw + 2 * lru_w + 4 * a_kv * HEAD_DIM + q_lora
        zeros = lambda n: jnp.zeros((d, n), F32)
        w_in_p = jnp.concatenate([w_in[l][:, :dq_end], zeros(2 * LANES - q_lora),
                                  w_in[l][:, dq_end:], zeros(LANES - QK_ROPE)], axis=1).astype(BF16)

        qk = QK_NOPE + QK_ROPE
        wuq = d_w_uq[l].reshape(q_lora, D_HEADS, qk)
        wuq = jnp.pad(wuq, ((0, 2 * LANES - q_lora), (0, 0), (0, LANES - qk))).reshape(2 * LANES, D_HEADS * LANES)
        wukv = d_w_ukv[l].reshape(kv_lora, D_HEADS, QK_NOPE + v_dim)
        wk = jnp.pad(wukv[:, :, :QK_NOPE], ((0, 0), (0, 0), (0, LANES - QK_NOPE))).reshape(kv_lora, D_HEADS * LANES)
        wv = wukv[:, :, QK_NOPE:].reshape(kv_lora, D_HEADS * v_dim)
        wukv_p = jnp.concatenate([wk, wv], axis=1).astype(BF16)
        dqn = jnp.pad(d_q_norm[l], (0, 2 * LANES - q_lora)).reshape(1, 2 * LANES)

        qa, ka, va, bx, bg, qc, kc, vc, qd, kd, vd = _inproj(
            xs, modtab, g_pre_mix[l].reshape(1, d), w_in_p, seg, tabs,
            jnp.tile(a_q_norm[l], a_heads).reshape(1, gw), jnp.tile(a_k_norm[l], a_kv).reshape(1, gw // 2),
            dqn, wuq.astype(BF16), d_kv_norm[l].reshape(1, kv_lora), wukv_p,
            q_lora=q_lora, mla_scale=mla_scale, ctx_len=ctx_len)

        need_ctx = l < depth - 1
        ya = _attention(qa, ka, va, split_q=True, ctx_len=ctx_len, need_ctx=need_ctx)
        yd = _attention(qd, kd, vd, split_q=False, ctx_len=ctx_len, need_ctx=need_ctx)
        yc = _window_attention(qc, kc, vc, c_sink[l], ctx_len=ctx_len, need_ctx=need_ctx)

        blocks = lru_wa.shape[2]
        bdiag = lambda wts: jnp.stack([jax.scipy.linalg.block_diag(*[wts[dd, h] for h in range(blocks)])
                                       for dd in range(2)]).astype(BF16)
        hf, hb = _lru(bx, conv_w[l], conv_b[l].reshape(1, lru_w), bdiag(lru_wa[l]), lru_ba[l].reshape(2, 1, lru_w),
                      bdiag(lru_wi[l]), lru_bi[l].reshape(2, 1, lru_w), lru_lambda[l].reshape(2, 1, lru_w),
                      ctx_len=ctx_len)

        rw_t = router_w[l].T.astype(BF16)
        xs_mid, v, gates, counts = _outproj(ya, hf, hb, bg, yc, yd, xs, modtab, w_out[l].astype(BF16),
                                            g_post_mix[l].reshape(1, d), g_pre_ffn[l].reshape(1, d),
                                            rw_t, router_bias[l].reshape(n_experts, 1), need_ctx=need_ctx)
        xs = _moe(v, gates, counts, xs_mid, modtab, w_gate, w_up, w_down, l,
                  sh_gate[l], sh_up[l], sh_down[l], g_post_ffn[l].reshape(1, d), has_ctx=need_ctx)
    return xs
```

```python
import functools

import jax
import jax.numpy as jnp
from jax import lax
from jax.experimental import pallas as pl
from jax.experimental.pallas import tpu as pltpu

F32 = jnp.float32
BF16 = jnp.bfloat16

GRID_W = 64
HEAD_DIM = 64
ROPE_THETA = 10000.0
NORM_EPS = 1e-6
WINDOW = 128
WIN_QUERIES = 256
D_HEADS = 4
QK_NOPE = 64
QK_ROPE = 32
LRU_C = 8.0
CONV_W = 4
TOP_K = 8
N_EXPERT_GROUPS = 8
TOPK_GROUPS = 4
ROUTED_SCALE = 2.5

LANES = 128
VMEM_LIMIT = 56 * 1024 * 1024
ROW_TILE = 256
INPROJ_ROWS = 768

NEG_INF = float("-inf")
LOG2E = 1.4426950408889634


def _params(*sem):
    return pltpu.CompilerParams(dimension_semantics=sem, vmem_limit_bytes=VMEM_LIMIT)


def _rms(x, gain):
    return x * lax.rsqrt(jnp.mean(x * x, axis=-1, keepdims=True) + NORM_EPS) * gain


def _dot(a, b):
    return jnp.dot(a, b, preferred_element_type=F32)


def _dot_t(a, b):
    return lax.dot_general(a, b, (((1,), (1,)), ((), ())), preferred_element_type=F32)


def _full(a):
    return pl.BlockSpec(a.shape, lambda *_: (0,) * a.ndim)


def _mod_kernel(c_ref, w_ref, b_ref, o_ref):
    a = c_ref[...]
    a = a * jax.nn.sigmoid(a)
    o_ref[0] = _dot(a.astype(BF16), w_ref[0].astype(BF16)) + b_ref[0]


def _modulation(cond, w_mod, b_mod):
    depth, d, n = w_mod.shape
    tn = n // 4
    rows = cond.shape[0]
    return pl.pallas_call(
        _mod_kernel,
        grid=(depth, n // tn),
        in_specs=[pl.BlockSpec((rows, d), lambda l, j: (0, 0)),
                  pl.BlockSpec((1, d, tn), lambda l, j: (l, 0, j)),
                  pl.BlockSpec((1, 1, tn), lambda l, j: (l, 0, j))],
        out_specs=pl.BlockSpec((1, rows, tn), lambda l, j: (l, 0, j)),
        out_shape=jax.ShapeDtypeStruct((depth, rows, n), F32),
        compiler_params=_params("arbitrary", "arbitrary"),
    )(cond, w_mod, b_mod.reshape(depth, 1, n))


def _rope(t, cos, sin_lo, sin_hi, half):
    w = t.shape[-1]
    return t * cos + pltpu.roll(t, w - half, 1) * sin_lo + pltpu.roll(t, half, 1) * sin_hi


def _inproj_kernel(x_ref, mod_ref, gpre_ref, w_ref, seg_ref,
                   cos_ref, sinl_ref, sinh_ref, cosd_ref, sindl_ref, sindh_ref,
                   aqn_ref, akn_ref, dqn_ref, wuq_ref, dkvn_ref, wukv_ref,
                   qa_ref, ka_ref, va_ref, bx_ref, bg_ref, qc_ref, kc_ref, vc_ref,
                   qd_ref, kd_ref, vd_ref, *, q_lora, mla_scale, ctx_len):
    x = x_ref[0]
    rows = x.shape[0]
    mods = mod_ref[0]
    is_ctx = pl.program_id(1) * rows + lax.broadcasted_iota(jnp.int32, (rows, 1), 0) < ctx_len
    shift = jnp.where(is_ctx, mods[0, 0:1], mods[1, 0:1])
    scale = jnp.where(is_ctx, mods[0, 1:2], mods[1, 1:2])
    u = (_rms(x, gpre_ref[...]) * (1.0 + scale) + shift).astype(BF16)
    proj = _dot(u, w_ref[...])

    seg = seg_ref[...]
    lane = lax.broadcasted_iota(jnp.int32, (1, LANES), 1)
    low = lane < HEAD_DIM
    cos1, sin_lo1, sin_hi1 = cos_ref[...], sinl_ref[...], sinh_ref[...]
    cosd1, sind_lo1, sind_hi1 = cosd_ref[...], sindl_ref[...], sindh_ref[...]

    def head_rms(t, gain):
        w = t.shape[-1]
        sq = t * t
        hi = sq.astype(BF16)
        lo = (sq - hi.astype(F32)).astype(BF16)
        ms = (_dot(hi, seg[0:w, 0:w]) + _dot(lo, seg[0:w, 0:w])) * (1.0 / HEAD_DIM)
        return t * lax.rsqrt(ms + NORM_EPS) * gain

    def rope_hd(t):
        n = t.shape[-1] // LANES
        return _rope(t, *(jnp.concatenate([a] * n, axis=1) for a in (cos1, sin_lo1, sin_hi1)), HEAD_DIM // 2)

    def rope_r(t):
        n = t.shape[-1] // LANES
        return _rope(t, *(jnp.concatenate([a] * n, axis=1) for a in (cosd1, sind_lo1, sind_hi1)), QK_ROPE // 2)

    def per_query_head(t):
        r = pltpu.roll(t, HEAD_DIM, 1)
        return jnp.concatenate([jnp.where(low, t, r), jnp.where(low, r, t)], axis=1)

    sc_hd = HEAD_DIM ** -0.5 * LOG2E
    gw = qa_ref.shape[-1]
    kvw = gw // 2
    aq, ak, av = proj[:, 0:gw], proj[:, gw:gw + kvw], proj[:, gw + kvw:2 * gw]
    bx, bg = proj[:, 2 * gw:3 * gw], proj[:, 3 * gw:4 * gw]
    cq_, ck, cv = proj[:, 4 * gw:5 * gw], proj[:, 5 * gw:5 * gw + kvw], proj[:, 5 * gw + kvw:6 * gw]
    dq, dkv, dkr = proj[:, 6 * gw:7 * gw], proj[:, 7 * gw:7 * gw + LANES], proj[:, 7 * gw + LANES:8 * gw]
    qa_ref[0] = (rope_hd(head_rms(aq, aqn_ref[...])) * sc_hd).astype(BF16)
    ka_ref[0] = per_query_head(rope_hd(head_rms(ak, akn_ref[...]))).astype(BF16)
    va_ref[0] = per_query_head(av).astype(BF16)
    bx_ref[0] = bx
    bg_ref[0] = bg
    qc_ref[0] = (rope_hd(cq_) * sc_hd).astype(BF16)
    kc_ref[0] = per_query_head(rope_hd(ck)).astype(BF16)
    vc_ref[0] = per_query_head(cv).astype(BF16)
    cq = dq * lax.rsqrt(jnp.sum(dq * dq, axis=-1, keepdims=True) * (1.0 / q_lora) + NORM_EPS) * dqn_ref[...]
    qd = _dot(cq.astype(BF16), wuq_ref[...])
    qd_ref[0] = (rope_r(qd) * (mla_scale * LOG2E)).astype(BF16)
    ckv = _rms(dkv, dkvn_ref[...])
    kv = _dot(ckv.astype(BF16), wukv_ref[...])
    kr = rope_r(pltpu.roll(dkr, QK_NOPE, 1))
    kd_ref[0] = (kv[:, 0:512] + jnp.concatenate([kr] * D_HEADS, axis=1)).astype(BF16)
    vd_ref[0] = kv[:, 512:768].astype(BF16)


def _inproj(xs, modtab, gpre, w_in_p, seg, tabs, aqn, akn, dqn, wuq, dkvn, wukv, *, q_lora, mla_scale, ctx_len):
    bsz, L, d = xs.shape
    rows = INPROJ_ROWS
    nt = L // rows
    rowblk = lambda w: pl.BlockSpec((1, rows, w), lambda b, t: (b, t, 0))
    tab = lambda a: pl.BlockSpec((rows, a.shape[1]), lambda b, t: (t, 0))
    out_w = [(256, BF16)] * 3 + [(256, F32)] * 2 + [(256, BF16)] * 3 + [(512, BF16), (512, BF16), (256, BF16)]
    return pl.pallas_call(
        functools.partial(_inproj_kernel, q_lora=q_lora, mla_scale=mla_scale, ctx_len=ctx_len),
        grid=(bsz, nt),
        in_specs=[rowblk(d),
                  pl.BlockSpec((1, 2, 6, d), lambda b, t: (b, 0, 0, 0)),
                  _full(gpre), _full(w_in_p), _full(seg)] + [tab(a) for a in tabs]
                 + [_full(a) for a in (aqn, akn, dqn, wuq, dkvn, wukv)],
        out_specs=[rowblk(w) for w, _ in out_w],
        out_shape=[jax.ShapeDtypeStruct((bsz, L, w), dt) for w, dt in out_w],
        compiler_params=_params("arbitrary", "arbitrary"),
    )(xs, modtab, gpre, w_in_p, seg, *tabs, aqn, akn, dqn, wuq, dkvn, wukv)


def _attn_kernel(q_ref, k_ref, v_ref, o_ref, *, split_q, ctx_len, k_len, first_tile):
    lane = lax.broadcasted_iota(jnp.int32, (1, LANES), 1)
    low = lane < HEAD_DIM

    def run(n_keys):
        for g in range(v_ref.shape[-1] // LANES):
            outs = []
            for h in range(2):
                if split_q:
                    lanes = slice(g * LANES, (g + 1) * LANES)
                    keep = low if h == 0 else jnp.logical_not(low)
                    q = q_ref[0, :, lanes]
                    q = jnp.where(keep, q, jnp.zeros_like(q))
                else:
                    lanes = slice((2 * g + h) * LANES, (2 * g + h + 1) * LANES)
                    q = q_ref[0, :, lanes]
                s = _dot_t(q, k_ref[0, 0:n_keys, lanes])
                p = jnp.exp2(s - jnp.max(s, axis=-1, keepdims=True))
                den = jnp.sum(p, axis=-1, keepdims=True)
                outs.append(_dot(p.astype(BF16), v_ref[0, 0:n_keys, g * LANES:(g + 1) * LANES]) / den)
            o_ref[0, :, g * LANES:(g + 1) * LANES] = jnp.where(low, outs[0], outs[1]).astype(o_ref.dtype)

    if first_tile > 0:
        run(k_len)
        return
    t = pl.program_id(1)

    @pl.when(t == 0)
    def _ctx():
        run(ctx_len)

    @pl.when(t > 0)
    def _lat():
        run(k_len)


def _attention(q, k, v, *, split_q, ctx_len, need_ctx):
    bsz, L, qw = q.shape
    vw = v.shape[-1]
    tq = ROW_TILE
    assert ctx_len == tq
    t0 = 0 if need_ctx else 1
    return pl.pallas_call(
        functools.partial(_attn_kernel, split_q=split_q, ctx_len=ctx_len, k_len=L, first_tile=t0),
        grid=(bsz, L // tq - t0),
        in_specs=[pl.BlockSpec((1, tq, qw), lambda b, t: (b, t + t0, 0)),
                  pl.BlockSpec((1, L, qw), lambda b, t: (b, 0, 0)),
                  pl.BlockSpec((1, L, vw), lambda b, t: (b, 0, 0))],
        out_specs=pl.BlockSpec((1, tq, vw), lambda b, t: (b, t + t0, 0)),
        out_shape=jax.ShapeDtypeStruct((bsz, L, vw), BF16),
        compiler_params=_params("arbitrary", "arbitrary"),
    )(q, k, v)


def _winattn_kernel(sink_ref, q_ref, k_ref, v_ref, o_ref, *, ctx_len, seq, first_block):
    t = pl.program_id(1) + first_block
    lane = lax.broadcasted_iota(jnp.int32, (1, LANES), 1)
    low = lane < HEAD_DIM
    ctx_blocks = ctx_len // WIN_QUERIES
    win = WIN_QUERIES + 2 * WINDOW
    n_groups = q_ref.shape[-1] // LANES

    def finish(parts, sink):
        m = sink
        for s, _ in parts:
            m = jnp.maximum(m, jnp.max(s, axis=-1, keepdims=True))
        den = jnp.exp2(sink - m)
        acc = None
        for s, vv in parts:
            p = jnp.exp2(s - m)
            den = den + jnp.sum(p, axis=-1, keepdims=True)
            o = _dot(p.astype(BF16), vv)
            acc = o if acc is None else acc + o
        return acc / den

    def heads(fn):
        tq = q_ref.shape[1]
        first_head = lax.broadcasted_iota(jnp.int32, (2 * tq, 1), 0) < tq
        for g in range(n_groups):
            lanes = slice(g * LANES, (g + 1) * LANES)
            q = q_ref[0, :, lanes]
            zero = jnp.zeros_like(q)
            q2 = jnp.concatenate([jnp.where(low, q, zero), jnp.where(low, zero, q)], axis=0)
            sink = jnp.where(first_head, sink_ref[2 * g], sink_ref[2 * g + 1]) * LOG2E
            o2 = fn(q2, sink, lanes)
            o_ref[0, :, lanes] = jnp.where(low, o2[0:tq], o2[tq:2 * tq]).astype(o_ref.dtype)

    def ctx_queries():
        def one(q, sink, lanes):
            return finish([(_dot_t(q, k_ref[0, 0:ctx_len, lanes]), v_ref[0, 0:ctx_len, lanes])], sink)
        heads(one)

    def latent_queries():
        n = t - ctx_blocks
        start = jnp.clip(n * WIN_QUERIES - WINDOW, 0, seq - win)
        off = pl.multiple_of(ctx_len + start, WINDOW)
        qrow = lax.broadcasted_iota(jnp.int32, (2 * WIN_QUERIES, win), 0)
        qpos = n * WIN_QUERIES + jnp.where(qrow < WIN_QUERIES, qrow, qrow - WIN_QUERIES)
        kpos = start + lax.broadcasted_iota(jnp.int32, (2 * WIN_QUERIES, win), 1)
        allowed = jnp.abs(kpos - qpos) <= WINDOW

        def one(q, sink, lanes):
            s_ctx = _dot_t(q, k_ref[0, 0:ctx_len, lanes])
            s_win = jnp.where(allowed, _dot_t(q, k_ref[0, pl.ds(off, win), lanes]), NEG_INF)
            return finish([(s_ctx, v_ref[0, 0:ctx_len, lanes]), (s_win, v_ref[0, pl.ds(off, win), lanes])], sink)
        heads(one)

    if first_block >= ctx_blocks:
        latent_queries()
    else:
        pl.when(t < ctx_blocks)(ctx_queries)
        pl.when(t >= ctx_blocks)(latent_queries)


def _window_attention(q, k, v, sink, *, ctx_len, need_ctx):
    bsz, L, w = q.shape
    b0 = 0 if need_ctx else ctx_len // WIN_QUERIES
    return pl.pallas_call(
        functools.partial(_winattn_kernel, ctx_len=ctx_len, seq=L - ctx_len, first_block=b0),
        grid=(bsz, L // WIN_QUERIES - b0),
        in_specs=[pl.BlockSpec(memory_space=pltpu.SMEM),
                  pl.BlockSpec((1, WIN_QUERIES, w), lambda b, t: (b, t + b0, 0)),
                  pl.BlockSpec((1, L, w), lambda b, t: (b, 0, 0)),
                  pl.BlockSpec((1, L, w), lambda b, t: (b, 0, 0))],
        out_specs=pl.BlockSpec((1, WIN_QUERIES, w), lambda b, t: (b, t + b0, 0)),
        out_shape=jax.ShapeDtypeStruct((bsz, L, w), BF16),
        compiler_params=_params("arbitrary", "arbitrary"),
    )(sink, q, k, v)


def _lru_kernel(xf_ref, xfp_ref, xfn_ref, xb_ref, xbp_ref, xbn_ref,
                cw_ref, cb_ref, wa_ref, ba_ref, wi_ref, bi_ref, lam_ref,
                hf_ref, hb_ref,
                af_s, bf_s, ab_s, bb_s, of_s, ob_s, sf_s, sb_s, *, n_chunks):
    i = pl.program_id(0)
    bsz, tc, w = xf_ref.shape
    rows = bsz * tc
    ti = lax.broadcasted_iota(jnp.int32, (bsz, tc, w), 1)

    def coeffs(x_ref, prev_ref, next_ref, chunk, d, a_s, b_s):
        has_prev = (chunk >= 2).astype(F32)
        has_next = jnp.logical_and(chunk >= 1, chunk <= n_chunks - 2).astype(F32)
        x = x_ref[...]
        p1 = prev_ref[:, 7:8, :] * has_prev
        n0 = next_ref[:, 0:1, :] * has_next
        n1 = next_ref[:, 1:2, :] * has_next
        x2 = x.reshape(rows, w)
        xm1 = jnp.where(ti == 0, p1, pltpu.roll(x2, 1, 0).reshape(bsz, tc, w))
        xp1 = jnp.where(ti == tc - 1, n0, pltpu.roll(x2, rows - 1, 0).reshape(bsz, tc, w))
        xp2 = jnp.where(ti == tc - 1, n1,
                        jnp.where(ti == tc - 2, n0, pltpu.roll(x2, rows - 2, 0).reshape(bsz, tc, w)))
        cw = cw_ref[...]
        xc = (cb_ref[...] + xm1 * cw[0:1] + x * cw[1:2] + xp1 * cw[2:3] + xp2 * cw[3:4]).reshape(rows, w)
        xcb = xc.astype(BF16)
        r = 0.5 * jnp.tanh(0.5 * (_dot(xcb, wa_ref[d]) + ba_ref[d])) + 0.5
        ig = 0.5 * jnp.tanh(0.5 * (_dot(xcb, wi_ref[d]) + bi_ref[d])) + 0.5
        log_a = (-LRU_C) * r * jax.nn.softplus(-lam_ref[d])
        a = jnp.exp(log_a)
        b = jnp.sqrt(1.0 - a * a) * (ig * xc)
        for j in range(w // LANES):
            a_s[j] = a[:, j * LANES:(j + 1) * LANES]
            b_s[j] = b[:, j * LANES:(j + 1) * LANES]

    chunk_b = jnp.where(i == 0, 0, n_chunks - i)
    coeffs(xf_ref, xfp_ref, xfn_ref, i, 0, af_s, bf_s)
    coeffs(xb_ref, xbp_ref, xbn_ref, chunk_b, 1, ab_s, bb_s)

    @pl.when(i == 0)
    def _init():
        sf_s[...] = jnp.zeros_like(sf_s)
        sb_s[...] = jnp.zeros_like(sb_s)

    nl = w // LANES

    def step(t, carry):
        fwd = pl.ds(t, bsz, stride=tc)
        bwd = pl.ds(tc - 1 - t, bsz, stride=tc)
        out = []
        for j in range(nl):
            hf = af_s[j, fwd, :] * carry[j] + bf_s[j, fwd, :]
            hb = ab_s[j, bwd, :] * carry[nl + j] + bb_s[j, bwd, :]
            of_s[j, fwd, :] = hf
            ob_s[j, bwd, :] = hb
            out.append((hf, hb))
        return tuple(o[0] for o in out) + tuple(o[1] for o in out)

    init = tuple(sf_s[j] for j in range(nl)) + tuple(sb_s[j] for j in range(nl))
    fin = lax.fori_loop(0, tc, step, init, unroll=8)
    for j in range(nl):
        sf_s[j] = fin[j]
        sb_s[j] = fin[nl + j]
    hf_ref[...] = jnp.concatenate([of_s[j] for j in range(nl)], axis=1).reshape(bsz, tc, w)
    hb_ref[...] = jnp.concatenate([ob_s[j] for j in range(nl)], axis=1).reshape(bsz, tc, w)


def _lru(bx, conv_w, conv_b, wa, ba, wi, bi, lam, *, ctx_len):
    bsz, L, w = bx.shape
    tc = ctx_len
    nc = L // tc
    hb_blocks = tc // 8
    last8 = L // 8 - 1

    def fchunk(i):
        return i

    def bchunk(i):
        return jnp.where(i == 0, 0, nc - i)

    def cur(cf):
        return pl.BlockSpec((bsz, tc, w), lambda i: (0, cf(i), 0))

    def prev(cf):
        return pl.BlockSpec((bsz, 8, w), lambda i: (0, jnp.maximum(cf(i) * hb_blocks - 1, 0), 0))

    def nxt(cf):
        return pl.BlockSpec((bsz, 8, w), lambda i: (0, jnp.minimum((cf(i) + 1) * hb_blocks, last8), 0))

    small = [conv_w, conv_b, wa, ba, wi, bi, lam]
    nl = w // LANES
    scr = [pltpu.VMEM((nl, bsz * tc, LANES), F32)] * 6 + [pltpu.VMEM((nl, bsz, LANES), F32)] * 2
    return pl.pallas_call(
        functools.partial(_lru_kernel, n_chunks=nc),
        grid=(nc,),
        in_specs=[cur(fchunk), prev(fchunk), nxt(fchunk), cur(bchunk), prev(bchunk), nxt(bchunk)]
                 + [_full(a) for a in small],
        out_specs=[cur(fchunk), cur(bchunk)],
        out_shape=[jax.ShapeDtypeStruct((bsz, L, w), F32)] * 2,
        scratch_shapes=scr,
        compiler_params=_params("arbitrary"),
    )(bx, bx, bx, bx, bx, bx, *small)


def _route(logits_t, bias, n_experts):
    per = n_experts // N_EXPERT_GROUPS
    tm = logits_t.shape[-1]
    scores = jax.nn.sigmoid(logits_t).reshape(N_EXPERT_GROUPS, per, tm)
    sel = scores + bias.reshape(N_EXPERT_GROUPS, per, 1)
    shape = sel.shape
    gi = lax.broadcasted_iota(jnp.int32, shape, 0)
    mi = lax.broadcasted_iota(jnp.int32, shape, 1)
    ei = gi * per + mi
    m1 = jnp.max(sel, axis=1, keepdims=True)
    first = jnp.min(jnp.where(sel == m1, mi, per), axis=1, keepdims=True)
    m2 = jnp.max(jnp.where(mi == first, NEG_INF, sel), axis=1, keepdims=True)
    gscore = m1 + m2
    gidx = lax.broadcasted_iota(jnp.int32, gscore.shape, 0)
    gmask = jnp.zeros(gscore.shape, F32)
    for _ in range(TOPK_GROUPS):
        m = jnp.max(gscore, axis=0, keepdims=True)
        pick = jnp.min(jnp.where(gscore == m, gidx, N_EXPERT_GROUPS), axis=0, keepdims=True)
        hit = gidx == pick
        gmask = jnp.where(hit, 1.0, gmask)
        gscore = jnp.where(hit, NEG_INF, gscore)
    cand = jnp.where(gmask > 0.0, sel, NEG_INF)
    chosen = jnp.zeros(shape, F32)
    for _ in range(TOP_K):
        m = jnp.max(jnp.max(cand, axis=1, keepdims=True), axis=0, keepdims=True)
        pick = jnp.where(cand == m, ei, n_experts)
        pick = jnp.min(jnp.min(pick, axis=1, keepdims=True), axis=0, keepdims=True)
        hit = ei == pick
        chosen = jnp.where(hit, 1.0, chosen)
        cand = jnp.where(hit, NEG_INF, cand)
    wsel = jnp.where(chosen > 0.0, scores, 0.0)
    den = jnp.sum(jnp.sum(wsel, axis=1, keepdims=True), axis=0, keepdims=True)
    return (wsel / den * ROUTED_SCALE).reshape(n_experts, tm)


def _outproj_kernel(ya_ref, hf_ref, hb_ref, bg_ref, yc_ref, yd_ref, x_ref, mod_ref,
                    wout_ref, gpost_ref, gffn_ref, rw_ref, rb_ref,
                    xo_ref, v_ref, gate_ref, cnt_ref, *, n_experts):
    gw = ya_ref.shape[-1]
    m = mod_ref[0, 0]
    yb = ((hf_ref[0] + hb_ref[0]) * jax.nn.gelu(bg_ref[0])).astype(BF16)
    y = (_dot(ya_ref[0], wout_ref[0:gw, :]) + _dot(yb, wout_ref[gw:2 * gw, :])
         + _dot(yc_ref[0], wout_ref[2 * gw:3 * gw, :]) + _dot(yd_ref[0], wout_ref[3 * gw:4 * gw, :]))
    x1 = x_ref[0] + m[2:3] * _rms(y, gpost_ref[...])
    xo_ref[0] = x1
    v = (_rms(x1, gffn_ref[...]) * (1.0 + m[4:5]) + m[3:4]).astype(BF16)
    v_ref[0] = v
    logits_t = _dot_t(rw_ref[...], v)
    gates_t = _route(logits_t, rb_ref[...], n_experts)
    pad = jnp.zeros((LANES - n_experts, gates_t.shape[1]), F32)
    gates = jnp.concatenate([gates_t, pad], axis=0).T
    gate_ref[0] = gates
    cnt_ref[0, 0] = jnp.sum(jnp.where(gates > 0.0, 1.0, 0.0), axis=0, keepdims=True).astype(jnp.int32)


def _outproj(ya, hf, hb, bg, yc, yd, xs, modtab, w_out, gpost, gffn, rw_t, rbias, *, need_ctx):
    bsz, L, d = xs.shape
    n_experts = rw_t.shape[0]
    gw = ya.shape[-1]
    t0 = 0 if need_ctx else 1
    nt = L // ROW_TILE - t0
    rows_out = nt * ROW_TILE
    blk = lambda w: pl.BlockSpec((1, ROW_TILE, w), lambda b, t: (b, t + t0, 0))
    oblk = lambda w: pl.BlockSpec((1, ROW_TILE, w), lambda b, t: (b, t, 0))
    return pl.pallas_call(
        functools.partial(_outproj_kernel, n_experts=n_experts),
        grid=(bsz, nt),
        in_specs=[blk(gw)] * 6 + [blk(d),
                  pl.BlockSpec((1, 1, 6, d), lambda b, t: (b, jnp.minimum(t + t0, 1), 0, 0)),
                  _full(w_out), _full(gpost), _full(gffn), _full(rw_t), _full(rbias)],
        out_specs=[oblk(d), oblk(d), oblk(LANES),
                   pl.BlockSpec((1, 1, 1, LANES), lambda b, t: (b, t, 0, 0))],
        out_shape=[jax.ShapeDtypeStruct((bsz, rows_out, d), F32), jax.ShapeDtypeStruct((bsz, rows_out, d), BF16),
                   jax.ShapeDtypeStruct((bsz, rows_out, LANES), F32),
                   jax.ShapeDtypeStruct((bsz, nt, 1, LANES), jnp.int32)],
        compiler_params=_params("arbitrary", "arbitrary"),
    )(ya, hf, hb, bg, yc, yd, xs, modtab, w_out, gpost, gffn, rw_t, rbias)


SLOT_ALIGN = 16
EXPERT_BLOCK = 1024
SLOT_CHUNK = 1536
SHORT_BITS = 2
CODE_BASE = 64.0


def _swiglu(v, wg, wu, wd):
    hg = _dot(v, wg.astype(BF16))
    hu = _dot(v, wu.astype(BF16))
    return hg * jax.nn.sigmoid(hg) * hu, wd.astype(BF16)


def _slot_cap(n_experts):
    rows = ROW_TILE * TOP_K + n_experts * (SLOT_ALIGN - 1)
    return -(-rows // SLOT_CHUNK) * SLOT_CHUNK


def _slot_codes(gates, off_row, ltri):
    chosen = gates > 0.0
    rank = _dot(ltri, jnp.where(chosen, 1.0, 0.0).astype(BF16))
    code = jnp.where(chosen, off_row.astype(F32) + rank + 1.0, 0.0)
    hi = jnp.floor(code * (1.0 / CODE_BASE))
    return hi.astype(BF16), (code - CODE_BASE * hi).astype(BF16)


def _slot_expert_onehot(off_row, len_row, first, rows):
    r = first + lax.broadcasted_iota(jnp.int32, (rows, LANES), 0)
    inside = jnp.where(r >= off_row, jnp.where(r < off_row + len_row, 1.0, 0.0), 0.0)
    return inside.astype(BF16)


def _row_digits(first, shape, axis):
    code = (first + 1 + lax.broadcasted_iota(jnp.int32, shape, axis)).astype(F32)
    hi = jnp.floor(code * (1.0 / CODE_BASE))
    return hi, code - CODE_BASE * hi


def _slot_dma(src, dst, src_off, dst_off, n_units, sem, start):
    def piece(first, size):
        cp = pltpu.make_async_copy(
            src.at[pl.ds(pl.multiple_of(src_off + first, SLOT_ALIGN), size)],
            dst.at[pl.ds(pl.multiple_of(dst_off + first, SLOT_ALIGN), size)], sem)
        if start:
            cp.start()
        else:
            cp.wait()

    for bit in range(SHORT_BITS):
        pl.when(((n_units >> bit) & 1) == 1)(
            functools.partial(piece, (n_units & ((1 << bit) - 1)) * SLOT_ALIGN, SLOT_ALIGN << bit))

    long_rows = SLOT_ALIGN << SHORT_BITS
    short_rows = (n_units & ((1 << SHORT_BITS) - 1)) * SLOT_ALIGN

    def long_piece(j, carry):
        piece(short_rows + j * long_rows, long_rows)
        return carry
    lax.fori_loop(0, n_units >> SHORT_BITS, long_piece, 0)


def _wait_rows(src, dst, n_units, sem):
    for bit in range((min(src.shape[0], dst.shape[0]) // SLOT_ALIGN).bit_length()):
        size = SLOT_ALIGN << bit

        @pl.when(((n_units >> bit) & 1) == 1)
        def _piece():
            pltpu.make_async_copy(src.at[pl.ds(0, size)], dst.at[pl.ds(0, size)], sem).wait()


def _dispatch_kernel(pos_ref, off_ref, len_ref, tot_ref, tpos_ref, tlen_ref,
                     v_ref, g_ref, offv_ref, lenv_ref, ltri_ref, xs_ref,
                     buf_s, zero_s, sem, *, n_experts):
    i = pl.program_id(0)
    nt = pl.num_programs(0)
    cur = i % 2
    cap = buf_s.shape[1]

    @pl.when(i == 0)
    def _zero():
        zero_s[...] = jnp.zeros_like(zero_s)

    g = g_ref[...]
    off_row, len_row = offv_ref[0], lenv_ref[0]
    hi, lo = _slot_codes(g, off_row, ltri_ref[...])
    g_hi = g.astype(BF16)
    g_lo = (g - g_hi.astype(F32)).astype(BF16)
    half = LANES // 2
    src = jnp.concatenate([v_ref[...], g_hi[:, :half], g_lo[:, :half]], axis=1)
    for c in range(cap // SLOT_CHUNK):
        first = c * SLOT_CHUNK
        oh = _slot_expert_onehot(off_row, len_row, first, SLOT_CHUNK)
        rh, rl = _row_digits(first, (SLOT_CHUNK, 1), 0)
        pick = jnp.where(_dot_t(oh, hi) == rh, jnp.where(_dot_t(oh, lo) == rl, 1.0, 0.0), 0.0)
        buf_s[cur, first:first + SLOT_CHUNK, :] = _dot(pick.astype(BF16), src).astype(BF16)

    def slots(tile, which, start):
        base = tile * n_experts
        entry = lambda k: (off_ref[k], pos_ref[k], len_ref[k])

        def body(e, cur):
            nxt = entry(base + jnp.minimum(e + 1, n_experts - 1))
            _slot_dma(buf_s.at[which], xs_ref, *cur, sem.at[0], start)
            return nxt
        lax.fori_loop(0, n_experts, body, entry(base))

    def tails(start):
        def body(e, carry):
            _slot_dma(zero_s, xs_ref, 0, tpos_ref[e], tlen_ref[e], sem.at[0], start)
            return carry
        lax.fori_loop(0, n_experts, body, 0)

    @pl.when(i > 0)
    def _drain_previous():
        _wait_rows(buf_s.at[1 - cur], xs_ref, tot_ref[i - 1], sem.at[0])

    slots(i, cur, True)

    @pl.when(i == nt - 1)
    def _last():
        tails(True)
        _wait_rows(buf_s.at[cur], xs_ref, tot_ref[i], sem.at[0])
        tails(False)


def _dispatch(v2, gates2, plan, ltri, *, n_experts, n_rows):
    t, d = v2.shape
    nt = t // ROW_TILE
    cap = _slot_cap(n_experts)
    assert n_experts <= LANES // 2
    width = d + LANES
    grid_spec = pltpu.PrefetchScalarGridSpec(
        num_scalar_prefetch=6, grid=(nt,),
        in_specs=[pl.BlockSpec((ROW_TILE, d), lambda i, *_: (i, 0)),
                  pl.BlockSpec((ROW_TILE, LANES), lambda i, *_: (i, 0)),
                  pl.BlockSpec((1, 1, LANES), lambda i, *_: (i, 0, 0)),
                  pl.BlockSpec((1, 1, LANES), lambda i, *_: (i, 0, 0)),
                  pl.BlockSpec(ltri.shape, lambda i, *_: (0, 0))],
        out_specs=pl.BlockSpec(memory_space=pl.ANY),
        scratch_shapes=[pltpu.VMEM((2, cap, width), BF16),
                        pltpu.VMEM((EXPERT_BLOCK, width), BF16),
                        pltpu.SemaphoreType.DMA((1,))])
    return pl.pallas_call(
        functools.partial(_dispatch_kernel, n_experts=n_experts),
        grid_spec=grid_spec,
        out_shape=jax.ShapeDtypeStruct((n_rows, width), BF16),
        compiler_params=_params("arbitrary"),
    )(plan["pos"], plan["off"], plan["len"], plan["tile_len"], plan["tail_pos"], plan["tail_len"],
      v2, gates2, plan["off_v"], plan["len_v"], ltri)


def _expert_kernel(be_ref, na_ref, x_ref, wg_ref, wu_ref, wd_ref, y_ref, wg_s, wu_s, wd_s):
    b = pl.program_id(0)
    e = be_ref[b]

    @pl.when(jnp.logical_or(b == 0, e != be_ref[jnp.maximum(b - 1, 0)]))
    def _new_expert():
        wg_s[...] = wg_ref[0, 0].astype(BF16)
        wu_s[...] = wu_ref[0, 0].astype(BF16)
        wd_s[...] = wd_ref[0, 0].astype(BF16)

    @pl.when(b < na_ref[0])
    def _active():
        d = wg_s.shape[0]
        x = x_ref[:, 0:d]
        gates = x_ref[:, d:d + LANES].astype(F32)
        lane = lax.broadcasted_iota(jnp.int32, gates.shape, 1) & (LANES // 2 - 1)
        gcol = jnp.sum(jnp.where(lane == e, gates, 0.0), axis=1, keepdims=True)
        hg = _dot(x, wg_s[...])
        h = hg * jax.nn.sigmoid(hg) * _dot(x, wu_s[...])
        y_ref[...] = (_dot(h.astype(BF16), wd_s[...]) * gcol).astype(BF16)


def _experts(xs_sorted, plan, wg, wu, wd, layer):
    n_rows, width = xs_sorted.shape
    _, _, d, de = wg.shape
    nb = n_rows // EXPERT_BLOCK
    rowmap = lambda b, be, na: (jnp.minimum(b, na[0] - 1), 0)
    grid_spec = pltpu.PrefetchScalarGridSpec(
        num_scalar_prefetch=2, grid=(nb,),
        in_specs=[pl.BlockSpec((EXPERT_BLOCK, width), rowmap),
                  pl.BlockSpec((1, 1, d, de), lambda b, be, na: (layer, be[b], 0, 0)),
                  pl.BlockSpec((1, 1, d, de), lambda b, be, na: (layer, be[b], 0, 0)),
                  pl.BlockSpec((1, 1, de, d), lambda b, be, na: (layer, be[b], 0, 0))],
        out_specs=pl.BlockSpec((EXPERT_BLOCK, d), rowmap),
        scratch_shapes=[pltpu.VMEM((d, de), BF16), pltpu.VMEM((d, de), BF16), pltpu.VMEM((de, d), BF16)])
    return pl.pallas_call(
        _expert_kernel, grid_spec=grid_spec,
        out_shape=jax.ShapeDtypeStruct((n_rows, d), BF16),
        compiler_params=_params("arbitrary"),
    )(plan["block_expert"], plan["n_active"], xs_sorted, wg, wu, wd)


def _combine_kernel(pos_ref, off_ref, len_ref, tot_ref,
                    ys_ref, v_ref, g_ref, offv_ref, lenv_ref, ltri_ref, x_ref, mod_ref,
                    sg_ref, su_ref, sd_ref, gpost_ref, o_ref, buf_s, sem, *, n_experts):
    i = pl.program_id(0)
    nt = pl.num_programs(0)
    cur = i % 2
    cap = buf_s.shape[1]

    def slots(tile, which, start):
        base = tile * n_experts
        entry = lambda k: (pos_ref[k], off_ref[k], len_ref[k])

        def body(e, cur):
            nxt = entry(base + jnp.minimum(e + 1, n_experts - 1))
            _slot_dma(ys_ref, buf_s.at[which], *cur, sem.at[which], start)
            return nxt
        lax.fori_loop(0, n_experts, body, entry(base))

    @pl.when(i == 0)
    def _first():
        buf_s[...] = jnp.zeros_like(buf_s)
        slots(0, 0, True)

    @pl.when(i + 1 < nt)
    def _prefetch():
        slots(i + 1, 1 - cur, True)

    v = v_ref[...]
    h, wd = _swiglu(v, sg_ref[...], su_ref[...], sd_ref[...])
    acc = _dot(h.astype(BF16), wd)
    off_row, len_row = offv_ref[0], lenv_ref[0]
    hi, lo = _slot_codes(g_ref[...], off_row, ltri_ref[...])

    _wait_rows(ys_ref, buf_s.at[cur], tot_ref[i], sem.at[cur])
    for c in range(cap // SLOT_CHUNK):
        first = c * SLOT_CHUNK
        oh = _slot_expert_onehot(off_row, len_row, first, SLOT_CHUNK)
        rh, rl = _row_digits(first, (1, SLOT_CHUNK), 1)
        pick = jnp.where(_dot_t(hi, oh) == rh, jnp.where(_dot_t(lo, oh) == rl, 1.0, 0.0), 0.0)
        acc = acc + _dot(pick.astype(BF16), buf_s[cur, first:first + SLOT_CHUNK, :])
    o_ref[0] = x_ref[0] + mod_ref[0, 0][5:6] * _rms(acc, gpost_ref[...])


def _combine(ys_sorted, v2, gates2, plan, ltri, xs, modtab, sg, su, sd, gpost, *, n_experts, has_ctx):
    bsz, L, d = xs.shape
    tpb = L // ROW_TILE
    nt = bsz * tpb
    cap = _slot_cap(n_experts)
    full = lambda a: pl.BlockSpec(a.shape, lambda i, *_: (0,) * a.ndim)
    grid_spec = pltpu.PrefetchScalarGridSpec(
        num_scalar_prefetch=4, grid=(nt,),
        in_specs=[pl.BlockSpec(memory_space=pl.ANY),
                  pl.BlockSpec((ROW_TILE, d), lambda i, *_: (i, 0)),
                  pl.BlockSpec((ROW_TILE, LANES), lambda i, *_: (i, 0)),
                  pl.BlockSpec((1, 1, LANES), lambda i, *_: (i, 0, 0)),
                  pl.BlockSpec((1, 1, LANES), lambda i, *_: (i, 0, 0)),
                  full(ltri),
                  pl.BlockSpec((1, ROW_TILE, d), lambda i, *_: (i // tpb, i % tpb, 0)),
                  pl.BlockSpec((1, 1, 6, d),
                               lambda i, *_: (i // tpb, jnp.minimum(i % tpb, 1) if has_ctx else 1, 0, 0)),
                  full(sg), full(su), full(sd), full(gpost)],
        out_specs=pl.BlockSpec((1, ROW_TILE, d), lambda i, *_: (i // tpb, i % tpb, 0)),
        scratch_shapes=[pltpu.VMEM((2, cap, d), BF16), pltpu.SemaphoreType.DMA((2,))])
    return pl.pallas_call(
        functools.partial(_combine_kernel, n_experts=n_experts),
        grid_spec=grid_spec,
        out_shape=jax.ShapeDtypeStruct((bsz, L, d), F32),
        compiler_params=_params("arbitrary"),
    )(plan["pos"], plan["off"], plan["len"], plan["tile_len"], ys_sorted, v2, gates2, plan["off_v"], plan["len_v"], ltri,
      xs, modtab, sg, su, sd, gpost)


def _moe_plan(counts, n_experts, n_blocks):
    a = (counts + (SLOT_ALIGN - 1)) // SLOT_ALIGN * SLOT_ALIGN
    rows = jnp.sum(a, axis=0)
    region = (rows + (EXPERT_BLOCK - 1)) // EXPERT_BLOCK * EXPERT_BLOCK
    region_end = jnp.cumsum(region)
    region_start = region_end - region
    pos = region_start[None, :] + jnp.cumsum(a, axis=0) - a
    off = jnp.cumsum(a, axis=1) - a
    first_row = jnp.arange(n_blocks, dtype=jnp.int32) * EXPERT_BLOCK
    block_expert = jnp.minimum(jnp.sum(region_end[None, :n_experts] <= first_row[:, None], axis=1), n_experts - 1)
    flat = lambda t: t[:, :n_experts].reshape(-1).astype(jnp.int32)
    nt = counts.shape[0]
    return {
        "pos": flat(pos), "off": flat(off), "len": flat(a // SLOT_ALIGN),
        "tile_len": (jnp.sum(a, axis=1) // SLOT_ALIGN).astype(jnp.int32),
        "tail_pos": (region_start + rows)[:n_experts].astype(jnp.int32),
        "tail_len": ((region - rows) // SLOT_ALIGN)[:n_experts].astype(jnp.int32),
        "off_v": off.reshape(nt, 1, LANES).astype(jnp.int32),
        "len_v": a.reshape(nt, 1, LANES).astype(jnp.int32),
        "block_expert": block_expert.astype(jnp.int32),
        "n_active": (region_end[-1:] // EXPERT_BLOCK).astype(jnp.int32),
    }


def _moe(v, gates, counts, xs, modtab, wg, wu, wd, layer, sg, su, sd, gpost, *, has_ctx):
    bsz, L, d = xs.shape
    n_experts = wg.shape[1]
    t = bsz * L
    nt = t // ROW_TILE
    worst = t * TOP_K + nt * n_experts * (SLOT_ALIGN - 1) + n_experts * (EXPERT_BLOCK - 1)
    n_blocks = -(-worst // EXPERT_BLOCK)
    plan = _moe_plan(counts.reshape(nt, LANES), n_experts, n_blocks)
    ltri = jnp.tril(jnp.ones((ROW_TILE, ROW_TILE), F32), -1).astype(BF16)
    v2, gates2 = v.reshape(t, d), gates.reshape(t, LANES)
    xs_sorted = _dispatch(v2, gates2, plan, ltri, n_experts=n_experts, n_rows=n_blocks * EXPERT_BLOCK)
    ys_sorted = _experts(xs_sorted, plan, wg, wu, wd, layer)
    return _combine(ys_sorted, v2, gates2, plan, ltri, xs, modtab, sg, su, sd, gpost,
                    n_experts=n_experts, has_ctx=has_ctx)


def _tables(seq, ctx_len):
    def build(rot_dim, lead, slot):
        n = rot_dim // 4
        pos = jnp.arange(seq)
        row = (pos // GRID_W).astype(F32)
        col = (pos % GRID_W).astype(F32)
        inv = ROPE_THETA ** (-jnp.arange(n, dtype=F32) / n)
        ang = jnp.concatenate([row[:, None] * inv, col[:, None] * inv], axis=-1)
        cos, sin = jnp.cos(ang), jnp.sin(ang)
        zero = jnp.zeros_like(sin)
        tail = slot - lead - rot_dim
        one_l, zero_l = jnp.ones((seq, lead), F32), jnp.zeros((seq, lead), F32)
        one_t, zero_t = jnp.ones((seq, tail), F32), jnp.zeros((seq, tail), F32)
        c = jnp.concatenate([one_l, cos, cos, one_t], axis=1)
        s_lo = jnp.concatenate([zero_l, -sin, zero, zero_t], axis=1)
        s_hi = jnp.concatenate([zero_l, zero, sin, zero_t], axis=1)
        ctx_c = jnp.ones((ctx_len, slot), F32)
        ctx_s = jnp.zeros((ctx_len, slot), F32)
        return [jnp.concatenate([ctx_c, c], axis=0), jnp.concatenate([ctx_s, s_lo], axis=0),
                jnp.concatenate([ctx_s, s_hi], axis=0)]

    hd = [jnp.concatenate([t, t], axis=1) for t in build(HEAD_DIM, 0, HEAD_DIM)]
    mla = build(QK_ROPE, QK_NOPE, LANES)
    return hd + mla


def kernel(x, c, ctx, c_ctx, w_mod, b_mod, g_pre_mix, g_post_mix, g_pre_ffn, g_post_ffn, w_in, a_q_norm, a_k_norm, conv_w, conv_b, lru_wa, lru_ba, lru_wi, lru_bi, lru_lambda, c_sink, d_q_norm, d_w_uq, d_kv_norm, d_w_ukv, w_out, router_w, router_bias, w_gate, w_up, w_down, sh_gate, sh_up, sh_down):
    bsz, seq, d = x.shape
    ctx_len = ctx.shape[1]
    depth = w_mod.shape[0]
    gw = d // 4
    lru_w = conv_w.shape[-1]
    q_lora = d_q_norm.shape[-1]
    kv_lora = d_kv_norm.shape[-1]
    n_experts = router_w.shape[-1]
    a_heads = gw // HEAD_DIM
    a_kv = (w_in.shape[-1] - (2 * gw + 2 * lru_w + q_lora + kv_lora + QK_ROPE)) // (4 * HEAD_DIM)
    mla_scale = (QK_NOPE + QK_ROPE) ** -0.5
    v_dim = gw // D_HEADS
    assert ctx_len == ROW_TILE and a_kv == 2 and a_heads == 4 and n_experts <= LANES

    cond = jnp.zeros((16, d), F32).at[:bsz].set(c).at[bsz].set(c_ctx)
    mod = _modulation(cond, w_mod, b_mod).reshape(depth, 16, 6, d)
    tabs = _tables(seq, ctx_len)
    seg = jnp.kron(jnp.eye(a_heads, dtype=F32), jnp.ones((HEAD_DIM, HEAD_DIM), F32)).astype(BF16)

    xs = jnp.concatenate([ctx, x], axis=1)
    for l in range(depth):
        modtab = jnp.stack([jnp.broadcast_to(mod[l, bsz], (bsz, 6, d)), mod[l, :bsz]], axis=1)

        dq_end = 2 * gw + 2 * lru_w + 4 * a_kv * HEAD_DIM + q_lora
        zeros = lambda n: jnp.zeros((d, n), F32)
        w_in_p = jnp.concatenate([w_in[l][:, :dq_end], zeros(2 * LANES - q_lora),
                                  w_in[l][:, dq_end:], zeros(LANES - QK_ROPE)], axis=1).astype(BF16)

        qk = QK_NOPE + QK_ROPE
        wuq = d_w_uq[l].reshape(q_lora, D_HEADS, qk)
        wuq = jnp.pad(wuq, ((0, 2 * LANES - q_lora), (0, 0), (0, LANES - qk))).reshape(2 * LANES, D_HEADS * LANES)
        wukv = d_w_ukv[l].reshape(kv_lora, D_HEADS, QK_NOPE + v_dim)
        wk = jnp.pad(wukv[:, :, :QK_NOPE], ((0, 0), (0, 0), (0, LANES - QK_NOPE))).reshape(kv_lora, D_HEADS * LANES)
        wv = wukv[:, :, QK_NOPE:].reshape(kv_lora, D_HEADS * v_dim)
        wukv_p = jnp.concatenate([wk, wv], axis=1).astype(BF16)
        dqn = jnp.pad(d_q_norm[l], (0, 2 * LANES - q_lora)).reshape(1, 2 * LANES)

        qa, ka, va, bx, bg, qc, kc, vc, qd, kd, vd = _inproj(
            xs, modtab, g_pre_mix[l].reshape(1, d), w_in_p, seg, tabs,
            jnp.tile(a_q_norm[l], a_heads).reshape(1, gw), jnp.tile(a_k_norm[l], a_kv).reshape(1, gw // 2),
            dqn, wuq.astype(BF16), d_kv_norm[l].reshape(1, kv_lora), wukv_p,
            q_lora=q_lora, mla_scale=mla_scale, ctx_len=ctx_len)

        need_ctx = l < depth - 1
        ya = _attention(qa, ka, va, split_q=True, ctx_len=ctx_len, need_ctx=need_ctx)
        yd = _attention(qd, kd, vd, split_q=False, ctx_len=ctx_len, need_ctx=need_ctx)
        yc = _window_attention(qc, kc, vc, c_sink[l], ctx_len=ctx_len, need_ctx=need_ctx)

        blocks = lru_wa.shape[2]
        bdiag = lambda wts: jnp.stack([jax.scipy.linalg.block_diag(*[wts[dd, h] for h in range(blocks)])
                                       for dd in range(2)]).astype(BF16)
        hf, hb = _lru(bx, conv_w[l], conv_b[l].reshape(1, lru_w), bdiag(lru_wa[l]), lru_ba[l].reshape(2, 1, lru_w),
                      bdiag(lru_wi[l]), lru_bi[l].reshape(2, 1, lru_w), lru_lambda[l].reshape(2, 1, lru_w),
                      ctx_len=ctx_len)

        rw_t = router_w[l].T.astype(BF16)
        xs_mid, v, gates, counts = _outproj(ya, hf, hb, bg, yc, yd, xs, modtab, w_out[l].astype(BF16),
                                            g_post_mix[l].reshape(1, d), g_pre_ffn[l].reshape(1, d),
                                            rw_t, router_bias[l].reshape(n_experts, 1), need_ctx=need_ctx)
        xs = _moe(v, gates, counts, xs_mid, modtab, w_gate, w_up, w_down, l,
                  sh_gate[l], sh_up[l], sh_down[l], g_post_ffn[l].reshape(1, d), has_ctx=need_ctx)
    return xs
```

```python
import functools

import jax
import jax.numpy as jnp
from jax import lax
from jax.experimental import pallas as pl
from jax.experimental.pallas import tpu as pltpu

F32 = jnp.float32
BF16 = jnp.bfloat16

GRID_W = 64
HEAD_DIM = 64
ROPE_THETA = 10000.0
NORM_EPS = 1e-6
WINDOW = 128
D_HEADS = 4
QK_NOPE = 64
QK_ROPE = 32
LRU_C = 8.0
TOP_K = 8
N_EXPERT_GROUPS = 8
TOPK_GROUPS = 4
ROUTED_SCALE = 2.5

LANES = 128
VMEM_LIMIT = 56 * 1024 * 1024
ROW_TILE = 256
INPROJ_ROWS = 768
WIN_QUERIES = 256

NEG_INF = float("-inf")
LOG2E = 1.4426950408889634


def _params(*sem):
    return pltpu.CompilerParams(dimension_semantics=sem, vmem_limit_bytes=VMEM_LIMIT)


def _rms(x, gain):
    return x * lax.rsqrt(jnp.mean(x * x, axis=-1, keepdims=True) + NORM_EPS) * gain


def _dot(a, b):
    return jnp.dot(a, b, preferred_element_type=F32)


def _dot_t(a, b):
    return lax.dot_general(a, b, (((1,), (1,)), ((), ())), preferred_element_type=F32)


def _full(a):
    return pl.BlockSpec(a.shape, lambda *_: (0,) * a.ndim)


def _mod_kernel(c_ref, w_ref, b_ref, o_ref):
    a = c_ref[...]
    a = a * jax.nn.sigmoid(a)
    o_ref[0] = _dot(a.astype(BF16), w_ref[0].astype(BF16)) + b_ref[0]


def _modulation(cond, w_mod, b_mod):
    depth, d, n = w_mod.shape
    tn = n // 4
    rows = cond.shape[0]
    return pl.pallas_call(
        _mod_kernel,
        grid=(depth, n // tn),
        in_specs=[pl.BlockSpec((rows, d), lambda l, j: (0, 0)),
                  pl.BlockSpec((1, d, tn), lambda l, j: (l, 0, j)),
                  pl.BlockSpec((1, 1, tn), lambda l, j: (l, 0, j))],
        out_specs=pl.BlockSpec((1, rows, tn), lambda l, j: (l, 0, j)),
        out_shape=jax.ShapeDtypeStruct((depth, rows, n), F32),
        compiler_params=_params("arbitrary", "arbitrary"),
    )(cond, w_mod, b_mod.reshape(depth, 1, n))


def _rope(t, cos, sin_lo, sin_hi, half):
    w = t.shape[-1]
    return t * cos + pltpu.roll(t, w - half, 1) * sin_lo + pltpu.roll(t, half, 1) * sin_hi


def _inproj_kernel(x_ref, mod_ref, gpre_ref, w_ref, seg_ref,
                   cos_ref, sinl_ref, sinh_ref, cosd_ref, sindl_ref, sindh_ref,
                   aqn_ref, akn_ref, dqn_ref, wuq_ref, dkvn_ref, wukv_ref,
                   qa_ref, ka_ref, va_ref, bx_ref, bg_ref, qc_ref, kc_ref, vc_ref,
                   qd_ref, kd_ref, vd_ref, *, q_lora, mla_scale, ctx_len):
    x = x_ref[0]
    rows = x.shape[0]
    mods = mod_ref[0]
    is_ctx = pl.program_id(1) * rows + lax.broadcasted_iota(jnp.int32, (rows, 1), 0) < ctx_len
    shift = jnp.where(is_ctx, mods[0, 0:1], mods[1, 0:1])
    scale = jnp.where(is_ctx, mods[0, 1:2], mods[1, 1:2])
    u = (_rms(x, gpre_ref[...]) * (1.0 + scale) + shift).astype(BF16)
    proj = _dot(u, w_ref[...])

    seg = seg_ref[...]
    lane = lax.broadcasted_iota(jnp.int32, (1, LANES), 1)
    low = lane < HEAD_DIM
    cos1, sin_lo1, sin_hi1 = cos_ref[...], sinl_ref[...], sinh_ref[...]
    cosd1, sind_lo1, sind_hi1 = cosd_ref[...], sindl_ref[...], sindh_ref[...]

    def head_rms(t, gain):
        w = t.shape[-1]
        sq = t * t
        hi = sq.astype(BF16)
        lo = (sq - hi.astype(F32)).astype(BF16)
        ms = (_dot(hi, seg[0:w, 0:w]) + _dot(lo, seg[0:w, 0:w])) * (1.0 / HEAD_DIM)
        return t * lax.rsqrt(ms + NORM_EPS) * gain

    def rope_hd(t):
        n = t.shape[-1] // LANES
        return _rope(t, *(jnp.concatenate([a] * n, axis=1) for a in (cos1, sin_lo1, sin_hi1)), HEAD_DIM // 2)

    def rope_r(t):
        n = t.shape[-1] // LANES
        return _rope(t, *(jnp.concatenate([a] * n, axis=1) for a in (cosd1, sind_lo1, sind_hi1)), QK_ROPE // 2)

    def per_query_head(t):
        r = pltpu.roll(t, HEAD_DIM, 1)
        return jnp.concatenate([jnp.where(low, t, r), jnp.where(low, r, t)], axis=1)

    sc_hd = HEAD_DIM ** -0.5 * LOG2E
    gw = qa_ref.shape[-1]
    kvw = gw // 2
    aq, ak, av = proj[:, 0:gw], proj[:, gw:gw + kvw], proj[:, gw + kvw:2 * gw]
    bx, bg = proj[:, 2 * gw:3 * gw], proj[:, 3 * gw:4 * gw]
    cq_, ck, cv = proj[:, 4 * gw:5 * gw], proj[:, 5 * gw:5 * gw + kvw], proj[:, 5 * gw + kvw:6 * gw]
    dq, dkv, dkr = proj[:, 6 * gw:7 * gw], proj[:, 7 * gw:7 * gw + LANES], proj[:, 7 * gw + LANES:8 * gw]
    qa_ref[0] = (rope_hd(head_rms(aq, aqn_ref[...])) * sc_hd).astype(BF16)
    ka_ref[0] = per_query_head(rope_hd(head_rms(ak, akn_ref[...]))).astype(BF16)
    va_ref[0] = per_query_head(av).astype(BF16)
    bx_ref[0] = bx
    bg_ref[0] = bg
    qc_ref[0] = (rope_hd(cq_) * sc_hd).astype(BF16)
    kc_ref[0] = per_query_head(rope_hd(ck)).astype(BF16)
    vc_ref[0] = per_query_head(cv).astype(BF16)
    cq = dq * lax.rsqrt(jnp.sum(dq * dq, axis=-1, keepdims=True) * (1.0 / q_lora) + NORM_EPS) * dqn_ref[...]
    qd = _dot(cq.astype(BF16), wuq_ref[...])
    qd_ref[0] = (rope_r(qd) * (mla_scale * LOG2E)).astype(BF16)
    ckv = _rms(dkv, dkvn_ref[...])
    kv = _dot(ckv.astype(BF16), wukv_ref[...])
    kr = rope_r(pltpu.roll(dkr, QK_NOPE, 1))
    kd_ref[0] = (kv[:, 0:512] + jnp.concatenate([kr] * D_HEADS, axis=1)).astype(BF16)
    vd_ref[0] = kv[:, 512:768].astype(BF16)


def _inproj(xs, modtab, gpre, w_in_p, seg, tabs, aqn, akn, dqn, wuq, dkvn, wukv, *, q_lora, mla_scale, ctx_len):
    bsz, L, d = xs.shape
    rows = INPROJ_ROWS
    nt = L // rows
    rowblk = lambda w: pl.BlockSpec((1, rows, w), lambda b, t: (b, t, 0))
    tab = lambda a: pl.BlockSpec((rows, a.shape[1]), lambda b, t: (t, 0))
    out_w = [(256, BF16)] * 3 + [(256, F32)] * 2 + [(256, BF16)] * 3 + [(512, BF16), (512, BF16), (256, BF16)]
    return pl.pallas_call(
        functools.partial(_inproj_kernel, q_lora=q_lora, mla_scale=mla_scale, ctx_len=ctx_len),
        grid=(bsz, nt),
        in_specs=[rowblk(d),
                  pl.BlockSpec((1, 2, 6, d), lambda b, t: (b, 0, 0, 0)),
                  _full(gpre), _full(w_in_p), _full(seg)] + [tab(a) for a in tabs]
                 + [_full(a) for a in (aqn, akn, dqn, wuq, dkvn, wukv)],
        out_specs=[rowblk(w) for w, _ in out_w],
        out_shape=[jax.ShapeDtypeStruct((bsz, L, w), dt) for w, dt in out_w],
        compiler_params=_params("arbitrary", "arbitrary"),
    )(xs, modtab, gpre, w_in_p, seg, *tabs, aqn, akn, dqn, wuq, dkvn, wukv)


def _attn_kernel(qa_ref, ka_ref, va_ref, qd_ref, kd_ref, vd_ref, oa_ref, od_ref, *, ctx_len, k_len, first_tile):
    lane = lax.broadcasted_iota(jnp.int32, (1, LANES), 1)
    low = lane < HEAD_DIM

    def mixer(q_ref, k_ref, v_ref, o_ref, split_q, n_keys):
        for g in range(v_ref.shape[-1] // LANES):
            outs = []
            for h in range(2):
                if split_q:
                    lanes = slice(g * LANES, (g + 1) * LANES)
                    keep = low if h == 0 else jnp.logical_not(low)
                    q = q_ref[0, :, lanes]
                    q = jnp.where(keep, q, jnp.zeros_like(q))
                else:
                    lanes = slice((2 * g + h) * LANES, (2 * g + h + 1) * LANES)
                    q = q_ref[0, :, lanes]
                s = _dot_t(q, k_ref[0, 0:n_keys, lanes])
                p = jnp.exp2(s - jnp.max(s, axis=-1, keepdims=True))
                den = jnp.sum(p, axis=-1, keepdims=True)
                outs.append(_dot(p.astype(BF16), v_ref[0, 0:n_keys, g * LANES:(g + 1) * LANES]) / den)
            o_ref[0, :, g * LANES:(g + 1) * LANES] = jnp.where(low, outs[0], outs[1]).astype(o_ref.dtype)

    def run(n_keys):
        mixer(qa_ref, ka_ref, va_ref, oa_ref, True, n_keys)
        mixer(qd_ref, kd_ref, vd_ref, od_ref, False, n_keys)

    if first_tile > 0:
        run(k_len)
        return
    t = pl.program_id(1)

    @pl.when(t == 0)
    def _ctx():
        run(ctx_len)

    @pl.when(t > 0)
    def _lat():
        run(k_len)


def _attention(qa, ka, va, qd, kd, vd, *, ctx_len, need_ctx):
    bsz, L, _ = qa.shape
    tq = ROW_TILE
    assert ctx_len == tq
    t0 = 0 if need_ctx else 1
    rows = lambda a: pl.BlockSpec((1, tq, a.shape[-1]), lambda b, t: (b, t + t0, 0))
    keys = lambda a: pl.BlockSpec((1, L, a.shape[-1]), lambda b, t: (b, 0, 0))
    return pl.pallas_call(
        functools.partial(_attn_kernel, ctx_len=ctx_len, k_len=L, first_tile=t0),
        grid=(bsz, L // tq - t0),
        in_specs=[rows(qa), keys(ka), keys(va), rows(qd), keys(kd), keys(vd)],
        out_specs=[rows(va), rows(vd)],
        out_shape=[jax.ShapeDtypeStruct(va.shape, BF16), jax.ShapeDtypeStruct(vd.shape, BF16)],
        compiler_params=_params("arbitrary", "arbitrary"),
    )(qa, ka, va, qd, kd, vd)


def _winattn_kernel(sink_ref, q_ref, k_ref, v_ref, o_ref, *, ctx_len, seq, first_block):
    t = pl.program_id(1) + first_block
    lane = lax.broadcasted_iota(jnp.int32, (1, LANES), 1)
    low = lane < HEAD_DIM
    ctx_blocks = ctx_len // WIN_QUERIES
    win = WIN_QUERIES + 2 * WINDOW
    n_groups = q_ref.shape[-1] // LANES

    def finish(parts, sink):
        m = sink
        for s, _ in parts:
            m = jnp.maximum(m, jnp.max(s, axis=-1, keepdims=True))
        den = jnp.exp2(sink - m)
        acc = None
        for s, vv in parts:
            p = jnp.exp2(s - m)
            den = den + jnp.sum(p, axis=-1, keepdims=True)
            o = _dot(p.astype(BF16), vv)
            acc = o if acc is None else acc + o
        return acc / den

    def heads(fn):
        tq = q_ref.shape[1]
        first_head = lax.broadcasted_iota(jnp.int32, (2 * tq, 1), 0) < tq
        for g in range(n_groups):
            lanes = slice(g * LANES, (g + 1) * LANES)
            q = q_ref[0, :, lanes]
            zero = jnp.zeros_like(q)
            q2 = jnp.concatenate([jnp.where(low, q, zero), jnp.where(low, zero, q)], axis=0)
            sink = jnp.where(first_head, sink_ref[2 * g], sink_ref[2 * g + 1]) * LOG2E
            o2 = fn(q2, sink, lanes)
            o_ref[0, :, lanes] = jnp.where(low, o2[0:tq], o2[tq:2 * tq]).astype(o_ref.dtype)

    def ctx_queries():
        def one(q, sink, lanes):
            return finish([(_dot_t(q, k_ref[0, 0:ctx_len, lanes]), v_ref[0, 0:ctx_len, lanes])], sink)
        heads(one)

    def latent_queries():
        n = t - ctx_blocks
        start = jnp.clip(n * WIN_QUERIES - WINDOW, 0, seq - win)
        off = pl.multiple_of(ctx_len + start, WINDOW)
        qrow = lax.broadcasted_iota(jnp.int32, (2 * WIN_QUERIES, win), 0)
        qpos = n * WIN_QUERIES + jnp.where(qrow < WIN_QUERIES, qrow, qrow - WIN_QUERIES)
        kpos = start + lax.broadcasted_iota(jnp.int32, (2 * WIN_QUERIES, win), 1)
        allowed = jnp.abs(kpos - qpos) <= WINDOW

        def one(q, sink, lanes):
            s_ctx = _dot_t(q, k_ref[0, 0:ctx_len, lanes])
            s_win = jnp.where(allowed, _dot_t(q, k_ref[0, pl.ds(off, win), lanes]), NEG_INF)
            return finish([(s_ctx, v_ref[0, 0:ctx_len, lanes]), (s_win, v_ref[0, pl.ds(off, win), lanes])], sink)
        heads(one)

    if first_block >= ctx_blocks:
        latent_queries()
    else:
        pl.when(t < ctx_blocks)(ctx_queries)
        pl.when(t >= ctx_blocks)(latent_queries)


def _window_attention(q, k, v, sink, *, ctx_len, need_ctx):
    bsz, L, w = q.shape
    b0 = 0 if need_ctx else ctx_len // WIN_QUERIES
    return pl.pallas_call(
        functools.partial(_winattn_kernel, ctx_len=ctx_len, seq=L - ctx_len, first_block=b0),
        grid=(bsz, L // WIN_QUERIES - b0),
        in_specs=[pl.BlockSpec(memory_space=pltpu.SMEM),
                  pl.BlockSpec((1, WIN_QUERIES, w), lambda b, t: (b, t + b0, 0)),
                  pl.BlockSpec((1, L, w), lambda b, t: (b, 0, 0)),
                  pl.BlockSpec((1, L, w), lambda b, t: (b, 0, 0))],
        out_specs=pl.BlockSpec((1, WIN_QUERIES, w), lambda b, t: (b, t + b0, 0)),
        out_shape=jax.ShapeDtypeStruct((bsz, L, w), BF16),
        compiler_params=_params("arbitrary", "arbitrary"),
    )(sink, q, k, v)


def _lru_kernel(xf_ref, xfp_ref, xfn_ref, xb_ref, xbp_ref, xbn_ref,
                cw_ref, cb_ref, wa_ref, ba_ref, wi_ref, bi_ref, lam_ref,
                hf_ref, hb_ref,
                af_s, bf_s, ab_s, bb_s, of_s, ob_s, sf_s, sb_s, *, n_chunks):
    i = pl.program_id(0)
    bsz, tc, w = xf_ref.shape
    rows = bsz * tc
    ti = lax.broadcasted_iota(jnp.int32, (bsz, tc, w), 1)

    def coeffs(x_ref, prev_ref, next_ref, chunk, d, a_s, b_s):
        has_prev = (chunk >= 2).astype(F32)
        has_next = jnp.logical_and(chunk >= 1, chunk <= n_chunks - 2).astype(F32)
        x = x_ref[...]
        p1 = prev_ref[:, 7:8, :] * has_prev
        n0 = next_ref[:, 0:1, :] * has_next
        n1 = next_ref[:, 1:2, :] * has_next
        x2 = x.reshape(rows, w)
        xm1 = jnp.where(ti == 0, p1, pltpu.roll(x2, 1, 0).reshape(bsz, tc, w))
        xp1 = jnp.where(ti == tc - 1, n0, pltpu.roll(x2, rows - 1, 0).reshape(bsz, tc, w))
        xp2 = jnp.where(ti == tc - 1, n1,
                        jnp.where(ti == tc - 2, n0, pltpu.roll(x2, rows - 2, 0).reshape(bsz, tc, w)))
        cw = cw_ref[...]
        xc = (cb_ref[...] + xm1 * cw[0:1] + x * cw[1:2] + xp1 * cw[2:3] + xp2 * cw[3:4]).reshape(rows, w)
        xcb = xc.astype(BF16)
        r = 0.5 * jnp.tanh(0.5 * (_dot(xcb, wa_ref[d]) + ba_ref[d])) + 0.5
        ig = 0.5 * jnp.tanh(0.5 * (_dot(xcb, wi_ref[d]) + bi_ref[d])) + 0.5
        log_a = (-LRU_C) * r * jax.nn.softplus(-lam_ref[d])
        a = jnp.exp(log_a)
        b = jnp.sqrt(1.0 - a * a) * (ig * xc)
        for j in range(w // LANES):
            a_s[j] = a[:, j * LANES:(j + 1) * LANES]
            b_s[j] = b[:, j * LANES:(j + 1) * LANES]

    chunk_b = jnp.where(i == 0, 0, n_chunks - i)
    coeffs(xf_ref, xfp_ref, xfn_ref, i, 0, af_s, bf_s)
    coeffs(xb_ref, xbp_ref, xbn_ref, chunk_b, 1, ab_s, bb_s)

    @pl.when(i == 0)
    def _init():
        sf_s[...] = jnp.zeros_like(sf_s)
        sb_s[...] = jnp.zeros_like(sb_s)

    nl = w // LANES

    def step(t, carry):
        fwd = pl.ds(t, bsz, stride=tc)
        bwd = pl.ds(tc - 1 - t, bsz, stride=tc)
        out = []
        for j in range(nl):
            hf = af_s[j, fwd, :] * carry[j] + bf_s[j, fwd, :]
            hb = ab_s[j, bwd, :] * carry[nl + j] + bb_s[j, bwd, :]
            of_s[j, fwd, :] = hf
            ob_s[j, bwd, :] = hb
            out.append((hf, hb))
        return tuple(o[0] for o in out) + tuple(o[1] for o in out)

    init = tuple(sf_s[j] for j in range(nl)) + tuple(sb_s[j] for j in range(nl))
    fin = lax.fori_loop(0, tc, step, init, unroll=8)
    for j in range(nl):
        sf_s[j] = fin[j]
        sb_s[j] = fin[nl + j]
    hf_ref[...] = jnp.concatenate([of_s[j] for j in range(nl)], axis=1).reshape(bsz, tc, w)
    hb_ref[...] = jnp.concatenate([ob_s[j] for j in range(nl)], axis=1).reshape(bsz, tc, w)


def _lru(bx, conv_w, conv_b, wa, ba, wi, bi, lam, *, ctx_len):
    bsz, L, w = bx.shape
    tc = ctx_len
    nc = L // tc
    hb_blocks = tc // 8
    last8 = L // 8 - 1

    def fchunk(i):
        return i

    def bchunk(i):
        return jnp.where(i == 0, 0, nc - i)

    def cur(cf):
        return pl.BlockSpec((bsz, tc, w), lambda i: (0, cf(i), 0))

    def prev(cf):
        return pl.BlockSpec((bsz, 8, w), lambda i: (0, jnp.maximum(cf(i) * hb_blocks - 1, 0), 0))

    def nxt(cf):
        return pl.BlockSpec((bsz, 8, w), lambda i: (0, jnp.minimum((cf(i) + 1) * hb_blocks, last8), 0))

    small = [conv_w, conv_b, wa, ba, wi, bi, lam]
    nl = w // LANES
    scr = [pltpu.VMEM((nl, bsz * tc, LANES), F32)] * 6 + [pltpu.VMEM((nl, bsz, LANES), F32)] * 2
    return pl.pallas_call(
        functools.partial(_lru_kernel, n_chunks=nc),
        grid=(nc,),
        in_specs=[cur(fchunk), prev(fchunk), nxt(fchunk), cur(bchunk), prev(bchunk), nxt(bchunk)]
                 + [_full(a) for a in small],
        out_specs=[cur(fchunk), cur(bchunk)],
        out_shape=[jax.ShapeDtypeStruct((bsz, L, w), F32)] * 2,
        scratch_shapes=scr,
        compiler_params=_params("arbitrary"),
    )(bx, bx, bx, bx, bx, bx, *small)


def _route(logits_t, bias, n_experts):
    per = n_experts // N_EXPERT_GROUPS
    tm = logits_t.shape[-1]
    scores = jax.nn.sigmoid(logits_t).reshape(N_EXPERT_GROUPS, per, tm)
    sel = scores + bias.reshape(N_EXPERT_GROUPS, per, 1)
    shape = sel.shape
    gi = lax.broadcasted_iota(jnp.int32, shape, 0)
    mi = lax.broadcasted_iota(jnp.int32, shape, 1)
    ei = gi * per + mi
    m1 = jnp.max(sel, axis=1, keepdims=True)
    first = jnp.min(jnp.where(sel == m1, mi, per), axis=1, keepdims=True)
    m2 = jnp.max(jnp.where(mi == first, NEG_INF, sel), axis=1, keepdims=True)
    gscore = m1 + m2
    gidx = lax.broadcasted_iota(jnp.int32, gscore.shape, 0)
    gmask = jnp.zeros(gscore.shape, F32)
    for _ in range(TOPK_GROUPS):
        m = jnp.max(gscore, axis=0, keepdims=True)
        pick = jnp.min(jnp.where(gscore == m, gidx, N_EXPERT_GROUPS), axis=0, keepdims=True)
        hit = gidx == pick
        gmask = jnp.where(hit, 1.0, gmask)
        gscore = jnp.where(hit, NEG_INF, gscore)
    cand = jnp.where(gmask > 0.0, sel, NEG_INF)
    chosen = jnp.zeros(shape, F32)
    for _ in range(TOP_K):
        m = jnp.max(jnp.max(cand, axis=1, keepdims=True), axis=0, keepdims=True)
        pick = jnp.where(cand == m, ei, n_experts)
        pick = jnp.min(jnp.min(pick, axis=1, keepdims=True), axis=0, keepdims=True)
        hit = ei == pick
        chosen = jnp.where(hit, 1.0, chosen)
        cand = jnp.where(hit, NEG_INF, cand)
    wsel = jnp.where(chosen > 0.0, scores, 0.0)
    den = jnp.sum(jnp.sum(wsel, axis=1, keepdims=True), axis=0, keepdims=True)
    return (wsel / den * ROUTED_SCALE).reshape(n_experts, tm)


def _outproj_kernel(ya_ref, hf_ref, hb_ref, bg_ref, yc_ref, yd_ref, x_ref, mod_ref,
                    wout_ref, gpost_ref, gffn_ref, rw_ref, rb_ref,
                    xo_ref, v_ref, gate_ref, cnt_ref, *, n_experts):
    gw = ya_ref.shape[-1]
    m = mod_ref[0, 0]
    yb = ((hf_ref[0] + hb_ref[0]) * jax.nn.gelu(bg_ref[0])).astype(BF16)
    y = (_dot(ya_ref[0], wout_ref[0:gw, :]) + _dot(yb, wout_ref[gw:2 * gw, :])
         + _dot(yc_ref[0], wout_ref[2 * gw:3 * gw, :]) + _dot(yd_ref[0], wout_ref[3 * gw:4 * gw, :]))
    x1 = x_ref[0] + m[2:3] * _rms(y, gpost_ref[...])
    xo_ref[0] = x1
    v = (_rms(x1, gffn_ref[...]) * (1.0 + m[4:5]) + m[3:4]).astype(BF16)
    v_ref[0] = v
    logits_t = _dot_t(rw_ref[...], v)
    gates_t = _route(logits_t, rb_ref[...], n_experts)
    pad = jnp.zeros((LANES - n_experts, gates_t.shape[1]), F32)
    gates = jnp.concatenate([gates_t, pad], axis=0).T
    gate_ref[0] = gates
    cnt_ref[0, 0] = jnp.sum(jnp.where(gates > 0.0, 1.0, 0.0), axis=0, keepdims=True).astype(jnp.int32)


def _outproj(ya, hf, hb, bg, yc, yd, xs, modtab, w_out, gpost, gffn, rw_t, rbias, *, need_ctx):
    bsz, L, d = xs.shape
    n_experts = rw_t.shape[0]
    gw = ya.shape[-1]
    t0 = 0 if need_ctx else 1
    nt = L // ROW_TILE - t0
    rows_out = nt * ROW_TILE
    blk = lambda w: pl.BlockSpec((1, ROW_TILE, w), lambda b, t: (b, t + t0, 0))
    oblk = lambda w: pl.BlockSpec((1, ROW_TILE, w), lambda b, t: (b, t, 0))
    return pl.pallas_call(
        functools.partial(_outproj_kernel, n_experts=n_experts),
        grid=(bsz, nt),
        in_specs=[blk(gw)] * 6 + [blk(d),
                  pl.BlockSpec((1, 1, 6, d), lambda b, t: (b, jnp.minimum(t + t0, 1), 0, 0)),
                  _full(w_out), _full(gpost), _full(gffn), _full(rw_t), _full(rbias)],
        out_specs=[oblk(d), oblk(d), oblk(LANES),
                   pl.BlockSpec((1, 1, 1, LANES), lambda b, t: (b, t, 0, 0))],
        out_shape=[jax.ShapeDtypeStruct((bsz, rows_out, d), F32), jax.ShapeDtypeStruct((bsz, rows_out, d), BF16),
                   jax.ShapeDtypeStruct((bsz, rows_out, LANES), F32),
                   jax.ShapeDtypeStruct((bsz, nt, 1, LANES), jnp.int32)],
        compiler_params=_params("arbitrary", "arbitrary"),
    )(ya, hf, hb, bg, yc, yd, xs, modtab, w_out, gpost, gffn, rw_t, rbias)


SLOT_ALIGN = 16
EXPERT_BLOCK = 1024
SLOT_CHUNK = 1536
SHORT_BITS = 2
CODE_BASE = 64.0


def _swiglu(v, wg, wu, wd):
    hg = _dot(v, wg.astype(BF16))
    hu = _dot(v, wu.astype(BF16))
    return hg * jax.nn.sigmoid(hg) * hu, wd.astype(BF16)


def _slot_cap(n_experts):
    rows = ROW_TILE * TOP_K + n_experts * (SLOT_ALIGN - 1)
    return -(-rows // SLOT_CHUNK) * SLOT_CHUNK


def _slot_codes(gates, off_row, ltri):
    chosen = gates > 0.0
    rank = _dot(ltri, jnp.where(chosen, 1.0, 0.0).astype(BF16))
    code = jnp.where(chosen, off_row.astype(F32) + rank + 1.0, 0.0)
    hi = jnp.floor(code * (1.0 / CODE_BASE))
    return hi.astype(BF16), (code - CODE_BASE * hi).astype(BF16)


def _slot_expert_onehot(off_row, len_row, first, rows):
    r = first + lax.broadcasted_iota(jnp.int32, (rows, LANES), 0)
    inside = jnp.where(r >= off_row, jnp.where(r < off_row + len_row, 1.0, 0.0), 0.0)
    return inside.astype(BF16)


def _row_digits(first, shape, axis):
    code = (first + 1 + lax.broadcasted_iota(jnp.int32, shape, axis)).astype(F32)
    hi = jnp.floor(code * (1.0 / CODE_BASE))
    return hi, code - CODE_BASE * hi


def _slot_dma(src, dst, src_off, dst_off, n_units, sem, start):
    def piece(first, size):
        cp = pltpu.make_async_copy(
            src.at[pl.ds(pl.multiple_of(src_off + first, SLOT_ALIGN), size)],
            dst.at[pl.ds(pl.multiple_of(dst_off + first, SLOT_ALIGN), size)], sem)
        if start:
            cp.start()
        else:
            cp.wait()

    for bit in range(SHORT_BITS):
        pl.when(((n_units >> bit) & 1) == 1)(
            functools.partial(piece, (n_units & ((1 << bit) - 1)) * SLOT_ALIGN, SLOT_ALIGN << bit))

    long_rows = SLOT_ALIGN << SHORT_BITS
    short_rows = (n_units & ((1 << SHORT_BITS) - 1)) * SLOT_ALIGN

    def long_piece(j, carry):
        piece(short_rows + j * long_rows, long_rows)
        return carry
    lax.fori_loop(0, n_units >> SHORT_BITS, long_piece, 0)


def _wait_rows(src, dst, n_units, sem):
    for bit in range((min(src.shape[0], dst.shape[0]) // SLOT_ALIGN).bit_length()):
        size = SLOT_ALIGN << bit

        @pl.when(((n_units >> bit) & 1) == 1)
        def _piece():
            pltpu.make_async_copy(src.at[pl.ds(0, size)], dst.at[pl.ds(0, size)], sem).wait()


def _dispatch_kernel(pos_ref, off_ref, len_ref, tot_ref, tpos_ref, tlen_ref,
                     v_ref, g_ref, offv_ref, lenv_ref, ltri_ref, xs_ref,
                     buf_s, zero_s, sem, *, n_experts):
    i = pl.program_id(0)
    nt = pl.num_programs(0)
    cur = i % 2
    cap = buf_s.shape[1]

    @pl.when(i == 0)
    def _zero():
        zero_s[...] = jnp.zeros_like(zero_s)

    g = g_ref[...]
    off_row, len_row = offv_ref[0], lenv_ref[0]
    hi, lo = _slot_codes(g, off_row, ltri_ref[...])
    g_hi = g.astype(BF16)
    g_lo = (g - g_hi.astype(F32)).astype(BF16)
    half = LANES // 2
    src = jnp.concatenate([v_ref[...], g_hi[:, :half], g_lo[:, :half]], axis=1)
    for c in range(cap // SLOT_CHUNK):
        first = c * SLOT_CHUNK
        oh = _slot_expert_onehot(off_row, len_row, first, SLOT_CHUNK)
        rh, rl = _row_digits(first, (SLOT_CHUNK, 1), 0)
        pick = jnp.where(_dot_t(oh, hi) == rh, jnp.where(_dot_t(oh, lo) == rl, 1.0, 0.0), 0.0)
        buf_s[cur, first:first + SLOT_CHUNK, :] = _dot(pick.astype(BF16), src).astype(BF16)

    def slots(tile, which, start):
        base = tile * n_experts
        entry = lambda k: (off_ref[k], pos_ref[k], len_ref[k])

        def body(e, cur):
            nxt = entry(base + jnp.minimum(e + 1, n_experts - 1))
            _slot_dma(buf_s.at[which], xs_ref, *cur, sem.at[0], start)
            return nxt
        lax.fori_loop(0, n_experts, body, entry(base))

    def tails(start):
        def body(e, carry):
            _slot_dma(zero_s, xs_ref, 0, tpos_ref[e], tlen_ref[e], sem.at[0], start)
            return carry
        lax.fori_loop(0, n_experts, body, 0)

    @pl.when(i > 0)
    def _drain_previous():
        _wait_rows(buf_s.at[1 - cur], xs_ref, tot_ref[i - 1], sem.at[0])

    slots(i, cur, True)

    @pl.when(i == nt - 1)
    def _last():
        tails(True)
        _wait_rows(buf_s.at[cur], xs_ref, tot_ref[i], sem.at[0])
        tails(False)


def _dispatch(v2, gates2, plan, ltri, *, n_experts, n_rows):
    t, d = v2.shape
    nt = t // ROW_TILE
    cap = _slot_cap(n_experts)
    assert n_experts <= LANES // 2
    width = d + LANES
    grid_spec = pltpu.PrefetchScalarGridSpec(
        num_scalar_prefetch=6, grid=(nt,),
        in_specs=[pl.BlockSpec((ROW_TILE, d), lambda i, *_: (i, 0)),
                  pl.BlockSpec((ROW_TILE, LANES), lambda i, *_: (i, 0)),
                  pl.BlockSpec((1, 1, LANES), lambda i, *_: (i, 0, 0)),
                  pl.BlockSpec((1, 1, LANES), lambda i, *_: (i, 0, 0)),
                  pl.BlockSpec(ltri.shape, lambda i, *_: (0, 0))],
        out_specs=pl.BlockSpec(memory_space=pl.ANY),
        scratch_shapes=[pltpu.VMEM((2, cap, width), BF16),
                        pltpu.VMEM((EXPERT_BLOCK, width), BF16),
                        pltpu.SemaphoreType.DMA((1,))])
    return pl.pallas_call(
        functools.partial(_dispatch_kernel, n_experts=n_experts),
        grid_spec=grid_spec,
        out_shape=jax.ShapeDtypeStruct((n_rows, width), BF16),
        compiler_params=_params("arbitrary"),
    )(plan["pos"], plan["off"], plan["len"], plan["tile_len"], plan["tail_pos"], plan["tail_len"],
      v2, gates2, plan["off_v"], plan["len_v"], ltri)


def _expert_kernel(be_ref, na_ref, x_ref, wg_ref, wu_ref, wd_ref, y_ref, wg_s, wu_s, wd_s):
    b = pl.program_id(0)
    e = be_ref[b]

    @pl.when(jnp.logical_or(b == 0, e != be_ref[jnp.maximum(b - 1, 0)]))
    def _new_expert():
        wg_s[...] = wg_ref[0, 0].astype(BF16)
        wu_s[...] = wu_ref[0, 0].astype(BF16)
        wd_s[...] = wd_ref[0, 0].astype(BF16)

    @pl.when(b < na_ref[0])
    def _active():
        d = wg_s.shape[0]
        x = x_ref[:, 0:d]
        gates = x_ref[:, d:d + LANES].astype(F32)
        lane = lax.broadcasted_iota(jnp.int32, gates.shape, 1) & (LANES // 2 - 1)
        gcol = jnp.sum(jnp.where(lane == e, gates, 0.0), axis=1, keepdims=True)
        hg = _dot(x, wg_s[...])
        h = hg * jax.nn.sigmoid(hg) * _dot(x, wu_s[...])
        y_ref[...] = (_dot(h.astype(BF16), wd_s[...]) * gcol).astype(BF16)


def _experts(xs_sorted, plan, wg, wu, wd, layer):
    n_rows, width = xs_sorted.shape
    _, _, d, de = wg.shape
    nb = n_rows // EXPERT_BLOCK
    rowmap = lambda b, be, na: (jnp.minimum(b, na[0] - 1), 0)
    grid_spec = pltpu.PrefetchScalarGridSpec(
        num_scalar_prefetch=2, grid=(nb,),
        in_specs=[pl.BlockSpec((EXPERT_BLOCK, width), rowmap),
                  pl.BlockSpec((1, 1, d, de), lambda b, be, na: (layer, be[b], 0, 0)),
                  pl.BlockSpec((1, 1, d, de), lambda b, be, na: (layer, be[b], 0, 0)),
                  pl.BlockSpec((1, 1, de, d), lambda b, be, na: (layer, be[b], 0, 0))],
        out_specs=pl.BlockSpec((EXPERT_BLOCK, d), rowmap),
        scratch_shapes=[pltpu.VMEM((d, de), BF16), pltpu.VMEM((d, de), BF16), pltpu.VMEM((de, d), BF16)])
    return pl.pallas_call(
        _expert_kernel, grid_spec=grid_spec,
        out_shape=jax.ShapeDtypeStruct((n_rows, d), BF16),
        compiler_params=_params("arbitrary"),
    )(plan["block_expert"], plan["n_active"], xs_sorted, wg, wu, wd)


def _combine_kernel(pos_ref, off_ref, len_ref, tot_ref,
                    ys_ref, v_ref, g_ref, offv_ref, lenv_ref, ltri_ref, x_ref, mod_ref,
                    sg_ref, su_ref, sd_ref, gpost_ref, o_ref, buf_s, sem, *, n_experts):
    i = pl.program_id(0)
    nt = pl.num_programs(0)
    cur = i % 2
    cap = buf_s.shape[1]

    def slots(tile, which, start):
        base = tile * n_experts
        entry = lambda k: (pos_ref[k], off_ref[k], len_ref[k])

        def body(e, cur):
            nxt = entry(base + jnp.minimum(e + 1, n_experts - 1))
            _slot_dma(ys_ref, buf_s.at[which], *cur, sem.at[which], start)
            return nxt
        lax.fori_loop(0, n_experts, body, entry(base))

    @pl.when(i == 0)
    def _first():
        buf_s[...] = jnp.zeros_like(buf_s)
        slots(0, 0, True)

    @pl.when(i + 1 < nt)
    def _prefetch():
        slots(i + 1, 1 - cur, True)

    v = v_ref[...]
    h, wd = _swiglu(v, sg_ref[...], su_ref[...], sd_ref[...])
    acc = _dot(h.astype(BF16), wd)
    off_row, len_row = offv_ref[0], lenv_ref[0]
    hi, lo = _slot_codes(g_ref[...], off_row, ltri_ref[...])

    _wait_rows(ys_ref, buf_s.at[cur], tot_ref[i], sem.at[cur])
    for c in range(cap // SLOT_CHUNK):
        first = c * SLOT_CHUNK
        oh = _slot_expert_onehot(off_row, len_row, first, SLOT_CHUNK)
        rh, rl = _row_digits(first, (1, SLOT_CHUNK), 1)
        pick = jnp.where(_dot_t(hi, oh) == rh, jnp.where(_dot_t(lo, oh) == rl, 1.0, 0.0), 0.0)
        acc = acc + _dot(pick.astype(BF16), buf_s[cur, first:first + SLOT_CHUNK, :])
    o_ref[0] = x_ref[0] + mod_ref[0, 0][5:6] * _rms(acc, gpost_ref[...])


def _combine(ys_sorted, v2, gates2, plan, ltri, xs, modtab, sg, su, sd, gpost, *, n_experts, has_ctx):
    bsz, L, d = xs.shape
    tpb = L // ROW_TILE
    nt = bsz * tpb
    cap = _slot_cap(n_experts)
    full = lambda a: pl.BlockSpec(a.shape, lambda i, *_: (0,) * a.ndim)
    grid_spec = pltpu.PrefetchScalarGridSpec(
        num_scalar_prefetch=4, grid=(nt,),
        in_specs=[pl.BlockSpec(memory_space=pl.ANY),
                  pl.BlockSpec((ROW_TILE, d), lambda i, *_: (i, 0)),
                  pl.BlockSpec((ROW_TILE, LANES), lambda i, *_: (i, 0)),
                  pl.BlockSpec((1, 1, LANES), lambda i, *_: (i, 0, 0)),
                  pl.BlockSpec((1, 1, LANES), lambda i, *_: (i, 0, 0)),
                  full(ltri),
                  pl.BlockSpec((1, ROW_TILE, d), lambda i, *_: (i // tpb, i % tpb, 0)),
                  pl.BlockSpec((1, 1, 6, d),
                               lambda i, *_: (i // tpb, jnp.minimum(i % tpb, 1) if has_ctx else 1, 0, 0)),
                  full(sg), full(su), full(sd), full(gpost)],
        out_specs=pl.BlockSpec((1, ROW_TILE, d), lambda i, *_: (i // tpb, i % tpb, 0)),
        scratch_shapes=[pltpu.VMEM((2, cap, d), BF16), pltpu.SemaphoreType.DMA((2,))])
    return pl.pallas_call(
        functools.partial(_combine_kernel, n_experts=n_experts),
        grid_spec=grid_spec,
        out_shape=jax.ShapeDtypeStruct((bsz, L, d), F32),
        compiler_params=_params("arbitrary"),
    )(plan["pos"], plan["off"], plan["len"], plan["tile_len"], ys_sorted, v2, gates2, plan["off_v"], plan["len_v"], ltri,
      xs, modtab, sg, su, sd, gpost)


def _moe_plan(counts, n_experts, n_blocks):
    a = (counts + (SLOT_ALIGN - 1)) // SLOT_ALIGN * SLOT_ALIGN
    rows = jnp.sum(a, axis=0)
    region = (rows + (EXPERT_BLOCK - 1)) // EXPERT_BLOCK * EXPERT_BLOCK
    region_end = jnp.cumsum(region)
    region_start = region_end - region
    pos = region_start[None, :] + jnp.cumsum(a, axis=0) - a
    off = jnp.cumsum(a, axis=1) - a
    first_row = jnp.arange(n_blocks, dtype=jnp.int32) * EXPERT_BLOCK
    block_expert = jnp.minimum(jnp.sum(region_end[None, :n_experts] <= first_row[:, None], axis=1), n_experts - 1)
    flat = lambda t: t[:, :n_experts].reshape(-1).astype(jnp.int32)
    nt = counts.shape[0]
    return {
        "pos": flat(pos), "off": flat(off), "len": flat(a // SLOT_ALIGN),
        "tile_len": (jnp.sum(a, axis=1) // SLOT_ALIGN).astype(jnp.int32),
        "tail_pos": (region_start + rows)[:n_experts].astype(jnp.int32),
        "tail_len": ((region - rows) // SLOT_ALIGN)[:n_experts].astype(jnp.int32),
        "off_v": off.reshape(nt, 1, LANES).astype(jnp.int32),
        "len_v": a.reshape(nt, 1, LANES).astype(jnp.int32),
        "block_expert": block_expert.astype(jnp.int32),
        "n_active": (region_end[-1:] // EXPERT_BLOCK).astype(jnp.int32),
    }


def _moe(v, gates, counts, xs, modtab, wg, wu, wd, layer, sg, su, sd, gpost, *, has_ctx):
    bsz, L, d = xs.shape
    n_experts = wg.shape[1]
    t = bsz * L
    nt = t // ROW_TILE
    worst = t * TOP_K + nt * n_experts * (SLOT_ALIGN - 1) + n_experts * (EXPERT_BLOCK - 1)
    n_blocks = -(-worst // EXPERT_BLOCK)
    plan = _moe_plan(counts.reshape(nt, LANES), n_experts, n_blocks)
    ltri = jnp.tril(jnp.ones((ROW_TILE, ROW_TILE), F32), -1).astype(BF16)
    v2, gates2 = v.reshape(t, d), gates.reshape(t, LANES)
    xs_sorted = _dispatch(v2, gates2, plan, ltri, n_experts=n_experts, n_rows=n_blocks * EXPERT_BLOCK)
    ys_sorted = _experts(xs_sorted, plan, wg, wu, wd, layer)
    return _combine(ys_sorted, v2, gates2, plan, ltri, xs, modtab, sg, su, sd, gpost,
                    n_experts=n_experts, has_ctx=has_ctx)


def _tables(seq, ctx_len):
    def build(rot_dim, lead, slot):
        n = rot_dim // 4
        pos = jnp.arange(seq)
        row = (pos // GRID_W).astype(F32)
        col = (pos % GRID_W).astype(F32)
        inv = ROPE_THETA ** (-jnp.arange(n, dtype=F32) / n)
        ang = jnp.concatenate([row[:, None] * inv, col[:, None] * inv], axis=-1)
        cos, sin = jnp.cos(ang), jnp.sin(ang)
        zero = jnp.zeros_like(sin)
        tail = slot - lead - rot_dim
        one_l, zero_l = jnp.ones((seq, lead), F32), jnp.zeros((seq, lead), F32)
        one_t, zero_t = jnp.ones((seq, tail), F32), jnp.zeros((seq, tail), F32)
        c = jnp.concatenate([one_l, cos, cos, one_t], axis=1)
        s_lo = jnp.concatenate([zero_l, -sin, zero, zero_t], axis=1)
        s_hi = jnp.concatenate([zero_l, zero, sin, zero_t], axis=1)
        ctx_c = jnp.ones((ctx_len, slot), F32)
        ctx_s = jnp.zeros((ctx_len, slot), F32)
        return [jnp.concatenate([ctx_c, c], axis=0), jnp.concatenate([ctx_s, s_lo], axis=0),
                jnp.concatenate([ctx_s, s_hi], axis=0)]

    hd = [jnp.concatenate([t, t], axis=1) for t in build(HEAD_DIM, 0, HEAD_DIM)]
    mla = build(QK_ROPE, QK_NOPE, LANES)
    return hd + mla


def kernel(x, c, ctx, c_ctx, w_mod, b_mod, g_pre_mix, g_post_mix, g_pre_ffn, g_post_ffn, w_in, a_q_norm, a_k_norm, conv_w, conv_b, lru_wa, lru_ba, lru_wi, lru_bi, lru_lambda, c_sink, d_q_norm, d_w_uq, d_kv_norm, d_w_ukv, w_out, router_w, router_bias, w_gate, w_up, w_down, sh_gate, sh_up, sh_down):
    bsz, seq, d = x.shape
    ctx_len = ctx.shape[1]
    depth = w_mod.shape[0]
    gw = d // 4
    lru_w = conv_w.shape[-1]
    q_lora = d_q_norm.shape[-1]
    kv_lora = d_kv_norm.shape[-1]
    n_experts = router_w.shape[-1]
    a_heads = gw // HEAD_DIM
    a_kv = (w_in.shape[-1] - (2 * gw + 2 * lru_w + q_lora + kv_lora + QK_ROPE)) // (4 * HEAD_DIM)
    mla_scale = (QK_NOPE + QK_ROPE) ** -0.5
    v_dim = gw // D_HEADS
    assert ctx_len == ROW_TILE and a_kv == 2 and a_heads == 4 and n_experts <= LANES

    cond = jnp.zeros((16, d), F32).at[:bsz].set(c).at[bsz].set(c_ctx)
    mod = _modulation(cond, w_mod, b_mod).reshape(depth, 16, 6, d)
    tabs = _tables(seq, ctx_len)
    seg = jnp.kron(jnp.eye(a_heads, dtype=F32), jnp.ones((HEAD_DIM, HEAD_DIM), F32)).astype(BF16)

    xs = jnp.concatenate([ctx, x], axis=1)
    for l in range(depth):
        modtab = jnp.stack([jnp.broadcast_to(mod[l, bsz], (bsz, 6, d)), mod[l, :bsz]], axis=1)

        dq_end = 2 * gw + 2 * lru_w + 4 * a_kv * HEAD_DIM + q_lora
        zeros = lambda n: jnp.zeros((d, n), F32)
        w_in_p = jnp.concatenate([w_in[l][:, :dq_end], zeros(2 * LANES - q_lora),
                                  w_in[l][:, dq_end:], zeros(LANES - QK_ROPE)], axis=1).astype(BF16)

        qk = QK_NOPE + QK_ROPE
        wuq = d_w_uq[l].reshape(q_lora, D_HEADS, qk)
        wuq = jnp.pad(wuq, ((0, 2 * LANES - q_lora), (0, 0), (0, LANES - qk))).reshape(2 * LANES, D_HEADS * LANES)
        wukv = d_w_ukv[l].reshape(kv_lora, D_HEADS, QK_NOPE + v_dim)
        wk = jnp.pad(wukv[:, :, :QK_NOPE], ((0, 0), (0, 0), (0, LANES - QK_NOPE))).reshape(kv_lora, D_HEADS * LANES)
        wv = wukv[:, :, QK_NOPE:].reshape(kv_lora, D_HEADS * v_dim)
        wukv_p = jnp.concatenate([wk, wv], axis=1).astype(BF16)
        dqn = jnp.pad(d_q_norm[l], (0, 2 * LANES - q_lora)).reshape(1, 2 * LANES)

        qa, ka, va, bx, bg, qc, kc, vc, qd, kd, vd = _inproj(
            xs, modtab, g_pre_mix[l].reshape(1, d), w_in_p, seg, tabs,
            jnp.tile(a_q_norm[l], a_heads).reshape(1, gw), jnp.tile(a_k_norm[l], a_kv).reshape(1, gw // 2),
            dqn, wuq.astype(BF16), d_kv_norm[l].reshape(1, kv_lora), wukv_p,
            q_lora=q_lora, mla_scale=mla_scale, ctx_len=ctx_len)

        need_ctx = l < depth - 1
        ya, yd = _attention(qa, ka, va, qd, kd, vd, ctx_len=ctx_len, need_ctx=need_ctx)
        yc = _window_attention(qc, kc, vc, c_sink[l], ctx_len=ctx_len, need_ctx=need_ctx)

        blocks = lru_wa.shape[2]
        bdiag = lambda wts: jnp.stack([jax.scipy.linalg.block_diag(*[wts[dd, h] for h in range(blocks)])
                                       for dd in range(2)]).astype(BF16)
        hf, hb = _lru(bx, conv_w[l], conv_b[l].reshape(1, lru_w), bdiag(lru_wa[l]), lru_ba[l].reshape(2, 1, lru_w),
                      bdiag(lru_wi[l]), lru_bi[l].reshape(2, 1, lru_w), lru_lambda[l].reshape(2, 1, lru_w),
                      ctx_len=ctx_len)

        rw_t = router_w[l].T.astype(BF16)
        xs_mid, v, gates, counts = _outproj(ya, hf, hb, bg, yc, yd, xs, modtab, w_out[l].astype(BF16),
                                            g_post_mix[l].reshape(1, d), g_pre_ffn[l].reshape(1, d),
                                            rw_t, router_bias[l].reshape(n_experts, 1), need_ctx=need_ctx)
        xs = _moe(v, gates, counts, xs_mid, modtab, w_gate, w_up, w_down, l,
                  sh_gate[l], sh_up[l], sh_down[l], g_post_ffn[l].reshape(1, d), has_ctx=need_ctx)
    return xs
```

```python
import functools

import jax
import jax.numpy as jnp
from jax import lax
from jax.experimental import pallas as pl
from jax.experimental.pallas import tpu as pltpu

F32 = jnp.float32
BF16 = jnp.bfloat16

GRID_W = 64
HEAD_DIM = 64
ROPE_THETA = 10000.0
NORM_EPS = 1e-6
WINDOW = 128
D_HEADS = 4
QK_NOPE = 64
QK_ROPE = 32
LRU_C = 8.0
TOP_K = 8
N_EXPERT_GROUPS = 8
TOPK_GROUPS = 4
ROUTED_SCALE = 2.5

LANES = 128
VMEM_LIMIT = 56 * 1024 * 1024
ROW_TILE = 256
INPROJ_ROWS = 768

NEG_INF = float("-inf")
LOG2E = 1.4426950408889634


def _params(*sem):
    return pltpu.CompilerParams(dimension_semantics=sem, vmem_limit_bytes=VMEM_LIMIT)


def _rms(x, gain):
    return x * lax.rsqrt(jnp.mean(x * x, axis=-1, keepdims=True) + NORM_EPS) * gain


def _dot(a, b):
    return jnp.dot(a, b, preferred_element_type=F32)


def _dot_t(a, b):
    return lax.dot_general(a, b, (((1,), (1,)), ((), ())), preferred_element_type=F32)


def _full(a):
    return pl.BlockSpec(a.shape, lambda *_: (0,) * a.ndim)


def _mod_kernel(c_ref, w_ref, b_ref, o_ref):
    a = c_ref[...]
    a = a * jax.nn.sigmoid(a)
    o_ref[0] = _dot(a.astype(BF16), w_ref[0].astype(BF16)) + b_ref[0]


def _modulation(cond, w_mod, b_mod):
    depth, d, n = w_mod.shape
    tn = n // 4
    rows = cond.shape[0]
    return pl.pallas_call(
        _mod_kernel,
        grid=(depth, n // tn),
        in_specs=[pl.BlockSpec((rows, d), lambda l, j: (0, 0)),
                  pl.BlockSpec((1, d, tn), lambda l, j: (l, 0, j)),
                  pl.BlockSpec((1, 1, tn), lambda l, j: (l, 0, j))],
        out_specs=pl.BlockSpec((1, rows, tn), lambda l, j: (l, 0, j)),
        out_shape=jax.ShapeDtypeStruct((depth, rows, n), F32),
        compiler_params=_params("arbitrary", "arbitrary"),
    )(cond, w_mod, b_mod.reshape(depth, 1, n))


def _rope(t, cos, sin_lo, sin_hi, half):
    w = t.shape[-1]
    return t * cos + pltpu.roll(t, w - half, 1) * sin_lo + pltpu.roll(t, half, 1) * sin_hi


def _inproj_kernel(x_ref, mod_ref, gpre_ref, w_ref, seg_ref,
                   cos_ref, sinl_ref, sinh_ref, cosd_ref, sindl_ref, sindh_ref,
                   aqn_ref, akn_ref, dqn_ref, wuq_ref, dkvn_ref, wukv_ref,
                   qa_ref, ka_ref, va_ref, bx_ref, bg_ref, qc_ref, kc_ref, vc_ref,
                   qd_ref, kd_ref, vd_ref, *, q_lora, mla_scale, ctx_len):
    x = x_ref[0]
    rows = x.shape[0]
    mods = mod_ref[0]
    is_ctx = pl.program_id(1) * rows + lax.broadcasted_iota(jnp.int32, (rows, 1), 0) < ctx_len
    shift = jnp.where(is_ctx, mods[0, 0:1], mods[1, 0:1])
    scale = jnp.where(is_ctx, mods[0, 1:2], mods[1, 1:2])
    u = (_rms(x, gpre_ref[...]) * (1.0 + scale) + shift).astype(BF16)
    proj = _dot(u, w_ref[...])

    seg = seg_ref[...]
    lane = lax.broadcasted_iota(jnp.int32, (1, LANES), 1)
    low = lane < HEAD_DIM
    cos1, sin_lo1, sin_hi1 = cos_ref[...], sinl_ref[...], sinh_ref[...]
    cosd1, sind_lo1, sind_hi1 = cosd_ref[...], sindl_ref[...], sindh_ref[...]

    def head_rms(t, gain):
        w = t.shape[-1]
        sq = t * t
        hi = sq.astype(BF16)
        lo = (sq - hi.astype(F32)).astype(BF16)
        ms = (_dot(hi, seg[0:w, 0:w]) + _dot(lo, seg[0:w, 0:w])) * (1.0 / HEAD_DIM)
        return t * lax.rsqrt(ms + NORM_EPS) * gain

    def rope_hd(t):
        n = t.shape[-1] // LANES
        return _rope(t, *(jnp.concatenate([a] * n, axis=1) for a in (cos1, sin_lo1, sin_hi1)), HEAD_DIM // 2)

    def rope_r(t):
        n = t.shape[-1] // LANES
        return _rope(t, *(jnp.concatenate([a] * n, axis=1) for a in (cosd1, sind_lo1, sind_hi1)), QK_ROPE // 2)

    def per_query_head(t):
        r = pltpu.roll(t, HEAD_DIM, 1)
        return jnp.concatenate([jnp.where(low, t, r), jnp.where(low, r, t)], axis=1)

    sc_hd = HEAD_DIM ** -0.5 * LOG2E
    gw = qa_ref.shape[-1]
    kvw = gw // 2
    aq, ak, av = proj[:, 0:gw], proj[:, gw:gw + kvw], proj[:, gw + kvw:2 * gw]
    bx, bg = proj[:, 2 * gw:3 * gw], proj[:, 3 * gw:4 * gw]
    cq_, ck, cv = proj[:, 4 * gw:5 * gw], proj[:, 5 * gw:5 * gw + kvw], proj[:, 5 * gw + kvw:6 * gw]
    dq, dkv, dkr = proj[:, 6 * gw:7 * gw], proj[:, 7 * gw:7 * gw + LANES], proj[:, 7 * gw + LANES:8 * gw]
    qa_ref[0] = (rope_hd(head_rms(aq, aqn_ref[...])) * sc_hd).astype(BF16)
    ka_ref[0] = per_query_head(rope_hd(head_rms(ak, akn_ref[...]))).astype(BF16)
    va_ref[0] = per_query_head(av).astype(BF16)
    bx_ref[0] = bx
    bg_ref[0] = bg
    qc_ref[0] = (rope_hd(cq_) * sc_hd).astype(BF16)
    kc_ref[0] = per_query_head(rope_hd(ck)).astype(BF16)
    vc_ref[0] = per_query_head(cv).astype(BF16)
    cq = dq * lax.rsqrt(jnp.sum(dq * dq, axis=-1, keepdims=True) * (1.0 / q_lora) + NORM_EPS) * dqn_ref[...]
    qd = _dot(cq.astype(BF16), wuq_ref[...])
    qd_ref[0] = (rope_r(qd) * (mla_scale * LOG2E)).astype(BF16)
    ckv = _rms(dkv, dkvn_ref[...])
    kv = _dot(ckv.astype(BF16), wukv_ref[...])
    kr = rope_r(pltpu.roll(dkr, QK_NOPE, 1))
    kd_ref[0] = (kv[:, 0:512] + jnp.concatenate([kr] * D_HEADS, axis=1)).astype(BF16)
    vd_ref[0] = kv[:, 512:768].astype(BF16)


def _inproj(xs, modtab, gpre, w_in_p, seg, tabs, aqn, akn, dqn, wuq, dkvn, wukv, *, q_lora, mla_scale, ctx_len):
    bsz, L, d = xs.shape
    rows = INPROJ_ROWS
    nt = L // rows
    rowblk = lambda w: pl.BlockSpec((1, rows, w), lambda b, t: (b, t, 0))
    tab = lambda a: pl.BlockSpec((rows, a.shape[1]), lambda b, t: (t, 0))
    out_w = [(256, BF16)] * 3 + [(256, F32)] * 2 + [(256, BF16)] * 3 + [(512, BF16), (512, BF16), (256, BF16)]
    return pl.pallas_call(
        functools.partial(_inproj_kernel, q_lora=q_lora, mla_scale=mla_scale, ctx_len=ctx_len),
        grid=(bsz, nt),
        in_specs=[rowblk(d),
                  pl.BlockSpec((1, 2, 6, d), lambda b, t: (b, 0, 0, 0)),
                  _full(gpre), _full(w_in_p), _full(seg)] + [tab(a) for a in tabs]
                 + [_full(a) for a in (aqn, akn, dqn, wuq, dkvn, wukv)],
        out_specs=[rowblk(w) for w, _ in out_w],
        out_shape=[jax.ShapeDtypeStruct((bsz, L, w), dt) for w, dt in out_w],
        compiler_params=_params("arbitrary", "arbitrary"),
    )(xs, modtab, gpre, w_in_p, seg, *tabs, aqn, akn, dqn, wuq, dkvn, wukv)


def _dense_heads(q_ref, k_ref, v_ref, o_ref, split_q, n_keys):
    low = lax.broadcasted_iota(jnp.int32, (1, LANES), 1) < HEAD_DIM
    for g in range(v_ref.shape[-1] // LANES):
        outs = []
        for h in range(2):
            if split_q:
                lanes = slice(g * LANES, (g + 1) * LANES)
                keep = low if h == 0 else jnp.logical_not(low)
                q = q_ref[0, :, lanes]
                q = jnp.where(keep, q, jnp.zeros_like(q))
            else:
                lanes = slice((2 * g + h) * LANES, (2 * g + h + 1) * LANES)
                q = q_ref[0, :, lanes]
            s = _dot_t(q, k_ref[0, 0:n_keys, lanes])
            p = jnp.exp2(s - jnp.max(s, axis=-1, keepdims=True))
            den = jnp.sum(p, axis=-1, keepdims=True)
            outs.append(_dot(p.astype(BF16), v_ref[0, 0:n_keys, g * LANES:(g + 1) * LANES]) / den)
        o_ref[0, :, g * LANES:(g + 1) * LANES] = jnp.where(low, outs[0], outs[1]).astype(o_ref.dtype)


def _band_heads(sink_ref, q_ref, k_ref, v_ref, o_ref, block, ctx_len, seq):
    low = lax.broadcasted_iota(jnp.int32, (1, LANES), 1) < HEAD_DIM
    tq = q_ref.shape[1]
    win = tq + 2 * WINDOW

    def finish(parts, sink):
        m = sink
        for s, _ in parts:
            m = jnp.maximum(m, jnp.max(s, axis=-1, keepdims=True))
        den = jnp.exp2(sink - m)
        acc = None
        for s, vv in parts:
            p = jnp.exp2(s - m)
            den = den + jnp.sum(p, axis=-1, keepdims=True)
            o = _dot(p.astype(BF16), vv)
            acc = o if acc is None else acc + o
        return acc / den

    if block is not None:
        start = jnp.clip(block * tq - WINDOW, 0, seq - win)
        off = pl.multiple_of(ctx_len + start, WINDOW)
        qrow = lax.broadcasted_iota(jnp.int32, (2 * tq, win), 0)
        qpos = block * tq + jnp.where(qrow < tq, qrow, qrow - tq)
        kpos = start + lax.broadcasted_iota(jnp.int32, (2 * tq, win), 1)
        allowed = jnp.abs(kpos - qpos) <= WINDOW

    first_head = lax.broadcasted_iota(jnp.int32, (2 * tq, 1), 0) < tq
    for g in range(q_ref.shape[-1] // LANES):
        lanes = slice(g * LANES, (g + 1) * LANES)
        q = q_ref[0, :, lanes]
        zero = jnp.zeros_like(q)
        q2 = jnp.concatenate([jnp.where(low, q, zero), jnp.where(low, zero, q)], axis=0)
        sink = jnp.where(first_head, sink_ref[2 * g], sink_ref[2 * g + 1]) * LOG2E
        parts = [(_dot_t(q2, k_ref[0, 0:ctx_len, lanes]), v_ref[0, 0:ctx_len, lanes])]
        if block is not None:
            s_win = jnp.where(allowed, _dot_t(q2, k_ref[0, pl.ds(off, win), lanes]), NEG_INF)
            parts.append((s_win, v_ref[0, pl.ds(off, win), lanes]))
        o2 = finish(parts, sink)
        o_ref[0, :, lanes] = jnp.where(low, o2[0:tq], o2[tq:2 * tq]).astype(o_ref.dtype)


def _mixers_kernel(sink_ref, qa_ref, ka_ref, va_ref, qd_ref, kd_ref, vd_ref, qc_ref, kc_ref, vc_ref,
                   oa_ref, od_ref, oc_ref, *, ctx_len, k_len, first_tile):
    t = pl.program_id(1) + first_tile

    def context_queries():
        _dense_heads(qa_ref, ka_ref, va_ref, oa_ref, True, ctx_len)
        _dense_heads(qd_ref, kd_ref, vd_ref, od_ref, False, ctx_len)
        _band_heads(sink_ref, qc_ref, kc_ref, vc_ref, oc_ref, None, ctx_len, k_len - ctx_len)

    def latent_queries():
        _dense_heads(qa_ref, ka_ref, va_ref, oa_ref, True, k_len)
        _dense_heads(qd_ref, kd_ref, vd_ref, od_ref, False, k_len)
        _band_heads(sink_ref, qc_ref, kc_ref, vc_ref, oc_ref, t - 1, ctx_len, k_len - ctx_len)

    if first_tile > 0:
        latent_queries()
    else:
        pl.when(t == 0)(context_queries)
        pl.when(t > 0)(latent_queries)


def _attention(qa, ka, va, qd, kd, vd, qc, kc, vc, sink, *, ctx_len, need_ctx):
    bsz, L, _ = qa.shape
    tq = ROW_TILE
    assert ctx_len == tq
    t0 = 0 if need_ctx else 1
    rows = lambda a: pl.BlockSpec((1, tq, a.shape[-1]), lambda b, t: (b, t + t0, 0))
    keys = lambda a: pl.BlockSpec((1, L, a.shape[-1]), lambda b, t: (b, 0, 0))
    return pl.pallas_call(
        functools.partial(_mixers_kernel, ctx_len=ctx_len, k_len=L, first_tile=t0),
        grid=(bsz, L // tq - t0),
        in_specs=[pl.BlockSpec(memory_space=pltpu.SMEM),
                  rows(qa), keys(ka), keys(va), rows(qd), keys(kd), keys(vd), rows(qc), keys(kc), keys(vc)],
        out_specs=[rows(va), rows(vd), rows(vc)],
        out_shape=[jax.ShapeDtypeStruct(v.shape, BF16) for v in (va, vd, vc)],
        compiler_params=_params("arbitrary", "arbitrary"),
    )(sink, qa, ka, va, qd, kd, vd, qc, kc, vc)


def _lru_kernel(xf_ref, xfp_ref, xfn_ref, xb_ref, xbp_ref, xbn_ref,
                cw_ref, cb_ref, wa_ref, ba_ref, wi_ref, bi_ref, lam_ref,
                hf_ref, hb_ref,
                af_s, bf_s, ab_s, bb_s, of_s, ob_s, sf_s, sb_s, *, n_chunks):
    i = pl.program_id(0)
    bsz, tc, w = xf_ref.shape
    rows = bsz * tc
    ti = lax.broadcasted_iota(jnp.int32, (bsz, tc, w), 1)

    def coeffs(x_ref, prev_ref, next_ref, chunk, d, a_s, b_s):
        has_prev = (chunk >= 2).astype(F32)
        has_next = jnp.logical_and(chunk >= 1, chunk <= n_chunks - 2).astype(F32)
        x = x_ref[...]
        p1 = prev_ref[:, 7:8, :] * has_prev
        n0 = next_ref[:, 0:1, :] * has_next
        n1 = next_ref[:, 1:2, :] * has_next
        x2 = x.reshape(rows, w)
        xm1 = jnp.where(ti == 0, p1, pltpu.roll(x2, 1, 0).reshape(bsz, tc, w))
        xp1 = jnp.where(ti == tc - 1, n0, pltpu.roll(x2, rows - 1, 0).reshape(bsz, tc, w))
        xp2 = jnp.where(ti == tc - 1, n1,
                        jnp.where(ti == tc - 2, n0, pltpu.roll(x2, rows - 2, 0).reshape(bsz, tc, w)))
        cw = cw_ref[...]
        xc = (cb_ref[...] + xm1 * cw[0:1] + x * cw[1:2] + xp1 * cw[2:3] + xp2 * cw[3:4]).reshape(rows, w)
        xcb = xc.astype(BF16)
        r = 0.5 * jnp.tanh(0.5 * (_dot(xcb, wa_ref[d]) + ba_ref[d])) + 0.5
        ig = 0.5 * jnp.tanh(0.5 * (_dot(xcb, wi_ref[d]) + bi_ref[d])) + 0.5
        log_a = (-LRU_C) * r * jax.nn.softplus(-lam_ref[d])
        a = jnp.exp(log_a)
        b = jnp.sqrt(1.0 - a * a) * (ig * xc)
        for j in range(w // LANES):
            a_s[j] = a[:, j * LANES:(j + 1) * LANES]
            b_s[j] = b[:, j * LANES:(j + 1) * LANES]

    chunk_b = jnp.where(i == 0, 0, n_chunks - i)
    coeffs(xf_ref, xfp_ref, xfn_ref, i, 0, af_s, bf_s)
    coeffs(xb_ref, xbp_ref, xbn_ref, chunk_b, 1, ab_s, bb_s)

    @pl.when(i == 0)
    def _init():
        sf_s[...] = jnp.zeros_like(sf_s)
        sb_s[...] = jnp.zeros_like(sb_s)

    nl = w // LANES

    def step(t, carry):
        fwd = pl.ds(t, bsz, stride=tc)
        bwd = pl.ds(tc - 1 - t, bsz, stride=tc)
        out = []
        for j in range(nl):
            hf = af_s[j, fwd, :] * carry[j] + bf_s[j, fwd, :]
            hb = ab_s[j, bwd, :] * carry[nl + j] + bb_s[j, bwd, :]
            of_s[j, fwd, :] = hf
            ob_s[j, bwd, :] = hb
            out.append((hf, hb))
        return tuple(o[0] for o in out) + tuple(o[1] for o in out)

    init = tuple(sf_s[j] for j in range(nl)) + tuple(sb_s[j] for j in range(nl))
    fin = lax.fori_loop(0, tc, step, init, unroll=8)
    for j in range(nl):
        sf_s[j] = fin[j]
        sb_s[j] = fin[nl + j]
    hf_ref[...] = jnp.concatenate([of_s[j] for j in range(nl)], axis=1).reshape(bsz, tc, w)
    hb_ref[...] = jnp.concatenate([ob_s[j] for j in range(nl)], axis=1).reshape(bsz, tc, w)


def _lru(bx, conv_w, conv_b, wa, ba, wi, bi, lam, *, ctx_len):
    bsz, L, w = bx.shape
    tc = ctx_len
    nc = L // tc
    hb_blocks = tc // 8
    last8 = L // 8 - 1

    def fchunk(i):
        return i

    def bchunk(i):
        return jnp.where(i == 0, 0, nc - i)

    def cur(cf):
        return pl.BlockSpec((bsz, tc, w), lambda i: (0, cf(i), 0))

    def prev(cf):
        return pl.BlockSpec((bsz, 8, w), lambda i: (0, jnp.maximum(cf(i) * hb_blocks - 1, 0), 0))

    def nxt(cf):
        return pl.BlockSpec((bsz, 8, w), lambda i: (0, jnp.minimum((cf(i) + 1) * hb_blocks, last8), 0))

    small = [conv_w, conv_b, wa, ba, wi, bi, lam]
    nl = w // LANES
    scr = [pltpu.VMEM((nl, bsz * tc, LANES), F32)] * 6 + [pltpu.VMEM((nl, bsz, LANES), F32)] * 2
    return pl.pallas_call(
        functools.partial(_lru_kernel, n_chunks=nc),
        grid=(nc,),
        in_specs=[cur(fchunk), prev(fchunk), nxt(fchunk), cur(bchunk), prev(bchunk), nxt(bchunk)]
                 + [_full(a) for a in small],
        out_specs=[cur(fchunk), cur(bchunk)],
        out_shape=[jax.ShapeDtypeStruct((bsz, L, w), F32)] * 2,
        scratch_shapes=scr,
        compiler_params=_params("arbitrary"),
    )(bx, bx, bx, bx, bx, bx, *small)


def _route(logits_t, bias, n_experts):
    per = n_experts // N_EXPERT_GROUPS
    tm = logits_t.shape[-1]
    scores = jax.nn.sigmoid(logits_t).reshape(N_EXPERT_GROUPS, per, tm)
    sel = scores + bias.reshape(N_EXPERT_GROUPS, per, 1)
    shape = sel.shape
    gi = lax.broadcasted_iota(jnp.int32, shape, 0)
    mi = lax.broadcasted_iota(jnp.int32, shape, 1)
    ei = gi * per + mi
    m1 = jnp.max(sel, axis=1, keepdims=True)
    first = jnp.min(jnp.where(sel == m1, mi, per), axis=1, keepdims=True)
    m2 = jnp.max(jnp.where(mi == first, NEG_INF, sel), axis=1, keepdims=True)
    gscore = m1 + m2
    gidx = lax.broadcasted_iota(jnp.int32, gscore.shape, 0)
    gmask = jnp.zeros(gscore.shape, F32)
    for _ in range(TOPK_GROUPS):
        m = jnp.max(gscore, axis=0, keepdims=True)
        pick = jnp.min(jnp.where(gscore == m, gidx, N_EXPERT_GROUPS), axis=0, keepdims=True)
        hit = gidx == pick
        gmask = jnp.where(hit, 1.0, gmask)
        gscore = jnp.where(hit, NEG_INF, gscore)
    cand = jnp.where(gmask > 0.0, sel, NEG_INF)
    chosen = jnp.zeros(shape, F32)
    for _ in range(TOP_K):
        m = jnp.max(jnp.max(cand, axis=1, keepdims=True), axis=0, keepdims=True)
        pick = jnp.where(cand == m, ei, n_experts)
        pick = jnp.min(jnp.min(pick, axis=1, keepdims=True), axis=0, keepdims=True)
        hit = ei == pick
        chosen = jnp.where(hit, 1.0, chosen)
        cand = jnp.where(hit, NEG_INF, cand)
    wsel = jnp.where(chosen > 0.0, scores, 0.0)
    den = jnp.sum(jnp.sum(wsel, axis=1, keepdims=True), axis=0, keepdims=True)
    return (wsel / den * ROUTED_SCALE).reshape(n_experts, tm)


def _outproj_kernel(ya_ref, hf_ref, hb_ref, bg_ref, yc_ref, yd_ref, x_ref, mod_ref,
                    wout_ref, gpost_ref, gffn_ref, rw_ref, rb_ref,
                    xo_ref, v_ref, gate_ref, cnt_ref, *, n_experts):
    gw = ya_ref.shape[-1]
    m = mod_ref[0, 0]
    yb = ((hf_ref[0] + hb_ref[0]) * jax.nn.gelu(bg_ref[0])).astype(BF16)
    y = (_dot(ya_ref[0], wout_ref[0:gw, :]) + _dot(yb, wout_ref[gw:2 * gw, :])
         + _dot(yc_ref[0], wout_ref[2 * gw:3 * gw, :]) + _dot(yd_ref[0], wout_ref[3 * gw:4 * gw, :]))
    x1 = x_ref[0] + m[2:3] * _rms(y, gpost_ref[...])
    xo_ref[0] = x1
    v = (_rms(x1, gffn_ref[...]) * (1.0 + m[4:5]) + m[3:4]).astype(BF16)
    v_ref[0] = v
    logits_t = _dot_t(rw_ref[...], v)
    gates_t = _route(logits_t, rb_ref[...], n_experts)
    pad = jnp.zeros((LANES - n_experts, gates_t.shape[1]), F32)
    gates = jnp.concatenate([gates_t, pad], axis=0).T
    gate_ref[0] = gates
    cnt_ref[0, 0] = jnp.sum(jnp.where(gates > 0.0, 1.0, 0.0), axis=0, keepdims=True).astype(jnp.int32)


def _outproj(ya, hf, hb, bg, yc, yd, xs, modtab, w_out, gpost, gffn, rw_t, rbias, *, need_ctx):
    bsz, L, d = xs.shape
    n_experts = rw_t.shape[0]
    gw = ya.shape[-1]
    t0 = 0 if need_ctx else 1
    nt = L // ROW_TILE - t0
    rows_out = nt * ROW_TILE
    blk = lambda w: pl.BlockSpec((1, ROW_TILE, w), lambda b, t: (b, t + t0, 0))
    oblk = lambda w: pl.BlockSpec((1, ROW_TILE, w), lambda b, t: (b, t, 0))
    return pl.pallas_call(
        functools.partial(_outproj_kernel, n_experts=n_experts),
        grid=(bsz, nt),
        in_specs=[blk(gw)] * 6 + [blk(d),
                  pl.BlockSpec((1, 1, 6, d), lambda b, t: (b, jnp.minimum(t + t0, 1), 0, 0)),
                  _full(w_out), _full(gpost), _full(gffn), _full(rw_t), _full(rbias)],
        out_specs=[oblk(d), oblk(d), oblk(LANES),
                   pl.BlockSpec((1, 1, 1, LANES), lambda b, t: (b, t, 0, 0))],
        out_shape=[jax.ShapeDtypeStruct((bsz, rows_out, d), F32), jax.ShapeDtypeStruct((bsz, rows_out, d), BF16),
                   jax.ShapeDtypeStruct((bsz, rows_out, LANES), F32),
                   jax.ShapeDtypeStruct((bsz, nt, 1, LANES), jnp.int32)],
        compiler_params=_params("arbitrary", "arbitrary"),
    )(ya, hf, hb, bg, yc, yd, xs, modtab, w_out, gpost, gffn, rw_t, rbias)


SLOT_ALIGN = 16
EXPERT_BLOCK = 1024
SLOT_CHUNK = 1536
SHORT_BITS = 2
CODE_BASE = 64.0


def _swiglu(v, wg, wu, wd):
    hg = _dot(v, wg.astype(BF16))
    hu = _dot(v, wu.astype(BF16))
    return hg * jax.nn.sigmoid(hg) * hu, wd.astype(BF16)


def _slot_cap(n_experts):
    rows = ROW_TILE * TOP_K + n_experts * (SLOT_ALIGN - 1)
    return -(-rows // SLOT_CHUNK) * SLOT_CHUNK


def _slot_codes(gates, off_row, ltri):
    chosen = gates > 0.0
    rank = _dot(ltri, jnp.where(chosen, 1.0, 0.0).astype(BF16))
    code = jnp.where(chosen, off_row.astype(F32) + rank + 1.0, 0.0)
    hi = jnp.floor(code * (1.0 / CODE_BASE))
    return hi.astype(BF16), (code - CODE_BASE * hi).astype(BF16)


def _slot_expert_onehot(off_row, len_row, first, rows):
    r = first + lax.broadcasted_iota(jnp.int32, (rows, LANES), 0)
    inside = jnp.where(r >= off_row, jnp.where(r < off_row + len_row, 1.0, 0.0), 0.0)
    return inside.astype(BF16)


def _row_digits(first, shape, axis):
    code = (first + 1 + lax.broadcasted_iota(jnp.int32, shape, axis)).astype(F32)
    hi = jnp.floor(code * (1.0 / CODE_BASE))
    return hi, code - CODE_BASE * hi


def _slot_dma(src, dst, src_off, dst_off, n_units, sem, start):
    def piece(first, size):
        cp = pltpu.make_async_copy(
            src.at[pl.ds(pl.multiple_of(src_off + first, SLOT_ALIGN), size)],
            dst.at[pl.ds(pl.multiple_of(dst_off + first, SLOT_ALIGN), size)], sem)
        if start:
            cp.start()
        else:
            cp.wait()

    for bit in range(SHORT_BITS):
        pl.when(((n_units >> bit) & 1) == 1)(
            functools.partial(piece, (n_units & ((1 << bit) - 1)) * SLOT_ALIGN, SLOT_ALIGN << bit))

    long_rows = SLOT_ALIGN << SHORT_BITS
    short_rows = (n_units & ((1 << SHORT_BITS) - 1)) * SLOT_ALIGN

    def long_piece(j, carry):
        piece(short_rows + j * long_rows, long_rows)
        return carry
    lax.fori_loop(0, n_units >> SHORT_BITS, long_piece, 0)


def _wait_rows(src, dst, n_units, sem):
    for bit in range((min(src.shape[0], dst.shape[0]) // SLOT_ALIGN).bit_length()):
        size = SLOT_ALIGN << bit

        @pl.when(((n_units >> bit) & 1) == 1)
        def _piece():
            pltpu.make_async_copy(src.at[pl.ds(0, size)], dst.at[pl.ds(0, size)], sem).wait()


def _dispatch_kernel(pos_ref, off_ref, len_ref, tot_ref, tpos_ref, tlen_ref,
                     v_ref, g_ref, offv_ref, lenv_ref, ltri_ref, xs_ref,
                     buf_s, zero_s, sem, *, n_experts):
    i = pl.program_id(0)
    nt = pl.num_programs(0)
    cur = i % 2
    cap = buf_s.shape[1]

    @pl.when(i == 0)
    def _zero():
        zero_s[...] = jnp.zeros_like(zero_s)

    g = g_ref[...]
    off_row, len_row = offv_ref[0], lenv_ref[0]
    hi, lo = _slot_codes(g, off_row, ltri_ref[...])
    g_hi = g.astype(BF16)
    g_lo = (g - g_hi.astype(F32)).astype(BF16)
    half = LANES // 2
    src = jnp.concatenate([v_ref[...], g_hi[:, :half], g_lo[:, :half]], axis=1)
    for c in range(cap // SLOT_CHUNK):
        first = c * SLOT_CHUNK
        oh = _slot_expert_onehot(off_row, len_row, first, SLOT_CHUNK)
        rh, rl = _row_digits(first, (SLOT_CHUNK, 1), 0)
        pick = jnp.where(_dot_t(oh, hi) == rh, jnp.where(_dot_t(oh, lo) == rl, 1.0, 0.0), 0.0)
        buf_s[cur, first:first + SLOT_CHUNK, :] = _dot(pick.astype(BF16), src).astype(BF16)

    def slots(tile, which, start):
        base = tile * n_experts
        entry = lambda k: (off_ref[k], pos_ref[k], len_ref[k])

        def body(e, cur):
            nxt = entry(base + jnp.minimum(e + 1, n_experts - 1))
            _slot_dma(buf_s.at[which], xs_ref, *cur, sem.at[0], start)
            return nxt
        lax.fori_loop(0, n_experts, body, entry(base))

    def tails(start):
        def body(e, carry):
            _slot_dma(zero_s, xs_ref, 0, tpos_ref[e], tlen_ref[e], sem.at[0], start)
            return carry
        lax.fori_loop(0, n_experts, body, 0)

    @pl.when(i > 0)
    def _drain_previous():
        _wait_rows(buf_s.at[1 - cur], xs_ref, tot_ref[i - 1], sem.at[0])

    slots(i, cur, True)

    @pl.when(i == nt - 1)
    def _last():
        tails(True)
        _wait_rows(buf_s.at[cur], xs_ref, tot_ref[i], sem.at[0])
        tails(False)


def _dispatch(v2, gates2, plan, ltri, *, n_experts, n_rows):
    t, d = v2.shape
    nt = t // ROW_TILE
    cap = _slot_cap(n_experts)
    assert n_experts <= LANES // 2
    width = d + LANES
    grid_spec = pltpu.PrefetchScalarGridSpec(
        num_scalar_prefetch=6, grid=(nt,),
        in_specs=[pl.BlockSpec((ROW_TILE, d), lambda i, *_: (i, 0)),
                  pl.BlockSpec((ROW_TILE, LANES), lambda i, *_: (i, 0)),
                  pl.BlockSpec((1, 1, LANES), lambda i, *_: (i, 0, 0)),
                  pl.BlockSpec((1, 1, LANES), lambda i, *_: (i, 0, 0)),
                  pl.BlockSpec(ltri.shape, lambda i, *_: (0, 0))],
        out_specs=pl.BlockSpec(memory_space=pl.ANY),
        scratch_shapes=[pltpu.VMEM((2, cap, width), BF16),
                        pltpu.VMEM((EXPERT_BLOCK, width), BF16),
                        pltpu.SemaphoreType.DMA((1,))])
    return pl.pallas_call(
        functools.partial(_dispatch_kernel, n_experts=n_experts),
        grid_spec=grid_spec,
        out_shape=jax.ShapeDtypeStruct((n_rows, width), BF16),
        compiler_params=_params("arbitrary"),
    )(plan["pos"], plan["off"], plan["len"], plan["tile_len"], plan["tail_pos"], plan["tail_len"],
      v2, gates2, plan["off_v"], plan["len_v"], ltri)


def _expert_kernel(be_ref, na_ref, x_ref, wg_ref, wu_ref, wd_ref, y_ref, wg_s, wu_s, wd_s):
    b = pl.program_id(0)
    e = be_ref[b]

    @pl.when(jnp.logical_or(b == 0, e != be_ref[jnp.maximum(b - 1, 0)]))
    def _new_expert():
        wg_s[...] = wg_ref[0, 0].astype(BF16)
        wu_s[...] = wu_ref[0, 0].astype(BF16)
        wd_s[...] = wd_ref[0, 0].astype(BF16)

    @pl.when(b < na_ref[0])
    def _active():
        d = wg_s.shape[0]
        x = x_ref[:, 0:d]
        gates = x_ref[:, d:d + LANES].astype(F32)
        lane = lax.broadcasted_iota(jnp.int32, gates.shape, 1) & (LANES // 2 - 1)
        gcol = jnp.sum(jnp.where(lane == e, gates, 0.0), axis=1, keepdims=True)
        hg = _dot(x, wg_s[...])
        h = hg * jax.nn.sigmoid(hg) * _dot(x, wu_s[...])
        y_ref[...] = (_dot(h.astype(BF16), wd_s[...]) * gcol).astype(BF16)


def _experts(xs_sorted, plan, wg, wu, wd, layer):
    n_rows, width = xs_sorted.shape
    _, _, d, de = wg.shape
    nb = n_rows // EXPERT_BLOCK
    rowmap = lambda b, be, na: (jnp.minimum(b, na[0] - 1), 0)
    grid_spec = pltpu.PrefetchScalarGridSpec(
        num_scalar_prefetch=2, grid=(nb,),
        in_specs=[pl.BlockSpec((EXPERT_BLOCK, width), rowmap),
                  pl.BlockSpec((1, 1, d, de), lambda b, be, na: (layer, be[b], 0, 0)),
                  pl.BlockSpec((1, 1, d, de), lambda b, be, na: (layer, be[b], 0, 0)),
                  pl.BlockSpec((1, 1, de, d), lambda b, be, na: (layer, be[b], 0, 0))],
        out_specs=pl.BlockSpec((EXPERT_BLOCK, d), rowmap),
        scratch_shapes=[pltpu.VMEM((d, de), BF16), pltpu.VMEM((d, de), BF16), pltpu.VMEM((de, d), BF16)])
    return pl.pallas_call(
        _expert_kernel, grid_spec=grid_spec,
        out_shape=jax.ShapeDtypeStruct((n_rows, d), BF16),
        compiler_params=_params("arbitrary"),
    )(plan["block_expert"], plan["n_active"], xs_sorted, wg, wu, wd)


def _combine_kernel(pos_ref, off_ref, len_ref, tot_ref,
                    ys_ref, v_ref, g_ref, offv_ref, lenv_ref, ltri_ref, x_ref, mod_ref,
                    sg_ref, su_ref, sd_ref, gpost_ref, o_ref, buf_s, sem, *, n_experts):
    i = pl.program_id(0)
    nt = pl.num_programs(0)
    cur = i % 2
    cap = buf_s.shape[1]

    def slots(tile, which, start):
        base = tile * n_experts
        entry = lambda k: (pos_ref[k], off_ref[k], len_ref[k])

        def body(e, cur):
            nxt = entry(base + jnp.minimum(e + 1, n_experts - 1))
            _slot_dma(ys_ref, buf_s.at[which], *cur, sem.at[which], start)
            return nxt
        lax.fori_loop(0, n_experts, body, entry(base))

    @pl.when(i == 0)
    def _first():
        buf_s[...] = jnp.zeros_like(buf_s)
        slots(0, 0, True)

    @pl.when(i + 1 < nt)
    def _prefetch():
        slots(i + 1, 1 - cur, True)

    v = v_ref[...]
    h, wd = _swiglu(v, sg_ref[...], su_ref[...], sd_ref[...])
    acc = _dot(h.astype(BF16), wd)
    off_row, len_row = offv_ref[0], lenv_ref[0]
    hi, lo = _slot_codes(g_ref[...], off_row, ltri_ref[...])

    _wait_rows(ys_ref, buf_s.at[cur], tot_ref[i], sem.at[cur])
    for c in range(cap // SLOT_CHUNK):
        first = c * SLOT_CHUNK
        oh = _slot_expert_onehot(off_row, len_row, first, SLOT_CHUNK)
        rh, rl = _row_digits(first, (1, SLOT_CHUNK), 1)
        pick = jnp.where(_dot_t(hi, oh) == rh, jnp.where(_dot_t(lo, oh) == rl, 1.0, 0.0), 0.0)
        acc = acc + _dot(pick.astype(BF16), buf_s[cur, first:first + SLOT_CHUNK, :])
    o_ref[0] = x_ref[0] + mod_ref[0, 0][5:6] * _rms(acc, gpost_ref[...])


def _combine(ys_sorted, v2, gates2, plan, ltri, xs, modtab, sg, su, sd, gpost, *, n_experts, has_ctx):
    bsz, L, d = xs.shape
    tpb = L // ROW_TILE
    nt = bsz * tpb
    cap = _slot_cap(n_experts)
    full = lambda a: pl.BlockSpec(a.shape, lambda i, *_: (0,) * a.ndim)
    grid_spec = pltpu.PrefetchScalarGridSpec(
        num_scalar_prefetch=4, grid=(nt,),
        in_specs=[pl.BlockSpec(memory_space=pl.ANY),
                  pl.BlockSpec((ROW_TILE, d), lambda i, *_: (i, 0)),
                  pl.BlockSpec((ROW_TILE, LANES), lambda i, *_: (i, 0)),
                  pl.BlockSpec((1, 1, LANES), lambda i, *_: (i, 0, 0)),
                  pl.BlockSpec((1, 1, LANES), lambda i, *_: (i, 0, 0)),
                  full(ltri),
                  pl.BlockSpec((1, ROW_TILE, d), lambda i, *_: (i // tpb, i % tpb, 0)),
                  pl.BlockSpec((1, 1, 6, d),
                               lambda i, *_: (i // tpb, jnp.minimum(i % tpb, 1) if has_ctx else 1, 0, 0)),
                  full(sg), full(su), full(sd), full(gpost)],
        out_specs=pl.BlockSpec((1, ROW_TILE, d), lambda i, *_: (i // tpb, i % tpb, 0)),
        scratch_shapes=[pltpu.VMEM((2, cap, d), BF16), pltpu.SemaphoreType.DMA((2,))])
    return pl.pallas_call(
        functools.partial(_combine_kernel, n_experts=n_experts),
        grid_spec=grid_spec,
        out_shape=jax.ShapeDtypeStruct((bsz, L, d), F32),
        compiler_params=_params("arbitrary"),
    )(plan["pos"], plan["off"], plan["len"], plan["tile_len"], ys_sorted, v2, gates2, plan["off_v"], plan["len_v"], ltri,
      xs, modtab, sg, su, sd, gpost)


def _moe_plan(counts, n_experts, n_blocks):
    a = (counts + (SLOT_ALIGN - 1)) // SLOT_ALIGN * SLOT_ALIGN
    rows = jnp.sum(a, axis=0)
    region = (rows + (EXPERT_BLOCK - 1)) // EXPERT_BLOCK * EXPERT_BLOCK
    region_end = jnp.cumsum(region)
    region_start = region_end - region
    pos = region_start[None, :] + jnp.cumsum(a, axis=0) - a
    off = jnp.cumsum(a, axis=1) - a
    first_row = jnp.arange(n_blocks, dtype=jnp.int32) * EXPERT_BLOCK
    block_expert = jnp.minimum(jnp.sum(region_end[None, :n_experts] <= first_row[:, None], axis=1), n_experts - 1)
    flat = lambda t: t[:, :n_experts].reshape(-1).astype(jnp.int32)
    nt = counts.shape[0]
    return {
        "pos": flat(pos), "off": flat(off), "len": flat(a // SLOT_ALIGN),
        "tile_len": (jnp.sum(a, axis=1) // SLOT_ALIGN).astype(jnp.int32),
        "tail_pos": (region_start + rows)[:n_experts].astype(jnp.int32),
        "tail_len": ((region - rows) // SLOT_ALIGN)[:n_experts].astype(jnp.int32),
        "off_v": off.reshape(nt, 1, LANES).astype(jnp.int32),
        "len_v": a.reshape(nt, 1, LANES).astype(jnp.int32),
        "block_expert": block_expert.astype(jnp.int32),
        "n_active": (region_end[-1:] // EXPERT_BLOCK).astype(jnp.int32),
    }


def _moe(v, gates, counts, xs, modtab, wg, wu, wd, layer, sg, su, sd, gpost, *, has_ctx):
    bsz, L, d = xs.shape
    n_experts = wg.shape[1]
    t = bsz * L
    nt = t // ROW_TILE
    worst = t * TOP_K + nt * n_experts * (SLOT_ALIGN - 1) + n_experts * (EXPERT_BLOCK - 1)
    n_blocks = -(-worst // EXPERT_BLOCK)
    plan = _moe_plan(counts.reshape(nt, LANES), n_experts, n_blocks)
    ltri = jnp.tril(jnp.ones((ROW_TILE, ROW_TILE), F32), -1).astype(BF16)
    v2, gates2 = v.reshape(t, d), gates.reshape(t, LANES)
    xs_sorted = _dispatch(v2, gates2, plan, ltri, n_experts=n_experts, n_rows=n_blocks * EXPERT_BLOCK)
    ys_sorted = _experts(xs_sorted, plan, wg, wu, wd, layer)
    return _combine(ys_sorted, v2, gates2, plan, ltri, xs, modtab, sg, su, sd, gpost,
                    n_experts=n_experts, has_ctx=has_ctx)


def _tables(seq, ctx_len):
    def build(rot_dim, lead, slot):
        n = rot_dim // 4
        pos = jnp.arange(seq)
        row = (pos // GRID_W).astype(F32)
        col = (pos % GRID_W).astype(F32)
        inv = ROPE_THETA ** (-jnp.arange(n, dtype=F32) / n)
        ang = jnp.concatenate([row[:, None] * inv, col[:, None] * inv], axis=-1)
        cos, sin = jnp.cos(ang), jnp.sin(ang)
        zero = jnp.zeros_like(sin)
        tail = slot - lead - rot_dim
        one_l, zero_l = jnp.ones((seq, lead), F32), jnp.zeros((seq, lead), F32)
        one_t, zero_t = jnp.ones((seq, tail), F32), jnp.zeros((seq, tail), F32)
        c = jnp.concatenate([one_l, cos, cos, one_t], axis=1)
        s_lo = jnp.concatenate([zero_l, -sin, zero, zero_t], axis=1)
        s_hi = jnp.concatenate([zero_l, zero, sin, zero_t], axis=1)
        ctx_c = jnp.ones((ctx_len, slot), F32)
        ctx_s = jnp.zeros((ctx_len, slot), F32)
        return [jnp.concatenate([ctx_c, c], axis=0), jnp.concatenate([ctx_s, s_lo], axis=0),
                jnp.concatenate([ctx_s, s_hi], axis=0)]

    hd = [jnp.concatenate([t, t], axis=1) for t in build(HEAD_DIM, 0, HEAD_DIM)]
    mla = build(QK_ROPE, QK_NOPE, LANES)
    return hd + mla


def kernel(x, c, ctx, c_ctx, w_mod, b_mod, g_pre_mix, g_post_mix, g_pre_ffn, g_post_ffn, w_in, a_q_norm, a_k_norm, conv_w, conv_b, lru_wa, lru_ba, lru_wi, lru_bi, lru_lambda, c_sink, d_q_norm, d_w_uq, d_kv_norm, d_w_ukv, w_out, router_w, router_bias, w_gate, w_up, w_down, sh_gate, sh_up, sh_down):
    bsz, seq, d = x.shape
    ctx_len = ctx.shape[1]
    depth = w_mod.shape[0]
    gw = d // 4
    lru_w = conv_w.shape[-1]
    q_lora = d_q_norm.shape[-1]
    kv_lora = d_kv_norm.shape[-1]
    n_experts = router_w.shape[-1]
    a_heads = gw // HEAD_DIM
    a_kv = (w_in.shape[-1] - (2 * gw + 2 * lru_w + q_lora + kv_lora + QK_ROPE)) // (4 * HEAD_DIM)
    mla_scale = (QK_NOPE + QK_ROPE) ** -0.5
    v_dim = gw // D_HEADS
    assert ctx_len == ROW_TILE and a_kv == 2 and a_heads == 4 and n_experts <= LANES

    cond = jnp.zeros((16, d), F32).at[:bsz].set(c).at[bsz].set(c_ctx)
    mod = _modulation(cond, w_mod, b_mod).reshape(depth, 16, 6, d)
    tabs = _tables(seq, ctx_len)
    seg = jnp.kron(jnp.eye(a_heads, dtype=F32), jnp.ones((HEAD_DIM, HEAD_DIM), F32)).astype(BF16)

    xs = jnp.concatenate([ctx, x], axis=1)
    for l in range(depth):
        modtab = jnp.stack([jnp.broadcast_to(mod[l, bsz], (bsz, 6, d)), mod[l, :bsz]], axis=1)

        dq_end = 2 * gw + 2 * lru_w + 4 * a_kv * HEAD_DIM + q_lora
        zeros = lambda n: jnp.zeros((d, n), F32)
        w_in_p = jnp.concatenate([w_in[l][:, :dq_end], zeros(2 * LANES - q_lora),
                                  w_in[l][:, dq_end:], zeros(LANES - QK_ROPE)], axis=1).astype(BF16)

        qk = QK_NOPE + QK_ROPE
        wuq = d_w_uq[l].reshape(q_lora, D_HEADS, qk)
        wuq = jnp.pad(wuq, ((0, 2 * LANES - q_lora), (0, 0), (0, LANES - qk))).reshape(2 * LANES, D_HEADS * LANES)
        wukv = d_w_ukv[l].reshape(kv_lora, D_HEADS, QK_NOPE + v_dim)
        wk = jnp.pad(wukv[:, :, :QK_NOPE], ((0, 0), (0, 0), (0, LANES - QK_NOPE))).reshape(kv_lora, D_HEADS * LANES)
        wv = wukv[:, :, QK_NOPE:].reshape(kv_lora, D_HEADS * v_dim)
        wukv_p = jnp.concatenate([wk, wv], axis=1).astype(BF16)
        dqn = jnp.pad(d_q_norm[l], (0, 2 * LANES - q_lora)).reshape(1, 2 * LANES)

        qa, ka, va, bx, bg, qc, kc, vc, qd, kd, vd = _inproj(
            xs, modtab, g_pre_mix[l].reshape(1, d), w_in_p, seg, tabs,
            jnp.tile(a_q_norm[l], a_heads).reshape(1, gw), jnp.tile(a_k_norm[l], a_kv).reshape(1, gw // 2),
            dqn, wuq.astype(BF16), d_kv_norm[l].reshape(1, kv_lora), wukv_p,
            q_lora=q_lora, mla_scale=mla_scale, ctx_len=ctx_len)

        need_ctx = l < depth - 1
        ya, yd, yc = _attention(qa, ka, va, qd, kd, vd, qc, kc, vc, c_sink[l], ctx_len=ctx_len, need_ctx=need_ctx)

        blocks = lru_wa.shape[2]
        bdiag = lambda wts: jnp.stack([jax.scipy.linalg.block_diag(*[wts[dd, h] for h in range(blocks)])
                                       for dd in range(2)]).astype(BF16)
        hf, hb = _lru(bx, conv_w[l], conv_b[l].reshape(1, lru_w), bdiag(lru_wa[l]), lru_ba[l].reshape(2, 1, lru_w),
                      bdiag(lru_wi[l]), lru_bi[l].reshape(2, 1, lru_w), lru_lambda[l].reshape(2, 1, lru_w),
                      ctx_len=ctx_len)

        rw_t = router_w[l].T.astype(BF16)
        xs_mid, v, gates, counts = _outproj(ya, hf, hb, bg, yc, yd, xs, modtab, w_out[l].astype(BF16),
                                            g_post_mix[l].reshape(1, d), g_pre_ffn[l].reshape(1, d),
                                            rw_t, router_bias[l].reshape(n_experts, 1), need_ctx=need_ctx)
        xs = _moe(v, gates, counts, xs_mid, modtab, w_gate, w_up, w_down, l,
                  sh_gate[l], sh_up[l], sh_down[l], g_post_ffn[l].reshape(1, d), has_ctx=need_ctx)
    return xs
```
